```python
import math
import jax, jax.numpy as jnp
from jax import lax
import numpy as np

D_MODEL = 2048
BATCH = 2
SEQ = 16384
DEPTH = 2
DEC_BATCH = 8
DEC_SEQ = 2048
PAST_LEN = 128

EPS = 1e-6
D_MIX = D_MODEL
MLA_HEADS = 8
MLA_DN = 128
MLA_DR = 64
MLA_DV = 128
Q_LORA = 512
KV_LORA = 512
ROPE_THETA = 10000.0
MLA_SCALE = 1.0 / math.sqrt(MLA_DN + MLA_DR)
Q_BLOCK = 128
GQA_HEADS = 8
GQA_KV_HEADS = 2
GQA_GROUP = GQA_HEADS // GQA_KV_HEADS
GQA_DH = 128
WINDOW = 128
W_BLOCK = 128
GQA_SCALE = 1.0 / math.sqrt(GQA_DH)
NUM_BUCKETS = 32
MAX_DISTANCE = 128
P_QLAT = Q_LORA
P_KVLAT = KV_LORA
P_KROPE = MLA_DR
P_GQ = GQA_HEADS * GQA_DH
P_GK = GQA_KV_HEADS * GQA_DH
P_GV = GQA_KV_HEADS * GQA_DH
P_IN = P_QLAT + P_KVLAT + P_KROPE + P_GQ + P_GK + P_GV
OUT_A = MLA_HEADS * MLA_DV
OUT_B = GQA_HEADS * GQA_DH
D_FF = 5632
N_EXPERTS = 8
TOP_K = 2
N_DENSE = (DEPTH + 1) // 2
N_MOE = DEPTH // 2
NEG_BIG = -1e30

kernel_name = "hybrid_mla_swa_encoder_adaln"


def rmsnorm(x, g):
    xf = x.astype(jnp.float32)
    y = xf * lax.rsqrt(jnp.mean(xf * xf, axis=-1, keepdims=True) + EPS)
    return (y * g.astype(jnp.float32)).astype(x.dtype)


def rope_tables(S, dtype):
    pos = jnp.arange(S, dtype=jnp.float32)
    inv = 1.0 / (ROPE_THETA ** (jnp.arange(0, MLA_DR, 2, dtype=jnp.float32) / MLA_DR))
    ang = pos[:, None] * inv[None, :]
    return jnp.cos(ang).astype(dtype), jnp.sin(ang).astype(dtype)


def apply_rope(x, cos, sin):
    x1, x2 = jnp.split(x, 2, axis=-1)
    return jnp.concatenate([x1 * cos - x2 * sin, x1 * sin + x2 * cos], axis=-1)


def t5_bucket(rel):
    nb = NUM_BUCKETS // 2
    ret = (rel > 0).astype(jnp.int32) * nb
    n = jnp.abs(rel)
    max_exact = nb // 2
    nf = jnp.maximum(n, 1).astype(jnp.float32)
    large = max_exact + (jnp.log(nf / max_exact) / math.log(MAX_DISTANCE / max_exact)
                         * (nb - max_exact)).astype(jnp.int32)
    large = jnp.minimum(large, nb - 1)
    return ret + jnp.where(n < max_exact, n, large)


def mla_attention(q_nope, q_rope, k_nope, k_rope, v):
    B, S, H, _ = q_nope.shape
    nb = S // Q_BLOCK

    def block(args):
        qn, qr = args
        s = (jnp.einsum('bqhd,bkhd->bhqk', qn, k_nope)
             + jnp.einsum('bqhr,bkr->bhqk', qr, k_rope))
        p = jax.nn.softmax(s.astype(jnp.float32) * MLA_SCALE, axis=-1).astype(v.dtype)
        return jnp.einsum('bhqk,bkhd->bqhd', p, v)

    qn_b = q_nope.reshape(B, nb, Q_BLOCK, H, MLA_DN).swapaxes(0, 1)
    qr_b = q_rope.reshape(B, nb, Q_BLOCK, H, MLA_DR).swapaxes(0, 1)
    o = lax.map(block, (qn_b, qr_b))
    return o.swapaxes(0, 1).reshape(B, S, H * MLA_DV)


def window_attention(q, k, v, sink, rel_bias):
    B, S = q.shape[0], q.shape[1]
    nb = S // W_BLOCK
    qb = q.reshape(B, nb, W_BLOCK, GQA_KV_HEADS, GQA_GROUP, GQA_DH)

    def band(t):
        tp = jnp.pad(t, ((0, 0), (W_BLOCK, W_BLOCK), (0, 0), (0, 0)))
        tp = tp.reshape(B, nb + 2, W_BLOCK, GQA_KV_HEADS, GQA_DH)
        return jnp.concatenate([tp[:, :-2], tp[:, 1:-1], tp[:, 2:]], axis=2)

    kb, vb = band(k), band(v)
    s = jnp.einsum('bnqkgd,bnjkd->bnkgqj', qb, kb).astype(jnp.float32) * GQA_SCALE
    qpos = jnp.arange(W_BLOCK, dtype=jnp.int32)
    jpos = jnp.arange(3 * W_BLOCK, dtype=jnp.int32)
    rel = jpos[None, :] - W_BLOCK - qpos[:, None]
    bias = rel_bias[t5_bucket(rel)].astype(jnp.float32)
    bias = bias.transpose(2, 0, 1).reshape(GQA_KV_HEADS, GQA_GROUP, W_BLOCK, 3 * W_BLOCK)
    key_abs = jnp.arange(nb, dtype=jnp.int32)[:, None] * W_BLOCK - W_BLOCK + jpos[None, :]
    valid = (jnp.abs(rel) <= WINDOW)[None] & ((key_abs >= 0) & (key_abs < S))[:, None, :]
    s = jnp.where(valid[None, :, None, None], s + bias[None, None], NEG_BIG)
    sk = sink.astype(jnp.float32).reshape(GQA_KV_HEADS, GQA_GROUP)[None, None, :, :, None, None]
    m = jnp.maximum(jnp.max(s, axis=-1, keepdims=True), sk)
    e = jnp.exp(s - m)
    p = e / (jnp.sum(e, axis=-1, keepdims=True) + jnp.exp(sk - m))
    o = jnp.einsum('bnkgqj,bnjkd->bnqkgd', p.astype(v.dtype), vb)
    return o.reshape(B, S, GQA_HEADS * GQA_DH)


def token_mixer(h, cos, sin, rel_bias, w_in, g_q_lat, w_uq, g_kv_lat, w_ukv, sink, g_out_a, g_out_b, w_out):
    B, S, _ = h.shape
    z = h @ w_in
    offs = np.cumsum([P_QLAT, P_KVLAT, P_KROPE, P_GQ, P_GK]).tolist()
    cq, ckv, kr, gq, gk, gv = jnp.split(z, offs, axis=-1)
    q = (rmsnorm(cq, g_q_lat) @ w_uq).reshape(B, S, MLA_HEADS, MLA_DN + MLA_DR)
    q_nope, q_rope = q[..., :MLA_DN], q[..., MLA_DN:]
    kv = (rmsnorm(ckv, g_kv_lat) @ w_ukv).reshape(B, S, MLA_HEADS, MLA_DN + MLA_DV)
    k_nope, v_a = kv[..., :MLA_DN], kv[..., MLA_DN:]
    q_rope = apply_rope(q_rope, cos[None, :, None, :], sin[None, :, None, :])
    k_rope = apply_rope(kr, cos[None], sin[None])
    o_a = mla_attention(q_nope, q_rope, k_nope, k_rope, v_a)
    o_b = window_attention(gq.reshape(B, S, GQA_HEADS, GQA_DH),
                           gk.reshape(B, S, GQA_KV_HEADS, GQA_DH),
                           gv.reshape(B, S, GQA_KV_HEADS, GQA_DH), sink, rel_bias)
    o = jnp.concatenate([rmsnorm(o_a, g_out_a), rmsnorm(o_b, g_out_b)], axis=-1)
    return o @ w_out


def swiglu(h, wg, wu, wd):
    return (jax.nn.silu(h @ wg) * (h @ wu)) @ wd


def moe_swiglu(h, wr, wg, wu, wd):
    B, S, D = h.shape
    t = h.reshape(B * S, D)
    logits = (t @ wr).astype(jnp.float32)
    top_v, top_i = lax.top_k(logits, TOP_K)
    w = jax.nn.softmax(top_v, axis=-1)
    gates = jnp.sum(jax.nn.one_hot(top_i, N_EXPERTS, dtype=jnp.float32) * w[..., None], axis=1).astype(h.dtype)
    y = jnp.zeros_like(t)
    for e in range(N_EXPERTS):
        y = y + gates[:, e:e + 1] * swiglu(t, wg[e], wu[e], wd[e])
    return y.reshape(B, S, D)


def trunk(x, c, rel_bias, w_ada, b_ada, g_norm_mix, g_norm_ffn, w_in, g_q_lat, w_uq, g_kv_lat, w_ukv,
          sink, g_out_a, g_out_b, w_out, w_gate_d, w_up_d, w_down_d, w_router, w_gate_e, w_up_e,
          w_down_e, g_final):
    S = x.shape[1]
    cos, sin = rope_tables(S, x.dtype)
    cs = jax.nn.silu(c)
    for l in range(DEPTH):
        mod = cs @ w_ada[l] + b_ada[l]
        sh1, sc1, g1, sh2, sc2, g2 = [m[:, None, :] for m in jnp.split(mod, 6, axis=-1)]
        h = rmsnorm(x, g_norm_mix[l]) * (1 + sc1) + sh1
        x = x + g1 * token_mixer(h, cos, sin, rel_bias, w_in[l], g_q_lat[l], w_uq[l], g_kv_lat[l],
                                 w_ukv[l], sink[l], g_out_a[l], g_out_b[l], w_out[l])
        h = rmsnorm(x, g_norm_ffn[l]) * (1 + sc2) + sh2
        i = l // 2
        if l % 2 == 0:
            f = swiglu(h, w_gate_d[i], w_up_d[i], w_down_d[i])
        else:
            f = moe_swiglu(h, w_router[i], w_gate_e[i], w_up_e[i], w_down_e[i])
        x = x + g2 * f
    return rmsnorm(x, g_final)


def setup_inputs(seed: int = 0) -> dict:
    key = jax.random.key(seed)
    ks = jax.random.split(key, 26)
    f32 = jnp.float32

    def nrm(k, shape, scale):
        return scale * jax.random.normal(k, shape, f32)

    def gain(k, shape):
        return 1.0 + nrm(k, shape, 0.05)

    return {
        "x_prompt": nrm(ks[0], (BATCH, SEQ, D_MODEL), 1.0),
        "x_sample": nrm(ks[1], (DEC_BATCH, DEC_SEQ, D_MODEL), 1.0),
        "c_prompt": nrm(ks[2], (BATCH, D_MODEL), 1.0),
        "c_sample": nrm(ks[3], (DEC_BATCH, D_MODEL), 1.0),
        "rel_bias": nrm(ks[4], (NUM_BUCKETS, GQA_HEADS), 0.5),
        "w_ada": nrm(ks[5], (DEPTH, D_MODEL, 6 * D_MODEL), 0.5 * D_MODEL ** -0.5),
        "b_ada": nrm(ks[6], (DEPTH, 6 * D_MODEL), 0.02),
        "g_norm_mix": gain(ks[7], (DEPTH, D_MODEL)),
        "g_norm_ffn": gain(ks[8], (DEPTH, D_MODEL)),
        "w_in": nrm(ks[9], (DEPTH, D_MODEL, P_IN), D_MODEL ** -0.5),
        "g_q_lat": gain(ks[10], (DEPTH, Q_LORA)),
        "w_uq": nrm(ks[11], (DEPTH, Q_LORA, MLA_HEADS * (MLA_DN + MLA_DR)), Q_LORA ** -0.5),
        "g_kv_lat": gain(ks[12], (DEPTH, KV_LORA)),
        "w_ukv": nrm(ks[13], (DEPTH, KV_LORA, MLA_HEADS * (MLA_DN + MLA_DV)), KV_LORA ** -0.5),
        "sink": nrm(ks[14], (DEPTH, GQA_HEADS), 0.5),
        "g_out_a": gain(ks[15], (DEPTH, OUT_A)),
        "g_out_b": gain(ks[16], (DEPTH, OUT_B)),
        "w_out": nrm(ks[17], (DEPTH, D_MIX, D_MODEL), D_MIX ** -0.5),
        "w_gate_d": nrm(ks[18], (N_DENSE, D_MODEL, D_FF), D_MODEL ** -0.5),
        "w_up_d": nrm(ks[19], (N_DENSE, D_MODEL, D_FF), D_MODEL ** -0.5),
        "w_down_d": nrm(ks[20], (N_DENSE, D_FF, D_MODEL), D_FF ** -0.5),
        "w_router": nrm(ks[21], (N_MOE, D_MODEL, N_EXPERTS), D_MODEL ** -0.5),
        "w_gate_e": nrm(ks[22], (N_MOE, N_EXPERTS, D_MODEL, D_FF), D_MODEL ** -0.5),
        "w_up_e": nrm(ks[23], (N_MOE, N_EXPERTS, D_MODEL, D_FF), D_MODEL ** -0.5),
        "w_down_e": nrm(ks[24], (N_MOE, N_EXPERTS, D_FF, D_MODEL), D_FF ** -0.5),
        "g_final": gain(ks[25], (D_MODEL,)),
    }


def reference(x_prompt, x_sample, c_prompt, c_sample, rel_bias, w_ada, b_ada, g_norm_mix, g_norm_ffn,
              w_in, g_q_lat, w_uq, g_kv_lat, w_ukv, sink, g_out_a, g_out_b, w_out, w_gate_d, w_up_d,
              w_down_d, w_router, w_gate_e, w_up_e, w_down_e, g_final):
    y_prompt = trunk(x_prompt, c_prompt, rel_bias, w_ada, b_ada, g_norm_mix, g_norm_ffn, w_in, g_q_lat,
                     w_uq, g_kv_lat, w_ukv, sink, g_out_a, g_out_b, w_out, w_gate_d, w_up_d, w_down_d,
                     w_router, w_gate_e, w_up_e, w_down_e, g_final)
    y_sample = trunk(x_sample, c_sample, rel_bias, w_ada, b_ada, g_norm_mix, g_norm_ffn, w_in, g_q_lat,
                     w_uq, g_kv_lat, w_ukv, sink, g_out_a, g_out_b, w_out, w_gate_d, w_up_d, w_down_d,
                     w_router, w_gate_e, w_up_e, w_down_e, g_final)
    return (y_prompt, y_sample)
```

```python
import functools
import math

import jax
import jax.numpy as jnp
from jax import lax
from jax.experimental import pallas as pl
from jax.experimental.pallas import tpu as pltpu

F32 = jnp.float32
BF16 = jnp.bfloat16

D_MODEL = 2048
DEPTH = 2
EPS = 1e-6
MLA_HEADS = 8
MLA_DN = 128
MLA_DR = 64
MLA_DV = 128
Q_LORA = 512
KV_LORA = 512
ROPE_THETA = 10000.0
MLA_SCALE = 1.0 / math.sqrt(MLA_DN + MLA_DR)
GQA_HEADS = 8
GQA_KV_HEADS = 2
GQA_GROUP = GQA_HEADS // GQA_KV_HEADS
GQA_DH = 128
WINDOW = 128
GQA_SCALE = 1.0 / math.sqrt(GQA_DH)
NUM_BUCKETS = 32
MAX_DISTANCE = 128
N_EXPERTS = 8
NEG_BIG = -1e30

LANES = 128
V7X_VMEM_LIMIT = 56 * 1024 * 1024

QK_PAD = 2 * LANES
Z_CQ = 0
Z_CKV = Z_CQ + Q_LORA
Z_KRA = Z_CKV + KV_LORA
Z_KRB = Z_KRA + LANES
Z_GQ = Z_KRB + LANES
Z_GK = Z_GQ + GQA_HEADS * GQA_DH
Z_GV = Z_GK + GQA_KV_HEADS * GQA_DH
Z_END = Z_GV + GQA_KV_HEADS * GQA_DH


def _params(*sem):
    return pltpu.CompilerParams(dimension_semantics=sem, vmem_limit_bytes=V7X_VMEM_LIMIT)


def _rms(x, g):
    return x * lax.rsqrt(jnp.mean(x * x, axis=-1, keepdims=True) + EPS) * g


def _silu(a):
    return a * (1.0 / (1.0 + jnp.exp(-a)))


def _const_spec(shape):
    nd = len(shape)
    return pl.BlockSpec(shape, lambda *_: (0,) * nd)


def _ada_kernel(c_ref, w_ref, b_ref, o_ref):
    cs = _silu(c_ref[...])
    o_ref[...] = jnp.dot(cs, w_ref[...], preferred_element_type=F32,
                         precision=lax.Precision.HIGHEST) + b_ref[...]


def _ada_mod(c_all, w_ada, b_ada):
    rows = c_all.shape[0]
    n = w_ada.shape[-1]
    tn = 1024
    return pl.pallas_call(
        _ada_kernel,
        grid=(DEPTH, n // tn),
        in_specs=[
            pl.BlockSpec((rows, D_MODEL), lambda l, j: (0, 0)),
            pl.BlockSpec((None, D_MODEL, tn), lambda l, j: (l, 0, j)),
            pl.BlockSpec((None, 1, tn), lambda l, j: (l, 0, j)),
        ],
        out_specs=pl.BlockSpec((None, rows, tn), lambda l, j: (l, 0, j)),
        out_shape=jax.ShapeDtypeStruct((DEPTH, rows, n), F32),
        compiler_params=_params("parallel", "parallel"),
        name="ada_mod",
    )(c_all, w_ada, b_ada.reshape(DEPTH, 1, n))


def _premix_kernel(x_ref, mod_ref, gn_ref, cos_ref, sin_ref, win_ref, gq_ref, wuqa_ref, wuqb_ref,
                   gkv_ref, wukv_ref, q_ref, k_ref, v_ref, wq_ref, wk_ref, wv_ref):
    mod = mod_ref[...]
    h = _rms(x_ref[...], gn_ref[...]) * (1.0 + mod[1:2, :]) + mod[0:1, :]
    z = jnp.dot(h.astype(BF16), win_ref[...], preferred_element_type=F32)
    cqn = _rms(z[:, Z_CQ:Z_CKV], gq_ref[...]).astype(BF16)
    ckvn = _rms(z[:, Z_CKV:Z_KRA], gkv_ref[...]).astype(BF16)
    cos_t = cos_ref[...]
    sin_t = sin_ref[...]
    kr = (z[:, Z_KRA:Z_KRB] * cos_t + z[:, Z_KRB:Z_GQ] * sin_t).astype(BF16)
    wq_ref[...] = (z[:, Z_GQ:Z_GK] * GQA_SCALE).astype(BF16)
    wk_ref[...] = z[:, Z_GK:Z_GV].astype(BF16)
    wv_ref[...] = z[:, Z_GV:Z_END].astype(BF16)
    qa = jnp.dot(cqn, wuqa_ref[...], preferred_element_type=F32)
    qb = jnp.dot(cqn, wuqb_ref[...], preferred_element_type=F32)
    kv = jnp.dot(ckvn, wukv_ref[...], preferred_element_type=F32)
    for hh in range(MLA_HEADS):
        a0 = hh * QK_PAD
        q_ref[hh, :, 0:LANES] = (qa[:, a0:a0 + LANES] * MLA_SCALE).astype(BF16)
        q_rope = qa[:, a0 + LANES:a0 + QK_PAD] * cos_t + qb[:, hh * LANES:(hh + 1) * LANES] * sin_t
        q_ref[hh, :, LANES:QK_PAD] = (q_rope * MLA_SCALE).astype(BF16)
        k_ref[hh, :, 0:LANES] = kv[:, a0:a0 + LANES].astype(BF16)
        k_ref[hh, :, LANES:QK_PAD] = kr
        v_ref[hh] = kv[:, a0 + LANES:a0 + QK_PAD].astype(BF16)


def _premix(x, mod, gn, cos_t, sin_t, w_in_p, g_q, w_uq_a, w_uq_b, g_kv, w_ukv, *, seq, tm):
    t = x.shape[0]
    nt = seq // tm
    row = lambda i: (i, 0)
    hrow = lambda i: (0, i, 0)
    return pl.pallas_call(
        _premix_kernel,
        grid=(t // tm,),
        in_specs=[
            pl.BlockSpec((tm, D_MODEL), row),
            pl.BlockSpec((None, 6, D_MODEL), lambda i: (i // nt, 0, 0)),
            _const_spec((1, D_MODEL)),
            pl.BlockSpec((tm, LANES), lambda i: (i % nt, 0)),
            pl.BlockSpec((tm, LANES), lambda i: (i % nt, 0)),
            _const_spec(w_in_p.shape),
            _const_spec((1, Q_LORA)),
            _const_spec(w_uq_a.shape),
            _const_spec(w_uq_b.shape),
            _const_spec((1, KV_LORA)),
            _const_spec(w_ukv.shape),
        ],
        out_specs=[
            pl.BlockSpec((MLA_HEADS, tm, QK_PAD), hrow),
            pl.BlockSpec((MLA_HEADS, tm, QK_PAD), hrow),
            pl.BlockSpec((MLA_HEADS, tm, MLA_DV), hrow),
            pl.BlockSpec((tm, GQA_HEADS * GQA_DH), row),
            pl.BlockSpec((tm, GQA_KV_HEADS * GQA_DH), row),
            pl.BlockSpec((tm, GQA_KV_HEADS * GQA_DH), row),
        ],
        out_shape=[
            jax.ShapeDtypeStruct((MLA_HEADS, t, QK_PAD), BF16),
            jax.ShapeDtypeStruct((MLA_HEADS, t, QK_PAD), BF16),
            jax.ShapeDtypeStruct((MLA_HEADS, t, MLA_DV), BF16),
            jax.ShapeDtypeStruct((t, GQA_HEADS * GQA_DH), BF16),
            jax.ShapeDtypeStruct((t, GQA_KV_HEADS * GQA_DH), BF16),
            jax.ShapeDtypeStruct((t, GQA_KV_HEADS * GQA_DH), BF16),
        ],
        compiler_params=_params("parallel"),
        name="premix",
    )(x, mod, gn, cos_t, sin_t, w_in_p, g_q, w_uq_a, w_uq_b, g_kv, w_ukv)


def _mla_kernel(q_ref, k_ref, v_ref, o_ref, *, tk, nk):
    q = q_ref[...]
    tq = q.shape[0]

    def body(j, carry):
        m, l, acc = carry
        off = pl.multiple_of(j * tk, tk)
        k = k_ref[pl.ds(off, tk), :]
        v = v_ref[pl.ds(off, tk), :]
        s = lax.dot_general(q, k, (((1,), (1,)), ((), ())), preferred_element_type=F32)
        m_new = jnp.maximum(m, jnp.max(s, axis=1, keepdims=True))
        alpha = jnp.exp(m - m_new)
        p = jnp.exp(s - m_new)
        l = alpha * l + jnp.sum(p, axis=1, keepdims=True)
        acc = alpha * acc + jnp.dot(p.astype(BF16), v, preferred_element_type=F32)
        return m_new, l, acc

    init = (jnp.full((tq, 1), -jnp.inf, F32), jnp.zeros((tq, 1), F32), jnp.zeros((tq, MLA_DV), F32))
    _, l, acc = lax.fori_loop(0, nk, body, init)
    o_ref[...] = (acc * (1.0 / l)).astype(BF16)


def _mla(q, k, v, *, batch, seq, tq, tk):
    t = batch * seq
    nq = seq // tq
    return pl.pallas_call(
        functools.partial(_mla_kernel, tk=tk, nk=seq // tk),
        grid=(batch, MLA_HEADS, nq),
        in_specs=[
            pl.BlockSpec((None, tq, QK_PAD), lambda b, h, i: (h, b * nq + i, 0)),
            pl.BlockSpec((None, seq, QK_PAD), lambda b, h, i: (h, b, 0)),
            pl.BlockSpec((None, seq, MLA_DV), lambda b, h, i: (h, b, 0)),
        ],
        out_specs=pl.BlockSpec((tq, MLA_DV), lambda b, h, i: (b * nq + i, h)),
        out_shape=jax.ShapeDtypeStruct((t, MLA_HEADS * MLA_DV), BF16),
        compiler_params=_params("parallel", "parallel", "parallel"),
        name="mla_attn",
    )(q, k, v)


def _window_kernel(q_ref, kp_ref, kc_ref, kn_ref, vp_ref, vc_ref, vn_ref, bias_ref, sink_ref, o_ref,
                   kcat, vcat, *, rows, nt):
    i = pl.program_id(0)
    nsub = rows // WINDOW
    kcat[0:WINDOW, :] = kp_ref[...]
    kcat[WINDOW:WINDOW + rows, :] = kc_ref[...]
    kcat[WINDOW + rows:, :] = kn_ref[...]
    vcat[0:WINDOW, :] = vp_ref[...]
    vcat[WINDOW:WINDOW + rows, :] = vc_ref[...]
    vcat[WINDOW + rows:, :] = vn_ref[...]
    first_tile = (i % nt) == 0
    last_tile = (i % nt) == nt - 1
    col = lax.broadcasted_iota(jnp.int32, (1, 3 * WINDOW), 1)

    def sub(n, carry):
        r0 = pl.multiple_of(n * WINDOW, WINDOW)
        lo = jnp.where(jnp.logical_and(first_tile, n == 0), WINDOW, 0)
        hi = jnp.where(jnp.logical_and(last_tile, n == nsub - 1), 2 * WINDOW, 3 * WINDOW)
        edge = jnp.where(jnp.logical_or(col < lo, col >= hi), NEG_BIG, 0.0).astype(F32)
        for kh in range(GQA_KV_HEADS):
            qall = q_ref[pl.ds(r0, WINDOW), kh * GQA_GROUP * GQA_DH:(kh + 1) * GQA_GROUP * GQA_DH]
            qs = jnp.concatenate([qall[:, g * GQA_DH:(g + 1) * GQA_DH] for g in range(GQA_GROUP)], axis=0)
            kk = kcat[pl.ds(r0, 3 * WINDOW), kh * GQA_DH:(kh + 1) * GQA_DH]
            vv = vcat[pl.ds(r0, 3 * WINDOW), kh * GQA_DH:(kh + 1) * GQA_DH]
            s = lax.dot_general(qs, kk, (((1,), (1,)), ((), ())), preferred_element_type=F32)
            s = s + bias_ref[kh] + edge
            sk = sink_ref[kh]
            m = jnp.maximum(jnp.max(s, axis=1, keepdims=True), sk)
            e = jnp.exp(s - m)
            denom = jnp.sum(e, axis=1, keepdims=True) + jnp.exp(sk - m)
            o = jnp.dot(e.astype(BF16), vv, preferred_element_type=F32) * (1.0 / denom)
            for g in range(GQA_GROUP):
                hcol = (kh * GQA_GROUP + g) * GQA_DH
                o_ref[pl.ds(r0, WINDOW), hcol:hcol + GQA_DH] = o[g * WINDOW:(g + 1) * WINDOW].astype(BF16)
        return carry

    lax.fori_loop(0, nsub, sub, 0)


def _window(gq, gk, gv, bias, sink_col, *, seq, rows):
    t = gq.shape[0]
    nt = seq // rows
    rb = rows // WINDOW
    nblk = t // WINDOW
    kvw = GQA_KV_HEADS * GQA_DH
    prev = lambda i: (jnp.maximum(i * rb - 1, 0), 0)
    nxt = lambda i: (jnp.minimum((i + 1) * rb, nblk - 1), 0)
    cur = lambda i: (i, 0)
    return pl.pallas_call(
        functools.partial(_window_kernel, rows=rows, nt=nt),
        grid=(t // rows,),
        in_specs=[
            pl.BlockSpec((rows, GQA_HEADS * GQA_DH), cur),
            pl.BlockSpec((WINDOW, kvw), prev),
            pl.BlockSpec((rows, kvw), cur),
            pl.BlockSpec((WINDOW, kvw), nxt),
            pl.BlockSpec((WINDOW, kvw), prev),
            pl.BlockSpec((rows, kvw), cur),
            pl.BlockSpec((WINDOW, kvw), nxt),
            _const_spec(bias.shape),
            _const_spec(sink_col.shape),
        ],
        out_specs=pl.BlockSpec((rows, GQA_HEADS * GQA_DH), cur),
        out_shape=jax.ShapeDtypeStruct((t, GQA_HEADS * GQA_DH), BF16),
        scratch_shapes=[pltpu.VMEM((rows + 2 * WINDOW, kvw), BF16),
                        pltpu.VMEM((rows + 2 * WINDOW, kvw), BF16)],
        compiler_params=_params("parallel"),
        name="window_attn",
    )(gq, gk, gk, gk, gv, gv, gv, bias, sink_col)


def _postmix_kernel(oa_ref, ob_ref, x_ref, mod_ref, ga_ref, gb_ref, wa_ref, wb_ref, gf_ref, *rest, moe):
    if moe:
        wr_ref, x1_ref, h2_ref, gates_ref = rest
    else:
        x1_ref, h2_ref = rest
    mod = mod_ref[...]
    na = _rms(oa_ref[...].astype(F32), ga_ref[...]).astype(BF16)
    nb = _rms(ob_ref[...].astype(F32), gb_ref[...]).astype(BF16)
    mix = (jnp.dot(na, wa_ref[...], preferred_element_type=F32)
           + jnp.dot(nb, wb_ref[...], preferred_element_type=F32))
    x1 = x_ref[...] + mod[2:3, :] * mix
    x1_ref[...] = x1
    h2 = _rms(x1, gf_ref[...]) * (1.0 + mod[4:5, :]) + mod[3:4, :]
    h2_ref[...] = h2.astype(BF16)
    if moe:
        logits = jnp.dot(h2, wr_ref[...], preferred_element_type=F32, precision=lax.Precision.HIGHEST)
        lane = lax.broadcasted_iota(jnp.int32, logits.shape, 1)
        lg = jnp.where(lane < N_EXPERTS, logits, -jnp.inf)
        m1 = jnp.max(lg, axis=1, keepdims=True)
        i1 = jnp.min(jnp.where(lg == m1, lane, LANES), axis=1, keepdims=True)
        lg2 = jnp.where(lane == i1, -jnp.inf, lg)
        m2 = jnp.max(lg2, axis=1, keepdims=True)
        i2 = jnp.min(jnp.where(lg2 == m2, lane, LANES), axis=1, keepdims=True)
        e2 = jnp.exp(m2 - m1)
        w1 = 1.0 / (1.0 + e2)
        w2 = e2 * w1
        gates_ref[...] = jnp.where(lane == i1, w1, 0.0) + jnp.where(lane == i2, w2, 0.0)


def _postmix(oa, ob, x, mod, ga, gb, wa, wb, gf, wr, *, seq, tm):
    t = x.shape[0]
    nt = seq // tm
    moe = wr is not None
    row = lambda i: (i, 0)
    half = oa.shape[1]
    in_specs = [
        pl.BlockSpec((tm, half), row),
        pl.BlockSpec((tm, half), row),
        pl.BlockSpec((tm, D_MODEL), row),
        pl.BlockSpec((None, 6, D_MODEL), lambda i: (i // nt, 0, 0)),
        _const_spec((1, half)),
        _const_spec((1, half)),
        _const_spec(wa.shape),
        _const_spec(wb.shape),
        _const_spec((1, D_MODEL)),
    ]
    out_specs = [pl.BlockSpec((tm, D_MODEL), row), pl.BlockSpec((tm, D_MODEL), row)]
    out_shape = [jax.ShapeDtypeStruct((t, D_MODEL), F32), jax.ShapeDtypeStruct((t, D_MODEL), BF16)]
    args = [oa, ob, x, mod, ga, gb, wa, wb, gf]
    if moe:
        in_specs.append(_const_spec(wr.shape))
        out_specs.append(pl.BlockSpec((tm, LANES), row))
        out_shape.append(jax.ShapeDtypeStruct((t, LANES), F32))
        args.append(wr)
    return pl.pallas_call(
        functools.partial(_postmix_kernel, moe=moe),
        grid=(t // tm,),
        in_specs=in_specs,
        out_specs=out_specs,
        out_shape=out_shape,
        compiler_params=_params("parallel"),
        name="postmix_moe" if moe else "postmix",
    )(*args)


def _ffn_kernel(x1_ref, h_ref, mod_ref, wg_ref, wu_ref, wd_ref, gfin_ref, *rest, moe, final):
    if moe:
        gates_ref, o_ref, acc_ref = rest
    else:
        o_ref, acc_ref = rest
    e = pl.program_id(1)
    f = pl.program_id(2)

    @pl.when(jnp.logical_and(e == 0, f == 0))
    def _():
        acc_ref[...] = jnp.zeros_like(acc_ref)

    h = h_ref[...]
    a = jnp.dot(h, wg_ref[...], preferred_element_type=F32)
    u = jnp.dot(h, wu_ref[...], preferred_element_type=F32)
    mid = _silu(a) * u
    if moe:
        gates = gates_ref[...]
        lane = lax.broadcasted_iota(jnp.int32, gates.shape, 1)
        mid = mid * jnp.sum(jnp.where(lane == e, gates, 0.0), axis=1, keepdims=True)
    acc_ref[...] += jnp.dot(mid.astype(BF16), wd_ref[...], preferred_element_type=F32)

    @pl.when(jnp.logical_and(e == pl.num_programs(1) - 1, f == pl.num_programs(2) - 1))
    def _():
        y = x1_ref[...] + mod_ref[...][5:6, :] * acc_ref[...]
        if final:
            y = _rms(y, gfin_ref[...])
        o_ref[...] = y


def _ffn(x1, h2, mod, wg, wu, wd, gfin, gates, *, seq, tm, tf, final):
    t = x1.shape[0]
    nt = seq // tm
    ne, _, dff = wg.shape
    moe = gates is not None
    row = lambda i, e, f: (i, 0)
    in_specs = [
        pl.BlockSpec((tm, D_MODEL), row),
        pl.BlockSpec((tm, D_MODEL), row),
        pl.BlockSpec((None, 6, D_MODEL), lambda i, e, f: (i // nt, 0, 0)),
        pl.BlockSpec((None, D_MODEL, tf), lambda i, e, f: (e, 0, f)),
        pl.BlockSpec((None, D_MODEL, tf), lambda i, e, f: (e, 0, f)),
        pl.BlockSpec((None, tf, D_MODEL), lambda i, e, f: (e, f, 0)),
        pl.BlockSpec((1, D_MODEL), lambda i, e, f: (0, 0)),
    ]
    args = [x1, h2, mod, wg, wu, wd, gfin]
    if moe:
        in_specs.append(pl.BlockSpec((tm, LANES), row))
        args.append(gates)
    return pl.pallas_call(
        functools.partial(_ffn_kernel, moe=moe, final=final),
        grid=(t // tm, ne, dff // tf),
        in_specs=in_specs,
        out_specs=pl.BlockSpec((tm, D_MODEL), row),
        out_shape=jax.ShapeDtypeStruct((t, D_MODEL), F32),
        scratch_shapes=[pltpu.VMEM((tm, D_MODEL), F32)],
        compiler_params=_params("parallel", "arbitrary", "arbitrary"),
        name="ffn_moe" if moe else "ffn_dense",
    )(*args)


def _rope_tables(seq):
    pos = jnp.arange(seq, dtype=F32)
    inv = 1.0 / (ROPE_THETA ** (jnp.arange(0, MLA_DR, 2, dtype=F32) / MLA_DR))
    ang = pos[:, None] * inv[None, :]
    cos, sin = jnp.cos(ang), jnp.sin(ang)
    zero = jnp.zeros((seq, LANES - MLA_DR), F32)
    return (jnp.concatenate([cos, cos, zero], axis=1), jnp.concatenate([-sin, sin, zero], axis=1))


def _t5_bucket(rel):
    nb = NUM_BUCKETS // 2
    ret = (rel > 0).astype(jnp.int32) * nb
    n = jnp.abs(rel)
    max_exact = nb // 2
    nf = jnp.maximum(n, 1).astype(F32)
    large = max_exact + (jnp.log(nf / max_exact) / math.log(MAX_DISTANCE / max_exact)
                         * (nb - max_exact)).astype(jnp.int32)
    large = jnp.minimum(large, nb - 1)
    return ret + jnp.where(n < max_exact, n, large)


def _window_bias(rel_bias):
    qpos = jnp.arange(WINDOW, dtype=jnp.int32)
    jpos = jnp.arange(3 * WINDOW, dtype=jnp.int32)
    rel = jpos[None, :] - WINDOW - qpos[:, None]
    bias = rel_bias[_t5_bucket(rel)].astype(F32).transpose(2, 0, 1)
    bias = jnp.where((jnp.abs(rel) <= WINDOW)[None], bias, NEG_BIG)
    return bias.reshape(GQA_KV_HEADS, GQA_GROUP * WINDOW, 3 * WINDOW)


def _prep_layer(w_in, w_uq, w_ukv, w_out):
    half = MLA_DR // 2
    zpad = jnp.zeros((D_MODEL, LANES - MLA_DR), F32)
    kr0 = Q_LORA + KV_LORA
    k1 = w_in[:, kr0:kr0 + half]
    k2 = w_in[:, kr0 + half:kr0 + MLA_DR]
    w_in_p = jnp.concatenate(
        [w_in[:, :kr0], k1, k2, zpad, k2, k1, zpad, w_in[:, kr0 + MLA_DR:]], axis=1).astype(BF16)
    wq = w_uq.reshape(Q_LORA, MLA_HEADS, MLA_DN + MLA_DR)
    r1 = wq[:, :, MLA_DN:MLA_DN + half]
    r2 = wq[:, :, MLA_DN + half:]
    zq = jnp.zeros((Q_LORA, MLA_HEADS, LANES - MLA_DR), F32)
    w_uq_a = jnp.concatenate([wq[:, :, :MLA_DN], r1, r2, zq], axis=2).reshape(Q_LORA, MLA_HEADS * QK_PAD)
    w_uq_b = jnp.concatenate([r2, r1, zq], axis=2).reshape(Q_LORA, MLA_HEADS * LANES)
    out_a = MLA_HEADS * MLA_DV
    return (w_in_p, w_uq_a.astype(BF16), w_uq_b.astype(BF16), w_ukv.astype(BF16),
            w_out[:out_a].astype(BF16), w_out[out_a:].astype(BF16))


def _trunk(x, mods, layers, bias, g_final, *, batch, seq):
    tm = min(512, seq)
    cos_t, sin_t = _rope_tables(seq)
    for l, p in enumerate(layers):
        q, k, v, gq, gk, gv = _premix(x, mods[l], p["g_norm_mix"], cos_t, sin_t, p["w_in_p"], p["g_q_lat"],
                                      p["w_uq_a"], p["w_uq_b"], p["g_kv_lat"], p["w_ukv"], seq=seq, tm=tm)
        oa = _mla(q, k, v, batch=batch, seq=seq, tq=min(256, seq), tk=min(512, seq))
        ob = _window(gq, gk, gv, bias, p["sink_col"], seq=seq, rows=min(512, seq))
        res = _postmix(oa, ob, x, mods[l], p["g_out_a"], p["g_out_b"], p["w_out_a"], p["w_out_b"],
                       p["g_norm_ffn"], p.get("w_router"), seq=seq, tm=tm)
        gates = res[2] if len(res) == 3 else None
        x = _ffn(res[0], res[1], mods[l], p["w_gate"], p["w_up"], p["w_down"], g_final, gates,
                 seq=seq, tm=tm, tf=512, final=(l == len(layers) - 1))
    return x


def kernel(x_prompt, x_sample, c_prompt, c_sample, rel_bias, w_ada, b_ada, g_norm_mix, g_norm_ffn, w_in, g_q_lat, w_uq, g_kv_lat, w_ukv, sink, g_out_a, g_out_b, w_out, w_gate_d, w_up_d, w_down_d, w_router, w_gate_e, w_up_e, w_down_e, g_final):
    bp, sp, _ = x_prompt.shape
    bs, ss, _ = x_sample.shape
    rows = -(-(bp + bs) // 8) * 8
    c_all = jnp.concatenate([c_prompt, c_sample, jnp.zeros((rows - bp - bs, D_MODEL), F32)], axis=0)
    mod = _ada_mod(c_all, w_ada, b_ada)
    mod_p = [mod[l, :bp].reshape(bp, 6, D_MODEL) for l in range(DEPTH)]
    mod_s = [mod[l, bp:bp + bs].reshape(bs, 6, D_MODEL) for l in range(DEPTH)]
    bias = _window_bias(rel_bias)
    layers = []
    for l in range(DEPTH):
        w_in_p, w_uq_a, w_uq_b, w_ukv_b, w_out_a, w_out_b = _prep_layer(w_in[l], w_uq[l], w_ukv[l], w_out[l])
        out_a = MLA_HEADS * MLA_DV
        p = dict(
            g_norm_mix=g_norm_mix[l][None], g_norm_ffn=g_norm_ffn[l][None],
            w_in_p=w_in_p, g_q_lat=g_q_lat[l][None], w_uq_a=w_uq_a, w_uq_b=w_uq_b,
            g_kv_lat=g_kv_lat[l][None], w_ukv=w_ukv_b,
            sink_col=jnp.repeat(sink[l].astype(F32), WINDOW).reshape(GQA_KV_HEADS, GQA_GROUP * WINDOW, 1),
            g_out_a=g_out_a[l][None], g_out_b=g_out_b[l][None], w_out_a=w_out_a, w_out_b=w_out_b,
        )
        i = l // 2
        if l % 2 == 0:
            p.update(w_gate=w_gate_d[i][None].astype(BF16), w_up=w_up_d[i][None].astype(BF16),
                     w_down=w_down_d[i][None].astype(BF16))
        else:
            wr = jnp.concatenate([w_router[i], jnp.zeros((D_MODEL, LANES - N_EXPERTS), F32)], axis=1)
            p.update(w_gate=w_gate_e[i].astype(BF16), w_up=w_up_e[i].astype(BF16),
                     w_down=w_down_e[i].astype(BF16), w_router=wr)
        layers.append(p)
    gfin = g_final[None]
    y_p = _trunk(x_prompt.reshape(bp * sp, D_MODEL), mod_p, layers, bias, gfin, batch=bp, seq=sp)
    y_s = _trunk(x_sample.reshape(bs * ss, D_MODEL), mod_s, layers, bias, gfin, batch=bs, seq=ss)
    return (y_p.reshape(bp, sp, D_MODEL), y_s.reshape(bs, ss, D_MODEL))
```

```python
import functools
import math

import jax
import jax.numpy as jnp
from jax import lax
from jax.experimental import pallas as pl
from jax.experimental.pallas import tpu as pltpu

F32 = jnp.float32
BF16 = jnp.bfloat16

D_MODEL = 2048
DEPTH = 2
EPS = 1e-6
MLA_HEADS = 8
MLA_DN = 128
MLA_DR = 64
MLA_DV = 128
Q_LORA = 512
KV_LORA = 512
ROPE_THETA = 10000.0
MLA_SCALE = 1.0 / math.sqrt(MLA_DN + MLA_DR)
MLA_QSCALE = MLA_SCALE * math.log2(math.e)
GQA_HEADS = 8
GQA_KV_HEADS = 2
GQA_GROUP = GQA_HEADS // GQA_KV_HEADS
GQA_DH = 128
WINDOW = 128
GQA_SCALE = 1.0 / math.sqrt(GQA_DH)
NUM_BUCKETS = 32
MAX_DISTANCE = 128
N_EXPERTS = 8
NEG_BIG = -1e30

LANES = 128
V7X_VMEM_LIMIT = 56 * 1024 * 1024

QK_PAD = 2 * LANES
MLA_TQ = 1024
MLA_TQ_SUB = 256
MLA_TK = 512
Z_CQ = 0
Z_CKV = Z_CQ + Q_LORA
Z_KRA = Z_CKV + KV_LORA
Z_KRB = Z_KRA + LANES
Z_GQ = Z_KRB + LANES
Z_GK = Z_GQ + GQA_HEADS * GQA_DH
Z_GV = Z_GK + GQA_KV_HEADS * GQA_DH
Z_END = Z_GV + GQA_KV_HEADS * GQA_DH


def _params(*sem):
    return pltpu.CompilerParams(dimension_semantics=sem, vmem_limit_bytes=V7X_VMEM_LIMIT)


def _rms(x, g):
    return x * lax.rsqrt(jnp.mean(x * x, axis=-1, keepdims=True) + EPS) * g


def _silu(a):
    return a * (1.0 / (1.0 + jnp.exp(-a)))


def _const_spec(shape):
    nd = len(shape)
    return pl.BlockSpec(shape, lambda *_: (0,) * nd)


def _ada_kernel(c_ref, w_ref, b_ref, o_ref):
    cs = _silu(c_ref[...])
    o_ref[...] = jnp.dot(cs, w_ref[...], preferred_element_type=F32,
                         precision=lax.Precision.HIGHEST) + b_ref[...]


def _ada_mod(c_all, w_ada, b_ada):
    rows = c_all.shape[0]
    n = w_ada.shape[-1]
    tn = 1024
    return pl.pallas_call(
        _ada_kernel,
        grid=(DEPTH, n // tn),
        in_specs=[
            pl.BlockSpec((rows, D_MODEL), lambda l, j: (0, 0)),
            pl.BlockSpec((None, D_MODEL, tn), lambda l, j: (l, 0, j)),
            pl.BlockSpec((None, 1, tn), lambda l, j: (l, 0, j)),
        ],
        out_specs=pl.BlockSpec((None, rows, tn), lambda l, j: (l, 0, j)),
        out_shape=jax.ShapeDtypeStruct((DEPTH, rows, n), F32),
        compiler_params=_params("parallel", "parallel"),
        name="ada_mod",
    )(c_all, w_ada, b_ada.reshape(DEPTH, 1, n))


def _premix_kernel(x_ref, mod_ref, gn_ref, cos_ref, sin_ref, win_ref, gq_ref, wuqa_ref, wuqb_ref,
                   gkv_ref, wukv_ref, q_ref, k_ref, v_ref, wq_ref, wk_ref, wv_ref):
    mod = mod_ref[...]
    h = _rms(x_ref[...], gn_ref[...]) * (1.0 + mod[1:2, :]) + mod[0:1, :]
    z = jnp.dot(h.astype(BF16), win_ref[...], preferred_element_type=F32)
    cqn = _rms(z[:, Z_CQ:Z_CKV], gq_ref[...]).astype(BF16)
    ckvn = _rms(z[:, Z_CKV:Z_KRA], gkv_ref[...]).astype(BF16)
    cos_t = cos_ref[...]
    sin_t = sin_ref[...]
    kr = (z[:, Z_KRA:Z_KRB] * cos_t + z[:, Z_KRB:Z_GQ] * sin_t).astype(BF16)
    wq_ref[...] = (z[:, Z_GQ:Z_GK] * GQA_SCALE).astype(BF16)
    wk_ref[...] = z[:, Z_GK:Z_GV].astype(BF16)
    wv_ref[...] = z[:, Z_GV:Z_END].astype(BF16)
    qa = jnp.dot(cqn, wuqa_ref[...], preferred_element_type=F32)
    qb = jnp.dot(cqn, wuqb_ref[...], preferred_element_type=F32)
    kv = jnp.dot(ckvn, wukv_ref[...], preferred_element_type=F32)
    lane = lax.broadcasted_iota(jnp.int32, kr.shape, 1)
    ones_col = jnp.where(lane == 0, 1.0, 0.0).astype(BF16)
    for hh in range(MLA_HEADS):
        a0 = hh * QK_PAD
        q_ref[hh, :, 0:LANES] = (qa[:, a0:a0 + LANES] * MLA_QSCALE).astype(BF16)
        q_rope = qa[:, a0 + LANES:a0 + QK_PAD] * cos_t + qb[:, hh * LANES:(hh + 1) * LANES] * sin_t
        q_ref[hh, :, LANES:QK_PAD] = (q_rope * MLA_QSCALE).astype(BF16)
        k_ref[hh, :, 0:LANES] = kv[:, a0:a0 + LANES].astype(BF16)
        k_ref[hh, :, LANES:QK_PAD] = kr
        v_ref[hh, :, 0:LANES] = kv[:, a0 + LANES:a0 + QK_PAD].astype(BF16)
        v_ref[hh, :, LANES:QK_PAD] = ones_col


def _premix(x, mod, gn, cos_t, sin_t, w_in_p, g_q, w_uq_a, w_uq_b, g_kv, w_ukv, *, seq, tm):
    t = x.shape[0]
    nt = seq // tm
    row = lambda i: (i, 0)
    hrow = lambda i: (0, i, 0)
    return pl.pallas_call(
        _premix_kernel,
        grid=(t // tm,),
        in_specs=[
            pl.BlockSpec((tm, D_MODEL), row),
            pl.BlockSpec((None, 6, D_MODEL), lambda i: (i // nt, 0, 0)),
            _const_spec((1, D_MODEL)),
            pl.BlockSpec((tm, LANES), lambda i: (i % nt, 0)),
            pl.BlockSpec((tm, LANES), lambda i: (i % nt, 0)),
            _const_spec(w_in_p.shape),
            _const_spec((1, Q_LORA)),
            _const_spec(w_uq_a.shape),
            _const_spec(w_uq_b.shape),
            _const_spec((1, KV_LORA)),
            _const_spec(w_ukv.shape),
        ],
        out_specs=[
            pl.BlockSpec((MLA_HEADS, tm, QK_PAD), hrow),
            pl.BlockSpec((MLA_HEADS, tm, QK_PAD), hrow),
            pl.BlockSpec((MLA_HEADS, tm, QK_PAD), hrow),
            pl.BlockSpec((tm, GQA_HEADS * GQA_DH), row),
            pl.BlockSpec((tm, GQA_KV_HEADS * GQA_DH), row),
            pl.BlockSpec((tm, GQA_KV_HEADS * GQA_DH), row),
        ],
        out_shape=[
            jax.ShapeDtypeStruct((MLA_HEADS, t, QK_PAD), BF16),
            jax.ShapeDtypeStruct((MLA_HEADS, t, QK_PAD), BF16),
            jax.ShapeDtypeStruct((MLA_HEADS, t, QK_PAD), BF16),
            jax.ShapeDtypeStruct((t, GQA_HEADS * GQA_DH), BF16),
            jax.ShapeDtypeStruct((t, GQA_KV_HEADS * GQA_DH), BF16),
            jax.ShapeDtypeStruct((t, GQA_KV_HEADS * GQA_DH), BF16),
        ],
        compiler_params=_params("parallel"),
        name="premix",
    )(x, mod, gn, cos_t, sin_t, w_in_p, g_q, w_uq_a, w_uq_b, g_kv, w_ukv)


def _mla_kernel(q_ref, k_ref, v_ref, o_ref, s_scr, p_scr, m_scr, a_scr, acc_scr, *, tk, nk, nsub):
    tq = q_ref.shape[0] // nsub

    def scores(j, buf):
        k = k_ref[pl.ds(pl.multiple_of(j * tk, tk), tk), :]
        for i in range(nsub):
            rows = slice(i * tq, (i + 1) * tq)
            s_scr[buf, rows, :] = lax.dot_general(q_ref[rows, :], k, (((1,), (1,)), ((), ())),
                                                  preferred_element_type=F32)

    def softmax(buf):
        for i in range(nsub):
            rows = slice(i * tq, (i + 1) * tq)
            s = s_scr[buf, rows, :]
            m_old = m_scr[rows, :]
            m_new = jnp.maximum(m_old, jnp.max(s, axis=1, keepdims=True))
            m_scr[rows, :] = m_new
            a_scr[buf, rows, :] = jnp.exp2(m_old - m_new)
            p_scr[buf, rows, :] = jnp.exp2(s - m_new[:, 0:1]).astype(BF16)

    def values(j, buf):
        v = v_ref[pl.ds(pl.multiple_of(j * tk, tk), tk), :]
        for i in range(nsub):
            rows = slice(i * tq, (i + 1) * tq)
            pv = jnp.dot(p_scr[buf, rows, :], v, preferred_element_type=F32)
            alpha = a_scr[buf, rows, :]
            acc_scr[rows, :] = jnp.concatenate([alpha, alpha], axis=1) * acc_scr[rows, :] + pv

    m_scr[...] = jnp.full(m_scr.shape, -jnp.inf, F32)
    acc_scr[...] = jnp.zeros(acc_scr.shape, F32)
    scores(0, 0)
    scores(1, 1)
    softmax(0)

    def body(u, carry):
        t = 2 * u + 1
        scores(t + 1, 0)
        softmax(1)
        values(t - 1, 0)
        scores(t + 2, 1)
        softmax(0)
        values(t, 1)
        return carry

    lax.fori_loop(0, (nk - 2) // 2, body, 0)
    softmax(1)
    values(nk - 2, 0)
    values(nk - 1, 1)
    acc = acc_scr[...]
    o_ref[...] = (acc[:, :MLA_DV] * (1.0 / acc[:, MLA_DV:MLA_DV + 1])).astype(BF16)


def _mla(q, k, v, *, batch, seq, tq, tk, nsub):
    t = batch * seq
    nq = seq // tq
    nk = seq // tk
    assert nk >= 2 and nk % 2 == 0, (seq, tk)
    return pl.pallas_call(
        functools.partial(_mla_kernel, tk=tk, nk=nk, nsub=nsub),
        grid=(batch, MLA_HEADS, nq),
        in_specs=[
            pl.BlockSpec((None, tq, QK_PAD), lambda b, h, i: (h, b * nq + i, 0)),
            pl.BlockSpec((None, seq, QK_PAD), lambda b, h, i: (h, b, 0)),
            pl.BlockSpec((None, seq, QK_PAD), lambda b, h, i: (h, b, 0)),
        ],
        out_specs=pl.BlockSpec((tq, MLA_DV), lambda b, h, i: (b * nq + i, h)),
        out_shape=jax.ShapeDtypeStruct((t, MLA_HEADS * MLA_DV), BF16),
        scratch_shapes=[
            pltpu.VMEM((2, tq, tk), F32),
            pltpu.VMEM((2, tq, tk), BF16),
            pltpu.VMEM((tq, LANES), F32),
            pltpu.VMEM((2, tq, LANES), F32),
            pltpu.VMEM((tq, QK_PAD), F32),
        ],
        compiler_params=_params("parallel", "parallel", "parallel"),
        name="mla_attn",
    )(q, k, v)


def _window_kernel(q_ref, kp_ref, kc_ref, kn_ref, vp_ref, vc_ref, vn_ref, bias_ref, sink_ref, o_ref,
                   kcat, vcat, *, rows, nt):
    i = pl.program_id(0)
    nsub = rows // WINDOW
    kcat[0:WINDOW, :] = kp_ref[...]
    kcat[WINDOW:WINDOW + rows, :] = kc_ref[...]
    kcat[WINDOW + rows:, :] = kn_ref[...]
    vcat[0:WINDOW, :] = vp_ref[...]
    vcat[WINDOW:WINDOW + rows, :] = vc_ref[...]
    vcat[WINDOW + rows:, :] = vn_ref[...]
    first_tile = (i % nt) == 0
    last_tile = (i % nt) == nt - 1
    col = lax.broadcasted_iota(jnp.int32, (1, 3 * WINDOW), 1)

    def sub(n, carry):
        r0 = pl.multiple_of(n * WINDOW, WINDOW)
        lo = jnp.where(jnp.logical_and(first_tile, n == 0), WINDOW, 0)
        hi = jnp.where(jnp.logical_and(last_tile, n == nsub - 1), 2 * WINDOW, 3 * WINDOW)
        edge = jnp.where(jnp.logical_or(col < lo, col >= hi), NEG_BIG, 0.0).astype(F32)
        for kh in range(GQA_KV_HEADS):
            qall = q_ref[pl.ds(r0, WINDOW), kh * GQA_GROUP * GQA_DH:(kh + 1) * GQA_GROUP * GQA_DH]
            qs = jnp.concatenate([qall[:, g * GQA_DH:(g + 1) * GQA_DH] for g in range(GQA_GROUP)], axis=0)
            kk = kcat[pl.ds(r0, 3 * WINDOW), kh * GQA_DH:(kh + 1) * GQA_DH]
            vv = vcat[pl.ds(r0, 3 * WINDOW), kh * GQA_DH:(kh + 1) * GQA_DH]
            s = lax.dot_general(qs, kk, (((1,), (1,)), ((), ())), preferred_element_type=F32)
            s = s + bias_ref[kh] + edge
            sk = sink_ref[kh]
            m = jnp.maximum(jnp.max(s, axis=1, keepdims=True), sk)
            e = jnp.exp(s - m)
            denom = jnp.sum(e, axis=1, keepdims=True) + jnp.exp(sk - m)
            o = jnp.dot(e.astype(BF16), vv, preferred_element_type=F32) * (1.0 / denom)
            for g in range(GQA_GROUP):
                hcol = (kh * GQA_GROUP + g) * GQA_DH
                o_ref[pl.ds(r0, WINDOW), hcol:hcol + GQA_DH] = o[g * WINDOW:(g + 1) * WINDOW].astype(BF16)
        return carry

    lax.fori_loop(0, nsub, sub, 0)


def _window(gq, gk, gv, bias, sink_col, *, seq, rows):
    t = gq.shape[0]
    nt = seq // rows
    rb = rows // WINDOW
    nblk = t // WINDOW
    kvw = GQA_KV_HEADS * GQA_DH
    prev = lambda i: (jnp.maximum(i * rb - 1, 0), 0)
    nxt = lambda i: (jnp.minimum((i + 1) * rb, nblk - 1), 0)
    cur = lambda i: (i, 0)
    return pl.pallas_call(
        functools.partial(_window_kernel, rows=rows, nt=nt),
        grid=(t // rows,),
        in_specs=[
            pl.BlockSpec((rows, GQA_HEADS * GQA_DH), cur),
            pl.BlockSpec((WINDOW, kvw), prev),
            pl.BlockSpec((rows, kvw), cur),
            pl.BlockSpec((WINDOW, kvw), nxt),
            pl.BlockSpec((WINDOW, kvw), prev),
            pl.BlockSpec((rows, kvw), cur),
            pl.BlockSpec((WINDOW, kvw), nxt),
            _const_spec(bias.shape),
            _const_spec(sink_col.shape),
        ],
        out_specs=pl.BlockSpec((rows, GQA_HEADS * GQA_DH), cur),
        out_shape=jax.ShapeDtypeStruct((t, GQA_HEADS * GQA_DH), BF16),
        scratch_shapes=[pltpu.VMEM((rows + 2 * WINDOW, kvw), BF16),
                        pltpu.VMEM((rows + 2 * WINDOW, kvw), BF16)],
        compiler_params=_params("parallel"),
        name="window_attn",
    )(gq, gk, gk, gk, gv, gv, gv, bias, sink_col)


def _postmix_kernel(oa_ref, ob_ref, x_ref, mod_ref, ga_ref, gb_ref, wa_ref, wb_ref, gf_ref, *rest, moe):
    if moe:
        wr_ref, x1_ref, h2_ref, gates_ref = rest
    else:
        x1_ref, h2_ref = rest
    mod = mod_ref[...]
    na = _rms(oa_ref[...].astype(F32), ga_ref[...]).astype(BF16)
    nb = _rms(ob_ref[...].astype(F32), gb_ref[...]).astype(BF16)
    mix = (jnp.dot(na, wa_ref[...], preferred_element_type=F32)
           + jnp.dot(nb, wb_ref[...], preferred_element_type=F32))
    x1 = x_ref[...] + mod[2:3, :] * mix
    x1_ref[...] = x1
    h2 = _rms(x1, gf_ref[...]) * (1.0 + mod[4:5, :]) + mod[3:4, :]
    h2_ref[...] = h2.astype(BF16)
    if moe:
        logits = jnp.dot(h2, wr_ref[...], preferred_element_type=F32, precision=lax.Precision.HIGHEST)
        lane = lax.broadcasted_iota(jnp.int32, logits.shape, 1)
        lg = jnp.where(lane < N_EXPERTS, logits, -jnp.inf)
        m1 = jnp.max(lg, axis=1, keepdims=True)
        i1 = jnp.min(jnp.where(lg == m1, lane, LANES), axis=1, keepdims=True)
        lg2 = jnp.where(lane == i1, -jnp.inf, lg)
        m2 = jnp.max(lg2, axis=1, keepdims=True)
        i2 = jnp.min(jnp.where(lg2 == m2, lane, LANES), axis=1, keepdims=True)
        e2 = jnp.exp(m2 - m1)
        w1 = 1.0 / (1.0 + e2)
        w2 = e2 * w1
        gates_ref[...] = jnp.where(lane == i1, w1, 0.0) + jnp.where(lane == i2, w2, 0.0)


def _postmix(oa, ob, x, mod, ga, gb, wa, wb, gf, wr, *, seq, tm):
    t = x.shape[0]
    nt = seq // tm
    moe = wr is not None
    row = lambda i: (i, 0)
    half = oa.shape[1]
    in_specs = [
        pl.BlockSpec((tm, half), row),
        pl.BlockSpec((tm, half), row),
        pl.BlockSpec((tm, D_MODEL), row),
        pl.BlockSpec((None, 6, D_MODEL), lambda i: (i // nt, 0, 0)),
        _const_spec((1, half)),
        _const_spec((1, half)),
        _const_spec(wa.shape),
        _const_spec(wb.shape),
        _const_spec((1, D_MODEL)),
    ]
    out_specs = [pl.BlockSpec((tm, D_MODEL), row), pl.BlockSpec((tm, D_MODEL), row)]
    out_shape = [jax.ShapeDtypeStruct((t, D_MODEL), F32), jax.ShapeDtypeStruct((t, D_MODEL), BF16)]
    args = [oa, ob, x, mod, ga, gb, wa, wb, gf]
    if moe:
        in_specs.append(_const_spec(wr.shape))
        out_specs.append(pl.BlockSpec((tm, LANES), row))
        out_shape.append(jax.ShapeDtypeStruct((t, LANES), F32))
        args.append(wr)
    return pl.pallas_call(
        functools.partial(_postmix_kernel, moe=moe),
        grid=(t // tm,),
        in_specs=in_specs,
        out_specs=out_specs,
        out_shape=out_shape,
        compiler_params=_params("parallel"),
        name="postmix_moe" if moe else "postmix",
    )(*args)


def _ffn_kernel(x1_ref, h_ref, mod_ref, wg_ref, wu_ref, wd_ref, gfin_ref, *rest, moe, final):
    if moe:
        gates_ref, o_ref, acc_ref = rest
    else:
        o_ref, acc_ref = rest
    e = pl.program_id(1)
    f = pl.program_id(2)

    @pl.when(jnp.logical_and(e == 0, f == 0))
    def _():
        acc_ref[...] = jnp.zeros_like(acc_ref)

    h = h_ref[...]
    a = jnp.dot(h, wg_ref[...], preferred_element_type=F32)
    u = jnp.dot(h, wu_ref[...], preferred_element_type=F32)
    mid = _silu(a) * u
    if moe:
        gates = gates_ref[...]
        lane = lax.broadcasted_iota(jnp.int32, gates.shape, 1)
        mid = mid * jnp.sum(jnp.where(lane == e, gates, 0.0), axis=1, keepdims=True)
    acc_ref[...] += jnp.dot(mid.astype(BF16), wd_ref[...], preferred_element_type=F32)

    @pl.when(jnp.logical_and(e == pl.num_programs(1) - 1, f == pl.num_programs(2) - 1))
    def _():
        y = x1_ref[...] + mod_ref[...][5:6, :] * acc_ref[...]
        if final:
            y = _rms(y, gfin_ref[...])
        o_ref[...] = y


def _ffn(x1, h2, mod, wg, wu, wd, gfin, gates, *, seq, tm, tf, final):
    t = x1.shape[0]
    nt = seq // tm
    ne, _, dff = wg.shape
    moe = gates is not None
    row = lambda i, e, f: (i, 0)
    in_specs = [
        pl.BlockSpec((tm, D_MODEL), row),
        pl.BlockSpec((tm, D_MODEL), row),
        pl.BlockSpec((None, 6, D_MODEL), lambda i, e, f: (i // nt, 0, 0)),
        pl.BlockSpec((None, D_MODEL, tf), lambda i, e, f: (e, 0, f)),
        pl.BlockSpec((None, D_MODEL, tf), lambda i, e, f: (e, 0, f)),
        pl.BlockSpec((None, tf, D_MODEL), lambda i, e, f: (e, f, 0)),
        pl.BlockSpec((1, D_MODEL), lambda i, e, f: (0, 0)),
    ]
    args = [x1, h2, mod, wg, wu, wd, gfin]
    if moe:
        in_specs.append(pl.BlockSpec((tm, LANES), row))
        args.append(gates)
    return pl.pallas_call(
        functools.partial(_ffn_kernel, moe=moe, final=final),
        grid=(t // tm, ne, dff // tf),
        in_specs=in_specs,
        out_specs=pl.BlockSpec((tm, D_MODEL), row),
        out_shape=jax.ShapeDtypeStruct((t, D_MODEL), F32),
        scratch_shapes=[pltpu.VMEM((tm, D_MODEL), F32)],
        compiler_params=_params("parallel", "arbitrary", "arbitrary"),
        name="ffn_moe" if moe else "ffn_dense",
    )(*args)


def _rope_tables(seq):
    pos = jnp.arange(seq, dtype=F32)
    inv = 1.0 / (ROPE_THETA ** (jnp.arange(0, MLA_DR, 2, dtype=F32) / MLA_DR))
    ang = pos[:, None] * inv[None, :]
    cos, sin = jnp.cos(ang), jnp.sin(ang)
    zero = jnp.zeros((seq, LANES - MLA_DR), F32)
    return (jnp.concatenate([cos, cos, zero], axis=1), jnp.concatenate([-sin, sin, zero], axis=1))


def _t5_bucket(rel):
    nb = NUM_BUCKETS // 2
    ret = (rel > 0).astype(jnp.int32) * nb
    n = jnp.abs(rel)
    max_exact = nb // 2
    nf = jnp.maximum(n, 1).astype(F32)
    large = max_exact + (jnp.log(nf / max_exact) / math.log(MAX_DISTANCE / max_exact)
                         * (nb - max_exact)).astype(jnp.int32)
    large = jnp.minimum(large, nb - 1)
    return ret + jnp.where(n < max_exact, n, large)


def _window_bias(rel_bias):
    qpos = jnp.arange(WINDOW, dtype=jnp.int32)
    jpos = jnp.arange(3 * WINDOW, dtype=jnp.int32)
    rel = jpos[None, :] - WINDOW - qpos[:, None]
    bias = rel_bias[_t5_bucket(rel)].astype(F32).transpose(2, 0, 1)
    bias = jnp.where((jnp.abs(rel) <= WINDOW)[None], bias, NEG_BIG)
    return bias.reshape(GQA_KV_HEADS, GQA_GROUP * WINDOW, 3 * WINDOW)


def _prep_layer(w_in, w_uq, w_ukv, w_out):
    half = MLA_DR // 2
    zpad = jnp.zeros((D_MODEL, LANES - MLA_DR), F32)
    kr0 = Q_LORA + KV_LORA
    k1 = w_in[:, kr0:kr0 + half]
    k2 = w_in[:, kr0 + half:kr0 + MLA_DR]
    w_in_p = jnp.concatenate(
        [w_in[:, :kr0], k1, k2, zpad, k2, k1, zpad, w_in[:, kr0 + MLA_DR:]], axis=1).astype(BF16)
    wq = w_uq.reshape(Q_LORA, MLA_HEADS, MLA_DN + MLA_DR)
    r1 = wq[:, :, MLA_DN:MLA_DN + half]
    r2 = wq[:, :, MLA_DN + half:]
    zq = jnp.zeros((Q_LORA, MLA_HEADS, LANES - MLA_DR), F32)
    w_uq_a = jnp.concatenate([wq[:, :, :MLA_DN], r1, r2, zq], axis=2).reshape(Q_LORA, MLA_HEADS * QK_PAD)
    w_uq_b = jnp.concatenate([r2, r1, zq], axis=2).reshape(Q_LORA, MLA_HEADS * LANES)
    out_a = MLA_HEADS * MLA_DV
    return (w_in_p, w_uq_a.astype(BF16), w_uq_b.astype(BF16), w_ukv.astype(BF16),
            w_out[:out_a].astype(BF16), w_out[out_a:].astype(BF16))


def _trunk(x, mods, layers, bias, g_final, *, batch, seq):
    tm = min(512, seq)
    cos_t, sin_t = _rope_tables(seq)
    for l, p in enumerate(layers):
        q, k, v, gq, gk, gv = _premix(x, mods[l], p["g_norm_mix"], cos_t, sin_t, p["w_in_p"], p["g_q_lat"],
                                      p["w_uq_a"], p["w_uq_b"], p["g_kv_lat"], p["w_ukv"], seq=seq, tm=tm)
        tq_sub = min(MLA_TQ_SUB, seq)
        tq = min(MLA_TQ, seq)
        oa = _mla(q, k, v, batch=batch, seq=seq, tq=tq, tk=min(MLA_TK, seq), nsub=tq // tq_sub)
        ob = _window(gq, gk, gv, bias, p["sink_col"], seq=seq, rows=min(512, seq))
        res = _postmix(oa, ob, x, mods[l], p["g_out_a"], p["g_out_b"], p["w_out_a"], p["w_out_b"],
                       p["g_norm_ffn"], p.get("w_router"), seq=seq, tm=tm)
        gates = res[2] if len(res) == 3 else None
        x = _ffn(res[0], res[1], mods[l], p["w_gate"], p["w_up"], p["w_down"], g_final, gates,
                 seq=seq, tm=tm, tf=512, final=(l == len(layers) - 1))
    return x


def kernel(x_prompt, x_sample, c_prompt, c_sample, rel_bias, w_ada, b_ada, g_norm_mix, g_norm_ffn, w_in, g_q_lat, w_uq, g_kv_lat, w_ukv, sink, g_out_a, g_out_b, w_out, w_gate_d, w_up_d, w_down_d, w_router, w_gate_e, w_up_e, w_down_e, g_final):
    bp, sp, _ = x_prompt.shape
    bs, ss, _ = x_sample.shape
    rows = -(-(bp + bs) // 8) * 8
    c_all = jnp.concatenate([c_prompt, c_sample, jnp.zeros((rows - bp - bs, D_MODEL), F32)], axis=0)
    mod = _ada_mod(c_all, w_ada, b_ada)
    mod_p = [mod[l, :bp].reshape(bp, 6, D_MODEL) for l in range(DEPTH)]
    mod_s = [mod[l, bp:bp + bs].reshape(bs, 6, D_MODEL) for l in range(DEPTH)]
    bias = _window_bias(rel_bias)
    layers = []
    for l in range(DEPTH):
        w_in_p, w_uq_a, w_uq_b, w_ukv_b, w_out_a, w_out_b = _prep_layer(w_in[l], w_uq[l], w_ukv[l], w_out[l])
        out_a = MLA_HEADS * MLA_DV
        p = dict(
            g_norm_mix=g_norm_mix[l][None], g_norm_ffn=g_norm_ffn[l][None],
            w_in_p=w_in_p, g_q_lat=g_q_lat[l][None], w_uq_a=w_uq_a, w_uq_b=w_uq_b,
            g_kv_lat=g_kv_lat[l][None], w_ukv=w_ukv_b,
            sink_col=jnp.repeat(sink[l].astype(F32), WINDOW).reshape(GQA_KV_HEADS, GQA_GROUP * WINDOW, 1),
            g_out_a=g_out_a[l][None], g_out_b=g_out_b[l][None], w_out_a=w_out_a, w_out_b=w_out_b,
        )
        i = l // 2
        if l % 2 == 0:
            p.update(w_gate=w_gate_d[i][None].astype(BF16), w_up=w_up_d[i][None].astype(BF16),
                     w_down=w_down_d[i][None].astype(BF16))
        else:
            wr = jnp.concatenate([w_router[i], jnp.zeros((D_MODEL, LANES - N_EXPERTS), F32)], axis=1)
            p.update(w_gate=w_gate_e[i].astype(BF16), w_up=w_up_e[i].astype(BF16),
                     w_down=w_down_e[i].astype(BF16), w_router=wr)
        layers.append(p)
    gfin = g_final[None]
    y_p = _trunk(x_prompt.reshape(bp * sp, D_MODEL), mod_p, layers, bias, gfin, batch=bp, seq=sp)
    y_s = _trunk(x_sample.reshape(bs * ss, D_MODEL), mod_s, layers, bias, gfin, batch=bs, seq=ss)
    return (y_p.reshape(bp, sp, D_MODEL), y_s.reshape(bs, ss, D_MODEL))
```

```python
import functools
import math

import jax
import jax.numpy as jnp
from jax import lax
from jax.experimental import pallas as pl
from jax.experimental.pallas import tpu as pltpu

F32 = jnp.float32
BF16 = jnp.bfloat16

D_MODEL = 2048
DEPTH = 2
EPS = 1e-6
MLA_HEADS = 8
MLA_DN = 128
MLA_DR = 64
MLA_DV = 128
Q_LORA = 512
KV_LORA = 512
ROPE_THETA = 10000.0
MLA_SCALE = 1.0 / math.sqrt(MLA_DN + MLA_DR)
MLA_QSCALE = MLA_SCALE * math.log2(math.e)
GQA_HEADS = 8
GQA_KV_HEADS = 2
GQA_GROUP = GQA_HEADS // GQA_KV_HEADS
GQA_DH = 128
WINDOW = 128
GQA_SCALE = 1.0 / math.sqrt(GQA_DH)
NUM_BUCKETS = 32
MAX_DISTANCE = 128
N_EXPERTS = 8
NEG_BIG = -1e30

LANES = 128
V7X_VMEM_LIMIT = 56 * 1024 * 1024

QK_PAD = 2 * LANES
MLA_TQ = 1024
MLA_TQ_SUB = 256
MLA_TK = 512
ROW_TILE = 512
FF_TILE = 512
Z_CQ = 0
Z_CKV = Z_CQ + Q_LORA
Z_KRA = Z_CKV + KV_LORA
Z_KRB = Z_KRA + LANES
Z_GQ = Z_KRB + LANES
Z_GK = Z_GQ + GQA_HEADS * GQA_DH
Z_GV = Z_GK + GQA_KV_HEADS * GQA_DH
Z_END = Z_GV + GQA_KV_HEADS * GQA_DH


def _params(*sem):
    return pltpu.CompilerParams(dimension_semantics=sem, vmem_limit_bytes=V7X_VMEM_LIMIT)


def _rms(x, g):
    return x * lax.rsqrt(jnp.mean(x * x, axis=-1, keepdims=True) + EPS) * g


def _silu(a):
    return a * (1.0 / (1.0 + jnp.exp(-a)))


def _const_spec(shape):
    nd = len(shape)
    return pl.BlockSpec(shape, lambda *_: (0,) * nd)


def _ada_kernel(c_ref, w_ref, b_ref, o_ref):
    cs = _silu(c_ref[...])
    o_ref[...] = jnp.dot(cs, w_ref[...], preferred_element_type=F32,
                         precision=lax.Precision.HIGHEST) + b_ref[...]


def _ada_mod(c_all, w_ada, b_ada):
    rows = c_all.shape[0]
    n = w_ada.shape[-1]
    tn = 1024
    return pl.pallas_call(
        _ada_kernel,
        grid=(DEPTH, n // tn),
        in_specs=[
            pl.BlockSpec((rows, D_MODEL), lambda l, j: (0, 0)),
            pl.BlockSpec((None, D_MODEL, tn), lambda l, j: (l, 0, j)),
            pl.BlockSpec((None, 1, tn), lambda l, j: (l, 0, j)),
        ],
        out_specs=pl.BlockSpec((None, rows, tn), lambda l, j: (l, 0, j)),
        out_shape=jax.ShapeDtypeStruct((DEPTH, rows, n), F32),
        compiler_params=_params("parallel", "parallel"),
        name="ada_mod",
    )(c_all, w_ada, b_ada.reshape(DEPTH, 1, n))


def _premix_kernel(x_ref, mod_ref, gn_ref, cos_ref, sin_ref, win_ref, gq_ref, wuqa_ref, wuqb_ref,
                   gkv_ref, wukv_ref, q_ref, k_ref, v_ref, wq_ref, wk_ref, wv_ref):
    mod = mod_ref[...]
    h = _rms(x_ref[...], gn_ref[...]) * (1.0 + mod[1:2, :]) + mod[0:1, :]
    z = jnp.dot(h.astype(BF16), win_ref[...], preferred_element_type=F32)
    cqn = _rms(z[:, Z_CQ:Z_CKV], gq_ref[...]).astype(BF16)
    ckvn = _rms(z[:, Z_CKV:Z_KRA], gkv_ref[...]).astype(BF16)
    cos_t = cos_ref[...]
    sin_t = sin_ref[...]
    kr = (z[:, Z_KRA:Z_KRB] * cos_t + z[:, Z_KRB:Z_GQ] * sin_t).astype(BF16)
    wq_ref[...] = (z[:, Z_GQ:Z_GK] * GQA_SCALE).astype(BF16)
    wk_ref[...] = z[:, Z_GK:Z_GV].astype(BF16)
    wv_ref[...] = z[:, Z_GV:Z_END].astype(BF16)
    qa = jnp.dot(cqn, wuqa_ref[...], preferred_element_type=F32)
    qb = jnp.dot(cqn, wuqb_ref[...], preferred_element_type=F32)
    kv = jnp.dot(ckvn, wukv_ref[...], preferred_element_type=F32)
    lane = lax.broadcasted_iota(jnp.int32, kr.shape, 1)
    ones_col = jnp.where(lane == 0, 1.0, 0.0).astype(BF16)
    for hh in range(MLA_HEADS):
        a0 = hh * QK_PAD
        q_ref[hh, :, 0:LANES] = (qa[:, a0:a0 + LANES] * MLA_QSCALE).astype(BF16)
        q_rope = qa[:, a0 + LANES:a0 + QK_PAD] * cos_t + qb[:, hh * LANES:(hh + 1) * LANES] * sin_t
        q_ref[hh, :, LANES:QK_PAD] = (q_rope * MLA_QSCALE).astype(BF16)
        k_ref[hh, :, 0:LANES] = kv[:, a0:a0 + LANES].astype(BF16)
        k_ref[hh, :, LANES:QK_PAD] = kr
        v_ref[hh, :, 0:LANES] = kv[:, a0 + LANES:a0 + QK_PAD].astype(BF16)
        v_ref[hh, :, LANES:QK_PAD] = ones_col


def _premix(x, mod, gn, cos_t, sin_t, w_in_p, g_q, w_uq_a, w_uq_b, g_kv, w_ukv, *, seq, tm):
    t = x.shape[0]
    nt = seq // tm
    row = lambda i: (i, 0)
    hrow = lambda i: (0, i, 0)
    return pl.pallas_call(
        _premix_kernel,
        grid=(t // tm,),
        in_specs=[
            pl.BlockSpec((tm, D_MODEL), row),
            pl.BlockSpec((None, 6, D_MODEL), lambda i: (i // nt, 0, 0)),
            _const_spec((1, D_MODEL)),
            pl.BlockSpec((tm, LANES), lambda i: (i % nt, 0)),
            pl.BlockSpec((tm, LANES), lambda i: (i % nt, 0)),
            _const_spec(w_in_p.shape),
            _const_spec((1, Q_LORA)),
            _const_spec(w_uq_a.shape),
            _const_spec(w_uq_b.shape),
            _const_spec((1, KV_LORA)),
            _const_spec(w_ukv.shape),
        ],
        out_specs=[
            pl.BlockSpec((MLA_HEADS, tm, QK_PAD), hrow),
            pl.BlockSpec((MLA_HEADS, tm, QK_PAD), hrow),
            pl.BlockSpec((MLA_HEADS, tm, QK_PAD), hrow),
            pl.BlockSpec((tm, GQA_HEADS * GQA_DH), row),
            pl.BlockSpec((tm, GQA_KV_HEADS * GQA_DH), row),
            pl.BlockSpec((tm, GQA_KV_HEADS * GQA_DH), row),
        ],
        out_shape=[
            jax.ShapeDtypeStruct((MLA_HEADS, t, QK_PAD), BF16),
            jax.ShapeDtypeStruct((MLA_HEADS, t, QK_PAD), BF16),
            jax.ShapeDtypeStruct((MLA_HEADS, t, QK_PAD), BF16),
            jax.ShapeDtypeStruct((t, GQA_HEADS * GQA_DH), BF16),
            jax.ShapeDtypeStruct((t, GQA_KV_HEADS * GQA_DH), BF16),
            jax.ShapeDtypeStruct((t, GQA_KV_HEADS * GQA_DH), BF16),
        ],
        compiler_params=_params("parallel"),
        name="premix",
    )(x, mod, gn, cos_t, sin_t, w_in_p, g_q, w_uq_a, w_uq_b, g_kv, w_ukv)


def _mla_kernel(q_ref, k_ref, v_ref, o_ref, s_scr, p_scr, m_scr, a_scr, acc_scr, *, tk, nk, nsub):
    tq = q_ref.shape[0] // nsub
    subs = range(nsub)

    def scores(j, buf, i):
        k = k_ref[pl.ds(pl.multiple_of(j * tk, tk), tk), :]
        rows = slice(i * tq, (i + 1) * tq)
        s_scr[buf, rows, :] = lax.dot_general(q_ref[rows, :], k, (((1,), (1,)), ((), ())),
                                              preferred_element_type=F32)

    def softmax(buf, i):
        rows = slice(i * tq, (i + 1) * tq)
        s = s_scr[buf, rows, :]
        m_old = m_scr[rows, :]
        m_new = jnp.maximum(m_old, jnp.max(s, axis=1, keepdims=True))
        m_scr[rows, :] = m_new
        a_scr[buf, rows, :] = jnp.exp2(m_old - m_new)
        p_scr[buf, rows, :] = jnp.exp2(s - m_new[:, 0:1]).astype(BF16)

    def values(j, buf, i):
        v = v_ref[pl.ds(pl.multiple_of(j * tk, tk), tk), :]
        rows = slice(i * tq, (i + 1) * tq)
        pv = jnp.dot(p_scr[buf, rows, :], v, preferred_element_type=F32)
        alpha = a_scr[buf, rows, :]
        acc_scr[rows, :] = jnp.concatenate([alpha, alpha], axis=1) * acc_scr[rows, :] + pv

    m_scr[...] = jnp.full(m_scr.shape, -jnp.inf, F32)
    acc_scr[...] = jnp.zeros(acc_scr.shape, F32)
    for i in subs:
        scores(0, 0, i)
        scores(1, 1, i)
        softmax(0, i)

    def body(u, carry):
        t = 2 * u + 1
        for i in subs:
            values(t - 1, 0, i)
            scores(t + 1, 0, i)
            softmax(1, i)
        for i in subs:
            values(t, 1, i)
            scores(t + 2, 1, i)
            softmax(0, i)
        return carry

    lax.fori_loop(0, (nk - 2) // 2, body, 0)
    for i in subs:
        softmax(1, i)
        values(nk - 2, 0, i)
        values(nk - 1, 1, i)
    acc = acc_scr[...]
    o_ref[...] = (acc[:, :MLA_DV] * (1.0 / acc[:, MLA_DV:MLA_DV + 1])).astype(BF16)


def _mla(q, k, v, *, batch, seq, tq, tk, nsub):
    t = batch * seq
    nq = seq // tq
    nk = seq // tk
    assert nk >= 2 and nk % 2 == 0, (seq, tk)
    return pl.pallas_call(
        functools.partial(_mla_kernel, tk=tk, nk=nk, nsub=nsub),
        grid=(batch, MLA_HEADS, nq),
        in_specs=[
            pl.BlockSpec((None, tq, QK_PAD), lambda b, h, i: (h, b * nq + i, 0)),
            pl.BlockSpec((None, seq, QK_PAD), lambda b, h, i: (h, b, 0)),
            pl.BlockSpec((None, seq, QK_PAD), lambda b, h, i: (h, b, 0)),
        ],
        out_specs=pl.BlockSpec((tq, MLA_DV), lambda b, h, i: (b * nq + i, h)),
        out_shape=jax.ShapeDtypeStruct((t, MLA_HEADS * MLA_DV), BF16),
        scratch_shapes=[
            pltpu.VMEM((2, tq, tk), F32),
            pltpu.VMEM((2, tq, tk), BF16),
            pltpu.VMEM((tq, LANES), F32),
            pltpu.VMEM((2, tq, LANES), F32),
            pltpu.VMEM((tq, QK_PAD), F32),
        ],
        compiler_params=_params("parallel", "parallel", "parallel"),
        name="mla_attn",
    )(q, k, v)


def _window_kernel(q_ref, kp_ref, kc_ref, kn_ref, vp_ref, vc_ref, vn_ref, bias_ref, sink_ref, o_ref,
                   kcat, vcat, *, rows, nt):
    i = pl.program_id(0)
    nsub = rows // WINDOW
    kcat[0:WINDOW, :] = kp_ref[...]
    kcat[WINDOW:WINDOW + rows, :] = kc_ref[...]
    kcat[WINDOW + rows:, :] = kn_ref[...]
    vcat[0:WINDOW, :] = vp_ref[...]
    vcat[WINDOW:WINDOW + rows, :] = vc_ref[...]
    vcat[WINDOW + rows:, :] = vn_ref[...]
    first_tile = (i % nt) == 0
    last_tile = (i % nt) == nt - 1
    col = lax.broadcasted_iota(jnp.int32, (1, 3 * WINDOW), 1)

    def sub(n, carry):
        r0 = pl.multiple_of(n * WINDOW, WINDOW)
        lo = jnp.where(jnp.logical_and(first_tile, n == 0), WINDOW, 0)
        hi = jnp.where(jnp.logical_and(last_tile, n == nsub - 1), 2 * WINDOW, 3 * WINDOW)
        edge = jnp.where(jnp.logical_or(col < lo, col >= hi), NEG_BIG, 0.0).astype(F32)
        for kh in range(GQA_KV_HEADS):
            qall = q_ref[pl.ds(r0, WINDOW), kh * GQA_GROUP * GQA_DH:(kh + 1) * GQA_GROUP * GQA_DH]
            qs = jnp.concatenate([qall[:, g * GQA_DH:(g + 1) * GQA_DH] for g in range(GQA_GROUP)], axis=0)
            kk = kcat[pl.ds(r0, 3 * WINDOW), kh * GQA_DH:(kh + 1) * GQA_DH]
            vv = vcat[pl.ds(r0, 3 * WINDOW), kh * GQA_DH:(kh + 1) * GQA_DH]
            s = lax.dot_general(qs, kk, (((1,), (1,)), ((), ())), preferred_element_type=F32)
            s = s + bias_ref[kh] + edge
            sk = sink_ref[kh]
            m = jnp.maximum(jnp.max(s, axis=1, keepdims=True), sk)
            e = jnp.exp(s - m)
            denom = jnp.sum(e, axis=1, keepdims=True) + jnp.exp(sk - m)
            o = jnp.dot(e.astype(BF16), vv, preferred_element_type=F32) * (1.0 / denom)
            for g in range(GQA_GROUP):
                hcol = (kh * GQA_GROUP + g) * GQA_DH
                o_ref[pl.ds(r0, WINDOW), hcol:hcol + GQA_DH] = o[g * WINDOW:(g + 1) * WINDOW].astype(BF16)
        return carry

    lax.fori_loop(0, nsub, sub, 0)


def _window(gq, gk, gv, bias, sink_col, *, seq, rows):
    t = gq.shape[0]
    nt = seq // rows
    rb = rows // WINDOW
    nblk = t // WINDOW
    kvw = GQA_KV_HEADS * GQA_DH
    prev = lambda i: (jnp.maximum(i * rb - 1, 0), 0)
    nxt = lambda i: (jnp.minimum((i + 1) * rb, nblk - 1), 0)
    cur = lambda i: (i, 0)
    return pl.pallas_call(
        functools.partial(_window_kernel, rows=rows, nt=nt),
        grid=(t // rows,),
        in_specs=[
            pl.BlockSpec((rows, GQA_HEADS * GQA_DH), cur),
            pl.BlockSpec((WINDOW, kvw), prev),
            pl.BlockSpec((rows, kvw), cur),
            pl.BlockSpec((WINDOW, kvw), nxt),
            pl.BlockSpec((WINDOW, kvw), prev),
            pl.BlockSpec((rows, kvw), cur),
            pl.BlockSpec((WINDOW, kvw), nxt),
            _const_spec(bias.shape),
            _const_spec(sink_col.shape),
        ],
        out_specs=pl.BlockSpec((rows, GQA_HEADS * GQA_DH), cur),
        out_shape=jax.ShapeDtypeStruct((t, GQA_HEADS * GQA_DH), BF16),
        scratch_shapes=[pltpu.VMEM((rows + 2 * WINDOW, kvw), BF16),
                        pltpu.VMEM((rows + 2 * WINDOW, kvw), BF16)],
        compiler_params=_params("parallel"),
        name="window_attn",
    )(gq, gk, gk, gk, gv, gv, gv, bias, sink_col)


def _postmix_kernel(oa_ref, ob_ref, x_ref, mod_ref, ga_ref, gb_ref, wa_ref, wb_ref, gf_ref, *rest, moe):
    if moe:
        wr_ref, x1_ref, h2_ref, topw_ref, topi_ref = rest
    else:
        x1_ref, h2_ref = rest
    mod = mod_ref[...]
    na = _rms(oa_ref[...].astype(F32), ga_ref[...]).astype(BF16)
    nb = _rms(ob_ref[...].astype(F32), gb_ref[...]).astype(BF16)
    mix = (jnp.dot(na, wa_ref[...], preferred_element_type=F32)
           + jnp.dot(nb, wb_ref[...], preferred_element_type=F32))
    x1 = x_ref[...] + mod[2:3, :] * mix
    x1_ref[...] = x1
    h2 = _rms(x1, gf_ref[...]) * (1.0 + mod[4:5, :]) + mod[3:4, :]
    h2_ref[...] = h2.astype(h2_ref.dtype)
    if moe:
        h_hi = h2.astype(BF16)
        h_lo = (h2 - h_hi.astype(F32)).astype(BF16)
        logits = (jnp.dot(h_hi, wr_ref[0], preferred_element_type=F32)
                  + jnp.dot(h_lo, wr_ref[0], preferred_element_type=F32)
                  + jnp.dot(h_hi, wr_ref[1], preferred_element_type=F32))
        lane = lax.broadcasted_iota(jnp.int32, logits.shape, 1)
        lg = jnp.where(lane < N_EXPERTS, logits, -jnp.inf)
        m1 = jnp.max(lg, axis=1, keepdims=True)
        i1 = jnp.min(jnp.where(lg == m1, lane, LANES), axis=1, keepdims=True)
        lg2 = jnp.where(lane == i1, -jnp.inf, lg)
        m2 = jnp.max(lg2, axis=1, keepdims=True)
        i2 = jnp.min(jnp.where(lg2 == m2, lane, LANES), axis=1, keepdims=True)
        e2 = jnp.exp(m2 - m1)
        w1 = 1.0 / (1.0 + e2)
        w2 = e2 * w1
        topw_ref[...] = jnp.where(lane == 0, w1, jnp.where(lane == 1, w2, 0.0))
        topi_ref[...] = jnp.where(lane == 0, i1, jnp.where(lane == 1, i2, 0))


def _postmix(oa, ob, x, mod, ga, gb, wa, wb, gf, wr, *, seq, tm):
    t = x.shape[0]
    nt = seq // tm
    moe = wr is not None
    row = lambda i: (i, 0)
    half = oa.shape[1]
    in_specs = [
        pl.BlockSpec((tm, half), row),
        pl.BlockSpec((tm, half), row),
        pl.BlockSpec((tm, D_MODEL), row),
        pl.BlockSpec((None, 6, D_MODEL), lambda i: (i // nt, 0, 0)),
        _const_spec((1, half)),
        _const_spec((1, half)),
        _const_spec(wa.shape),
        _const_spec(wb.shape),
        _const_spec((1, D_MODEL)),
    ]
    out_specs = [pl.BlockSpec((tm, D_MODEL), row), pl.BlockSpec((tm, D_MODEL), row)]
    out_shape = [jax.ShapeDtypeStruct((t, D_MODEL), F32), jax.ShapeDtypeStruct((t, D_MODEL), F32 if moe else BF16)]
    args = [oa, ob, x, mod, ga, gb, wa, wb, gf]
    if moe:
        in_specs.append(_const_spec(wr.shape))
        out_specs += [pl.BlockSpec((tm, LANES), row), pl.BlockSpec((tm, LANES), row)]
        out_shape += [jax.ShapeDtypeStruct((t, LANES), F32), jax.ShapeDtypeStruct((t, LANES), jnp.int32)]
        args.append(wr)
    return pl.pallas_call(
        functools.partial(_postmix_kernel, moe=moe),
        grid=(t // tm,),
        in_specs=in_specs,
        out_specs=out_specs,
        out_shape=out_shape,
        compiler_params=_params("parallel"),
        name="postmix_moe" if moe else "postmix",
    )(*args)


def _ffn_kernel(x1_ref, h_ref, mod_ref, wg_ref, wu_ref, wd_ref, o_ref, acc_ref):
    f = pl.program_id(1)

    @pl.when(f == 0)
    def _():
        acc_ref[...] = jnp.zeros_like(acc_ref)

    h = h_ref[...]
    a = jnp.dot(h, wg_ref[...], preferred_element_type=F32)
    u = jnp.dot(h, wu_ref[...], preferred_element_type=F32)
    acc_ref[...] += jnp.dot((_silu(a) * u).astype(BF16), wd_ref[...], preferred_element_type=F32)

    @pl.when(f == pl.num_programs(1) - 1)
    def _():
        o_ref[...] = x1_ref[...] + mod_ref[...][5:6, :] * acc_ref[...]


def _ffn(x1, h2, mod, wg, wu, wd, *, seq, tm, tf):
    t = x1.shape[0]
    nt = seq // tm
    dff = wg.shape[1]
    row = lambda i, f: (i, 0)
    return pl.pallas_call(
        _ffn_kernel,
        grid=(t // tm, dff // tf),
        in_specs=[
            pl.BlockSpec((tm, D_MODEL), row),
            pl.BlockSpec((tm, D_MODEL), row),
            pl.BlockSpec((None, 6, D_MODEL), lambda i, f: (i // nt, 0, 0)),
            pl.BlockSpec((D_MODEL, tf), lambda i, f: (0, f)),
            pl.BlockSpec((D_MODEL, tf), lambda i, f: (0, f)),
            pl.BlockSpec((tf, D_MODEL), lambda i, f: (f, 0)),
        ],
        out_specs=pl.BlockSpec((tm, D_MODEL), row),
        out_shape=jax.ShapeDtypeStruct((t, D_MODEL), F32),
        scratch_shapes=[pltpu.VMEM((tm, D_MODEL), F32)],
        compiler_params=_params("parallel", "arbitrary"),
        name="ffn_dense",
    )(x1, h2, mod, wg, wu, wd)


def _route(topi, tm):
    t = topi.shape[0]
    na = 2 * t
    nt = na // tm + N_EXPERTS
    flat_e = topi.reshape(na)
    onehot = (flat_e[:, None] == jnp.arange(N_EXPERTS, dtype=jnp.int32)[None, :]).astype(jnp.int32)
    csum = jnp.cumsum(onehot, axis=0)
    rank = jnp.sum((csum - onehot) * onehot, axis=1)
    padded = ((csum[-1] + tm - 1) // tm) * tm
    ends = jnp.cumsum(padded)
    pos = (ends - padded)[flat_e] + rank
    a = jnp.arange(na, dtype=jnp.int32)
    r = jnp.arange(nt * tm, dtype=jnp.int32)
    src = jnp.zeros((nt * tm,), jnp.int32).at[pos].set(a // 2)
    dst = (na + r % tm).at[pos].set((a % 2) * t + a // 2)
    tile_start = jnp.arange(nt, dtype=jnp.int32) * tm
    tile_valid = (tile_start < ends[-1]).astype(jnp.int32)
    tile_expert = jnp.minimum(jnp.sum((tile_start[:, None] >= ends[None, :]).astype(jnp.int32), axis=1),
                              N_EXPERTS - 1)
    return src.reshape(nt, 1, tm), dst.reshape(nt, 1, tm), tile_expert, tile_valid


def _moe_kernel(te_ref, tv_ref, src_ref, nsrc_ref, dst_ref, h_hbm, wg_ref, wu_ref, wd_ref, y_hbm,
                hbuf, hb16, ybuf, acc, gsem, ssem, *, tm):
    i = pl.program_id(0)
    f = pl.program_id(1)
    nt = pl.num_programs(0)
    nf = pl.num_programs(1)
    slot = i % 2
    valid = tv_ref[i] == 1
    nxt = jnp.minimum(i + 1, nt - 1)
    next_valid = jnp.logical_and(i + 1 < nt, tv_ref[nxt] == 1)

    def gather(idx_ref, s):
        def row(r, c):
            pltpu.make_async_copy(h_hbm.at[pl.ds(idx_ref[0, r], 1), :], hbuf.at[s, pl.ds(r, 1), :],
                                  gsem.at[s]).start()
            return c
        lax.fori_loop(0, tm, row, 0, unroll=8)

    def wait_scatter():
        pltpu.make_async_copy(ybuf, ybuf, ssem).wait()

    @pl.when(jnp.logical_and(valid, f == 0))
    def _():
        @pl.when(i == 0)
        def _():
            gather(src_ref, 0)

        @pl.when(next_valid)
        def _():
            gather(nsrc_ref, 1 - slot)

        pltpu.make_async_copy(hbuf.at[slot], hbuf.at[slot], gsem.at[slot]).wait()
        hb16[...] = hbuf[slot].astype(BF16)
        acc[...] = jnp.zeros_like(acc)

    @pl.when(valid)
    def _():
        h = hb16[...]
        a = jnp.dot(h, wg_ref[...], preferred_element_type=F32)
        u = jnp.dot(h, wu_ref[...], preferred_element_type=F32)
        acc[...] += jnp.dot((_silu(a) * u).astype(BF16), wd_ref[...], preferred_element_type=F32)

    @pl.when(jnp.logical_and(valid, f == nf - 1))
    def _():
        @pl.when(i > 0)
        def _():
            wait_scatter()

        ybuf[...] = acc[...]

        def row(r, c):
            pltpu.make_async_copy(ybuf.at[pl.ds(r, 1), :], y_hbm.at[pl.ds(dst_ref[0, r], 1), :], ssem).start()
            return c
        lax.fori_loop(0, tm, row, 0, unroll=8)

        @pl.when(jnp.logical_not(next_valid))
        def _():
            wait_scatter()
            ybuf[...] = jnp.zeros_like(ybuf)
            spare = pltpu.make_async_copy(ybuf, y_hbm.at[pl.ds(y_hbm.shape[0] - tm, tm), :], ssem)
            spare.start()
            spare.wait()


def _moe_ffn(h2, topi, wg, wu, wd, *, tm, tf):
    t = h2.shape[0]
    dff = wg.shape[2]
    nf = dff // tf
    src, dst, tile_expert, tile_valid = _route(topi, tm)
    nt = src.shape[0]
    idx_spec = lambda fn: pl.BlockSpec((None, 1, tm), fn, memory_space=pltpu.SMEM)
    fsel = lambda i, f, tv: jnp.where(tv[i] == 1, f, nf - 1)
    return pl.pallas_call(
        functools.partial(_moe_kernel, tm=tm),
        grid_spec=pltpu.PrefetchScalarGridSpec(
            num_scalar_prefetch=2,
            grid=(nt, nf),
            in_specs=[
                idx_spec(lambda i, f, te, tv: (i, 0, 0)),
                idx_spec(lambda i, f, te, tv: (jnp.minimum(i + 1, nt - 1), 0, 0)),
                idx_spec(lambda i, f, te, tv: (i, 0, 0)),
                pl.BlockSpec(memory_space=pl.ANY),
                pl.BlockSpec((None, D_MODEL, tf), lambda i, f, te, tv: (te[i], 0, fsel(i, f, tv))),
                pl.BlockSpec((None, D_MODEL, tf), lambda i, f, te, tv: (te[i], 0, fsel(i, f, tv))),
                pl.BlockSpec((None, tf, D_MODEL), lambda i, f, te, tv: (te[i], fsel(i, f, tv), 0)),
            ],
            out_specs=pl.BlockSpec(memory_space=pl.ANY),
            scratch_shapes=[
                pltpu.VMEM((2, tm, D_MODEL), F32),
                pltpu.VMEM((tm, D_MODEL), BF16),
                pltpu.VMEM((tm, D_MODEL), F32),
                pltpu.VMEM((tm, D_MODEL), F32),
                pltpu.SemaphoreType.DMA((2,)),
                pltpu.SemaphoreType.DMA(()),
            ],
        ),
        out_shape=jax.ShapeDtypeStruct((2 * t + tm, D_MODEL), F32),
        compiler_params=_params("arbitrary", "arbitrary"),
        name="ffn_moe",
    )(tile_expert, tile_valid, src, src, dst, h2, wg, wu, wd)


def _combine_kernel(x1_ref, ya_ref, yb_ref, topw_ref, mod_ref, gfin_ref, o_ref):
    w = topw_ref[...]
    f = w[:, 0:1] * ya_ref[...] + w[:, 1:2] * yb_ref[...]
    o_ref[...] = _rms(x1_ref[...] + mod_ref[...][5:6, :] * f, gfin_ref[...])


def _combine(x1, y2, topw, mod, gfin, *, seq, tm):
    t = x1.shape[0]
    nt = seq // tm
    nb = t // tm
    row = lambda i: (i, 0)
    return pl.pallas_call(
        _combine_kernel,
        grid=(nb,),
        in_specs=[
            pl.BlockSpec((tm, D_MODEL), row),
            pl.BlockSpec((tm, D_MODEL), row),
            pl.BlockSpec((tm, D_MODEL), lambda i: (nb + i, 0)),
            pl.BlockSpec((tm, LANES), row),
            pl.BlockSpec((None, 6, D_MODEL), lambda i: (i // nt, 0, 0)),
            _const_spec((1, D_MODEL)),
        ],
        out_specs=pl.BlockSpec((tm, D_MODEL), row),
        out_shape=jax.ShapeDtypeStruct((t, D_MODEL), F32),
        compiler_params=_params("parallel"),
        name="moe_combine",
    )(x1, y2, y2, topw, mod, gfin)


def _rope_tables(seq):
    pos = jnp.arange(seq, dtype=F32)
    inv = 1.0 / (ROPE_THETA ** (jnp.arange(0, MLA_DR, 2, dtype=F32) / MLA_DR))
    ang = pos[:, None] * inv[None, :]
    cos, sin = jnp.cos(ang), jnp.sin(ang)
    zero = jnp.zeros((seq, LANES - MLA_DR), F32)
    return (jnp.concatenate([cos, cos, zero], axis=1), jnp.concatenate([-sin, sin, zero], axis=1))


def _t5_bucket(rel):
    nb = NUM_BUCKETS // 2
    ret = (rel > 0).astype(jnp.int32) * nb
    n = jnp.abs(rel)
    max_exact = nb // 2
    nf = jnp.maximum(n, 1).astype(F32)
    large = max_exact + (jnp.log(nf / max_exact) / math.log(MAX_DISTANCE / max_exact)
                         * (nb - max_exact)).astype(jnp.int32)
    large = jnp.minimum(large, nb - 1)
    return ret + jnp.where(n < max_exact, n, large)


def _window_bias(rel_bias):
    qpos = jnp.arange(WINDOW, dtype=jnp.int32)
    jpos = jnp.arange(3 * WINDOW, dtype=jnp.int32)
    rel = jpos[None, :] - WINDOW - qpos[:, None]
    bias = rel_bias[_t5_bucket(rel)].astype(F32).transpose(2, 0, 1)
    bias = jnp.where((jnp.abs(rel) <= WINDOW)[None], bias, NEG_BIG)
    return bias.reshape(GQA_KV_HEADS, GQA_GROUP * WINDOW, 3 * WINDOW)


def _prep_layer(w_in, w_uq, w_ukv, w_out):
    half = MLA_DR // 2
    zpad = jnp.zeros((D_MODEL, LANES - MLA_DR), F32)
    kr0 = Q_LORA + KV_LORA
    k1 = w_in[:, kr0:kr0 + half]
    k2 = w_in[:, kr0 + half:kr0 + MLA_DR]
    w_in_p = jnp.concatenate(
        [w_in[:, :kr0], k1, k2, zpad, k2, k1, zpad, w_in[:, kr0 + MLA_DR:]], axis=1).astype(BF16)
    wq = w_uq.reshape(Q_LORA, MLA_HEADS, MLA_DN + MLA_DR)
    r1 = wq[:, :, MLA_DN:MLA_DN + half]
    r2 = wq[:, :, MLA_DN + half:]
    zq = jnp.zeros((Q_LORA, MLA_HEADS, LANES - MLA_DR), F32)
    w_uq_a = jnp.concatenate([wq[:, :, :MLA_DN], r1, r2, zq], axis=2).reshape(Q_LORA, MLA_HEADS * QK_PAD)
    w_uq_b = jnp.concatenate([r2, r1, zq], axis=2).reshape(Q_LORA, MLA_HEADS * LANES)
    out_a = MLA_HEADS * MLA_DV
    return (w_in_p, w_uq_a.astype(BF16), w_uq_b.astype(BF16), w_ukv.astype(BF16),
            w_out[:out_a].astype(BF16), w_out[out_a:].astype(BF16))


def _trunk(x, mods, layers, bias, g_final, *, batch, seq):
    tm = min(ROW_TILE, seq)
    cos_t, sin_t = _rope_tables(seq)
    for l, p in enumerate(layers):
        q, k, v, gq, gk, gv = _premix(x, mods[l], p["g_norm_mix"], cos_t, sin_t, p["w_in_p"], p["g_q_lat"],
                                      p["w_uq_a"], p["w_uq_b"], p["g_kv_lat"], p["w_ukv"], seq=seq, tm=tm)
        tq_sub = min(MLA_TQ_SUB, seq)
        tq = min(MLA_TQ, seq)
        oa = _mla(q, k, v, batch=batch, seq=seq, tq=tq, tk=min(MLA_TK, seq), nsub=tq // tq_sub)
        ob = _window(gq, gk, gv, bias, p["sink_col"], seq=seq, rows=min(ROW_TILE, seq))
        res = _postmix(oa, ob, x, mods[l], p["g_out_a"], p["g_out_b"], p["w_out_a"], p["w_out_b"],
                       p["g_norm_ffn"], p.get("w_router"), seq=seq, tm=tm)
        if "w_router" in p:
            assert l == len(layers) - 1
            x1, h2, topw, topi = res
            y2 = _moe_ffn(h2, topi[:, :2], p["w_gate"], p["w_up"], p["w_down"], tm=ROW_TILE, tf=FF_TILE)
            x = _combine(x1, y2, topw, mods[l], g_final, seq=seq, tm=tm)
        else:
            x1, h2 = res
            x = _ffn(x1, h2, mods[l], p["w_gate"], p["w_up"], p["w_down"], seq=seq, tm=tm, tf=FF_TILE)
    return x


def kernel(x_prompt, x_sample, c_prompt, c_sample, rel_bias, w_ada, b_ada, g_norm_mix, g_norm_ffn, w_in, g_q_lat, w_uq, g_kv_lat, w_ukv, sink, g_out_a, g_out_b, w_out, w_gate_d, w_up_d, w_down_d, w_router, w_gate_e, w_up_e, w_down_e, g_final):
    bp, sp, _ = x_prompt.shape
    bs, ss, _ = x_sample.shape
    rows = -(-(bp + bs) // 8) * 8
    c_all = jnp.concatenate([c_prompt, c_sample, jnp.zeros((rows - bp - bs, D_MODEL), F32)], axis=0)
    mod = _ada_mod(c_all, w_ada, b_ada)
    mod_p = [mod[l, :bp].reshape(bp, 6, D_MODEL) for l in range(DEPTH)]
    mod_s = [mod[l, bp:bp + bs].reshape(bs, 6, D_MODEL) for l in range(DEPTH)]
    bias = _window_bias(rel_bias)
    layers = []
    for l in range(DEPTH):
        w_in_p, w_uq_a, w_uq_b, w_ukv_b, w_out_a, w_out_b = _prep_layer(w_in[l], w_uq[l], w_ukv[l], w_out[l])
        out_a = MLA_HEADS * MLA_DV
        p = dict(
            g_norm_mix=g_norm_mix[l][None], g_norm_ffn=g_norm_ffn[l][None],
            w_in_p=w_in_p, g_q_lat=g_q_lat[l][None], w_uq_a=w_uq_a, w_uq_b=w_uq_b,
            g_kv_lat=g_kv_lat[l][None], w_ukv=w_ukv_b,
            sink_col=jnp.repeat(sink[l].astype(F32), WINDOW).reshape(GQA_KV_HEADS, GQA_GROUP * WINDOW, 1),
            g_out_a=g_out_a[l][None], g_out_b=g_out_b[l][None], w_out_a=w_out_a, w_out_b=w_out_b,
        )
        i = l // 2
        if l % 2 == 0:
            p.update(w_gate=w_gate_d[i].astype(BF16), w_up=w_up_d[i].astype(BF16), w_down=w_down_d[i].astype(BF16))
        else:
            wr = jnp.concatenate([w_router[i], jnp.zeros((D_MODEL, LANES - N_EXPERTS), F32)], axis=1)
            wr_hi = wr.astype(BF16)
            wr = jnp.stack([wr_hi, (wr - wr_hi.astype(F32)).astype(BF16)])
            p.update(w_gate=w_gate_e[i].astype(BF16), w_up=w_up_e[i].astype(BF16),
                     w_down=w_down_e[i].astype(BF16), w_router=wr)
        layers.append(p)
    gfin = g_final[None]
    y_s = _trunk(x_sample.reshape(bs * ss, D_MODEL), mod_s, layers, bias, gfin, batch=bs, seq=ss)
    y_p = _trunk(x_prompt.reshape(bp * sp, D_MODEL), mod_p, layers, bias, gfin, batch=bp, seq=sp)
    return (y_p.reshape(bp, sp, D_MODEL), y_s.reshape(bs, ss, D_MODEL))
```

```python
import functools
import math

import jax
import jax.numpy as jnp
from jax import lax
from jax.experimental import pallas as pl
from jax.experimental.pallas import tpu as pltpu

F32 = jnp.float32
BF16 = jnp.bfloat16

D_MODEL = 2048
DEPTH = 2
EPS = 1e-6
MLA_HEADS = 8
MLA_DN = 128
MLA_DR = 64
MLA_DV = 128
Q_LORA = 512
KV_LORA = 512
ROPE_THETA = 10000.0
MLA_SCALE = 1.0 / math.sqrt(MLA_DN + MLA_DR)
MLA_QSCALE = MLA_SCALE * math.log2(math.e)
GQA_HEADS = 8
GQA_KV_HEADS = 2
GQA_GROUP = GQA_HEADS // GQA_KV_HEADS
GQA_DH = 128
WINDOW = 128
GQA_SCALE = 1.0 / math.sqrt(GQA_DH)
NUM_BUCKETS = 32
MAX_DISTANCE = 128
N_EXPERTS = 8
NEG_BIG = -1e30

LANES = 128
V7X_VMEM_LIMIT = 56 * 1024 * 1024

QK_PAD = 2 * LANES
VT_ROWS = MLA_DV + 16
MLA_TQ = 2048
MLA_TQ_SUB = 256
MLA_TK = 512
ROW_TILE = 512
FF_TILE = 512
Z_CQ = 0
Z_CKV = Z_CQ + Q_LORA
Z_KRA = Z_CKV + KV_LORA
Z_KRB = Z_KRA + LANES
Z_GQ = Z_KRB + LANES
Z_GK = Z_GQ + GQA_HEADS * GQA_DH
Z_GV = Z_GK + GQA_KV_HEADS * GQA_DH
Z_END = Z_GV + GQA_KV_HEADS * GQA_DH


def _params(*sem):
    return pltpu.CompilerParams(dimension_semantics=sem, vmem_limit_bytes=V7X_VMEM_LIMIT)


def _rms(x, g):
    return x * lax.rsqrt(jnp.mean(x * x, axis=-1, keepdims=True) + EPS) * g


def _silu(a):
    return a * (1.0 / (1.0 + jnp.exp(-a)))


def _const_spec(shape):
    nd = len(shape)
    return pl.BlockSpec(shape, lambda *_: (0,) * nd)


def _ada_kernel(c_ref, w_ref, b_ref, o_ref):
    cs = _silu(c_ref[...])
    o_ref[...] = jnp.dot(cs, w_ref[...], preferred_element_type=F32,
                         precision=lax.Precision.HIGHEST) + b_ref[...]


def _ada_mod(c_all, w_ada, b_ada):
    rows = c_all.shape[0]
    n = w_ada.shape[-1]
    tn = 1024
    return pl.pallas_call(
        _ada_kernel,
        grid=(DEPTH, n // tn),
        in_specs=[
            pl.BlockSpec((rows, D_MODEL), lambda l, j: (0, 0)),
            pl.BlockSpec((None, D_MODEL, tn), lambda l, j: (l, 0, j)),
            pl.BlockSpec((None, 1, tn), lambda l, j: (l, 0, j)),
        ],
        out_specs=pl.BlockSpec((None, rows, tn), lambda l, j: (l, 0, j)),
        out_shape=jax.ShapeDtypeStruct((DEPTH, rows, n), F32),
        compiler_params=_params("parallel", "parallel"),
        name="ada_mod",
    )(c_all, w_ada, b_ada.reshape(DEPTH, 1, n))


def _premix_kernel(x_ref, mod_ref, gn_ref, cos_ref, sin_ref, win_ref, gq_ref, wuqa_ref, wuqb_ref,
                   gkv_ref, wukv_ref, q_ref, k_ref, v_ref, wq_ref, wk_ref, wv_ref):
    mod = mod_ref[...]
    h = _rms(x_ref[...], gn_ref[...]) * (1.0 + mod[1:2, :]) + mod[0:1, :]
    z = jnp.dot(h.astype(BF16), win_ref[...], preferred_element_type=F32)
    cqn = _rms(z[:, Z_CQ:Z_CKV], gq_ref[...]).astype(BF16)
    ckvn = _rms(z[:, Z_CKV:Z_KRA], gkv_ref[...]).astype(BF16)
    cos_t = cos_ref[...]
    sin_t = sin_ref[...]
    kr = (z[:, Z_KRA:Z_KRB] * cos_t + z[:, Z_KRB:Z_GQ] * sin_t).astype(BF16)
    wq_ref[...] = (z[:, Z_GQ:Z_GK] * GQA_SCALE).astype(BF16)
    wk_ref[...] = z[:, Z_GK:Z_GV].astype(BF16)
    wv_ref[...] = z[:, Z_GV:Z_END].astype(BF16)
    qa = jnp.dot(cqn, wuqa_ref[...], preferred_element_type=F32)
    qb = jnp.dot(cqn, wuqb_ref[...], preferred_element_type=F32)
    kv = jnp.dot(ckvn, wukv_ref[...], preferred_element_type=F32)
    sub = lax.broadcasted_iota(jnp.int32, (VT_ROWS - MLA_DV, kr.shape[0]), 0)
    ones_row = jnp.where(sub == 0, 1.0, 0.0).astype(BF16)
    for hh in range(MLA_HEADS):
        a0 = hh * QK_PAD
        q_ref[hh, :, 0:LANES] = (qa[:, a0:a0 + LANES] * MLA_QSCALE).astype(BF16)
        q_rope = qa[:, a0 + LANES:a0 + QK_PAD] * cos_t + qb[:, hh * LANES:(hh + 1) * LANES] * sin_t
        q_ref[hh, :, LANES:QK_PAD] = (q_rope * MLA_QSCALE).astype(BF16)
        k_ref[hh, :, 0:LANES] = kv[:, a0:a0 + LANES].astype(BF16)
        k_ref[hh, :, LANES:QK_PAD] = kr
        v_ref[hh, 0:MLA_DV, :] = kv[:, a0 + LANES:a0 + QK_PAD].T.astype(BF16)
        v_ref[hh, MLA_DV:VT_ROWS, :] = ones_row


def _premix(x, mod, gn, cos_t, sin_t, w_in_p, g_q, w_uq_a, w_uq_b, g_kv, w_ukv, *, seq, tm):
    t = x.shape[0]
    nt = seq // tm
    row = lambda i: (i, 0)
    hrow = lambda i: (0, i, 0)
    return pl.pallas_call(
        _premix_kernel,
        grid=(t // tm,),
        in_specs=[
            pl.BlockSpec((tm, D_MODEL), row),
            pl.BlockSpec((None, 6, D_MODEL), lambda i: (i // nt, 0, 0)),
            _const_spec((1, D_MODEL)),
            pl.BlockSpec((tm, LANES), lambda i: (i % nt, 0)),
            pl.BlockSpec((tm, LANES), lambda i: (i % nt, 0)),
            _const_spec(w_in_p.shape),
            _const_spec((1, Q_LORA)),
            _const_spec(w_uq_a.shape),
            _const_spec(w_uq_b.shape),
            _const_spec((1, KV_LORA)),
            _const_spec(w_ukv.shape),
        ],
        out_specs=[
            pl.BlockSpec((MLA_HEADS, tm, QK_PAD), hrow),
            pl.BlockSpec((MLA_HEADS, tm, QK_PAD), hrow),
            pl.BlockSpec((MLA_HEADS, VT_ROWS, tm), lambda i: (0, 0, i)),
            pl.BlockSpec((tm, GQA_HEADS * GQA_DH), row),
            pl.BlockSpec((tm, GQA_KV_HEADS * GQA_DH), row),
            pl.BlockSpec((tm, GQA_KV_HEADS * GQA_DH), row),
        ],
        out_shape=[
            jax.ShapeDtypeStruct((MLA_HEADS, t, QK_PAD), BF16),
            jax.ShapeDtypeStruct((MLA_HEADS, t, QK_PAD), BF16),
            jax.ShapeDtypeStruct((MLA_HEADS, VT_ROWS, t), BF16),
            jax.ShapeDtypeStruct((t, GQA_HEADS * GQA_DH), BF16),
            jax.ShapeDtypeStruct((t, GQA_KV_HEADS * GQA_DH), BF16),
            jax.ShapeDtypeStruct((t, GQA_KV_HEADS * GQA_DH), BF16),
        ],
        compiler_params=_params("parallel"),
        name="premix",
    )(x, mod, gn, cos_t, sin_t, w_in_p, g_q, w_uq_a, w_uq_b, g_kv, w_ukv)


def _mla_kernel(q_ref, k_ref, v_ref, o_ref, qt_scr, s_scr, p_scr, m_scr, a_scr, acc_scr, *, tk, nk, nsub):
    tq = q_ref.shape[0] // nsub
    subs = range(nsub)
    qt_scr[...] = q_ref[...].astype(F32).T.astype(BF16)

    def scores(j, buf, i):
        k = k_ref[pl.ds(pl.multiple_of(j * tk, tk), tk), :]
        cols = slice(i * tq, (i + 1) * tq)
        s_scr[buf, :, cols] = jnp.dot(k, qt_scr[:, cols], preferred_element_type=F32)

    def softmax(buf, i):
        cols = slice(i * tq, (i + 1) * tq)
        s = s_scr[buf, :, cols]
        m_old = m_scr[:, cols]
        m_new = jnp.maximum(m_old, jnp.max(s, axis=0, keepdims=True))
        m_scr[:, cols] = m_new
        a_scr[buf, :, cols] = jnp.exp2(m_old - m_new)
        p_scr[buf, :, cols] = jnp.exp2(s - m_new).astype(BF16)

    def values(j, buf, i):
        v = v_ref[:, pl.ds(pl.multiple_of(j * tk, tk), tk)]
        cols = slice(i * tq, (i + 1) * tq)
        pv = jnp.dot(v, p_scr[buf, :, cols], preferred_element_type=F32)
        acc_scr[:, cols] = a_scr[buf, :, cols] * acc_scr[:, cols] + pv

    m_scr[...] = jnp.full(m_scr.shape, -jnp.inf, F32)
    acc_scr[...] = jnp.zeros(acc_scr.shape, F32)
    for i in subs:
        scores(0, 0, i)
        scores(1, 1, i)
        softmax(0, i)

    def body(u, carry):
        t = 2 * u + 1
        for i in subs:
            values(t - 1, 0, i)
            scores(t + 1, 0, i)
            softmax(1, i)
        for i in subs:
            values(t, 1, i)
            scores(t + 2, 1, i)
            softmax(0, i)
        return carry

    lax.fori_loop(0, (nk - 2) // 2, body, 0)
    for i in subs:
        softmax(1, i)
        values(nk - 2, 0, i)
        values(nk - 1, 1, i)
    acc = acc_scr[...]
    o_ref[...] = (acc[0:MLA_DV, :] * (1.0 / acc[MLA_DV:MLA_DV + 1, :])).T.astype(BF16)


def _mla(q, k, v, *, batch, seq, tq, tk, nsub):
    t = batch * seq
    nq = seq // tq
    nk = seq // tk
    assert nk >= 2 and nk % 2 == 0, (seq, tk)
    return pl.pallas_call(
        functools.partial(_mla_kernel, tk=tk, nk=nk, nsub=nsub),
        grid=(batch, MLA_HEADS, nq),
        in_specs=[
            pl.BlockSpec((None, tq, QK_PAD), lambda b, h, i: (h, b * nq + i, 0)),
            pl.BlockSpec((None, seq, QK_PAD), lambda b, h, i: (h, b, 0)),
            pl.BlockSpec((None, VT_ROWS, seq), lambda b, h, i: (h, 0, b)),
        ],
        out_specs=pl.BlockSpec((tq, MLA_DV), lambda b, h, i: (b * nq + i, h)),
        out_shape=jax.ShapeDtypeStruct((t, MLA_HEADS * MLA_DV), BF16),
        scratch_shapes=[
            pltpu.VMEM((QK_PAD, tq), BF16),
            pltpu.VMEM((2, tk, tq), F32),
            pltpu.VMEM((2, tk, tq), BF16),
            pltpu.VMEM((1, tq), F32),
            pltpu.VMEM((2, 1, tq), F32),
            pltpu.VMEM((VT_ROWS, tq), F32),
        ],
        compiler_params=_params("parallel", "parallel", "parallel"),
        name="mla_attn",
    )(q, k, v)


def _window_kernel(q_ref, kp_ref, kc_ref, kn_ref, vp_ref, vc_ref, vn_ref, bias_ref, sink_ref, o_ref,
                   kcat, vcat, *, rows, nt):
    i = pl.program_id(0)
    nsub = rows // WINDOW
    kcat[0:WINDOW, :] = kp_ref[...]
    kcat[WINDOW:WINDOW + rows, :] = kc_ref[...]
    kcat[WINDOW + rows:, :] = kn_ref[...]
    vcat[0:WINDOW, :] = vp_ref[...]
    vcat[WINDOW:WINDOW + rows, :] = vc_ref[...]
    vcat[WINDOW + rows:, :] = vn_ref[...]
    first_tile = (i % nt) == 0
    last_tile = (i % nt) == nt - 1
    col = lax.broadcasted_iota(jnp.int32, (1, 3 * WINDOW), 1)

    chains = [(n, kh) for n in range(nsub) for kh in range(GQA_KV_HEADS)]
    s_all = []
    for n, kh in chains:
        r0 = n * WINDOW
        qall = q_ref[r0:r0 + WINDOW, kh * GQA_GROUP * GQA_DH:(kh + 1) * GQA_GROUP * GQA_DH]
        qs = jnp.concatenate([qall[:, g * GQA_DH:(g + 1) * GQA_DH] for g in range(GQA_GROUP)], axis=0)
        kk = kcat[r0:r0 + 3 * WINDOW, kh * GQA_DH:(kh + 1) * GQA_DH]
        s = lax.dot_general(qs, kk, (((1,), (1,)), ((), ())), preferred_element_type=F32) + bias_ref[kh]
        if n == 0:
            s = s + jnp.where(jnp.logical_and(first_tile, col < WINDOW), NEG_BIG, 0.0)
        if n == nsub - 1:
            s = s + jnp.where(jnp.logical_and(last_tile, col >= 2 * WINDOW), NEG_BIG, 0.0)
        s_all.append(s)
    m_all = [jnp.maximum(jnp.max(s, axis=1, keepdims=True), sink_ref[kh]) for s, (n, kh) in zip(s_all, chains)]
    e_all = [jnp.exp(s - m) for s, m in zip(s_all, m_all)]
    d_all = [jnp.sum(e, axis=1, keepdims=True) + jnp.exp(sink_ref[kh] - m)
             for e, m, (n, kh) in zip(e_all, m_all, chains)]
    for e, d, (n, kh) in zip(e_all, d_all, chains):
        r0 = n * WINDOW
        vv = vcat[r0:r0 + 3 * WINDOW, kh * GQA_DH:(kh + 1) * GQA_DH]
        o = jnp.dot(e.astype(BF16), vv, preferred_element_type=F32) * (1.0 / d)
        for g in range(GQA_GROUP):
            hcol = (kh * GQA_GROUP + g) * GQA_DH
            o_ref[r0:r0 + WINDOW, hcol:hcol + GQA_DH] = o[g * WINDOW:(g + 1) * WINDOW].astype(BF16)


def _window(gq, gk, gv, bias, sink_col, *, seq, rows):
    t = gq.shape[0]
    nt = seq // rows
    rb = rows // WINDOW
    nblk = t // WINDOW
    kvw = GQA_KV_HEADS * GQA_DH
    prev = lambda i: (jnp.maximum(i * rb - 1, 0), 0)
    nxt = lambda i: (jnp.minimum((i + 1) * rb, nblk - 1), 0)
    cur = lambda i: (i, 0)
    return pl.pallas_call(
        functools.partial(_window_kernel, rows=rows, nt=nt),
        grid=(t // rows,),
        in_specs=[
            pl.BlockSpec((rows, GQA_HEADS * GQA_DH), cur),
            pl.BlockSpec((WINDOW, kvw), prev),
            pl.BlockSpec((rows, kvw), cur),
            pl.BlockSpec((WINDOW, kvw), nxt),
            pl.BlockSpec((WINDOW, kvw), prev),
            pl.BlockSpec((rows, kvw), cur),
            pl.BlockSpec((WINDOW, kvw), nxt),
            _const_spec(bias.shape),
            _const_spec(sink_col.shape),
        ],
        out_specs=pl.BlockSpec((rows, GQA_HEADS * GQA_DH), cur),
        out_shape=jax.ShapeDtypeStruct((t, GQA_HEADS * GQA_DH), BF16),
        scratch_shapes=[pltpu.VMEM((rows + 2 * WINDOW, kvw), BF16),
                        pltpu.VMEM((rows + 2 * WINDOW, kvw), BF16)],
        compiler_params=_params("parallel"),
        name="window_attn",
    )(gq, gk, gk, gk, gv, gv, gv, bias, sink_col)


def _postmix_kernel(oa_ref, ob_ref, x_ref, mod_ref, ga_ref, gb_ref, wa_ref, wb_ref, gf_ref, *rest, moe):
    if moe:
        wr_ref, x1_ref, h2_ref, topw_ref, topi_ref = rest
    else:
        x1_ref, h2_ref = rest
    mod = mod_ref[...]
    na = _rms(oa_ref[...].astype(F32), ga_ref[...]).astype(BF16)
    nb = _rms(ob_ref[...].astype(F32), gb_ref[...]).astype(BF16)
    mix = (jnp.dot(na, wa_ref[...], preferred_element_type=F32)
           + jnp.dot(nb, wb_ref[...], preferred_element_type=F32))
    x1 = x_ref[...] + mod[2:3, :] * mix
    x1_ref[...] = x1
    h2 = _rms(x1, gf_ref[...]) * (1.0 + mod[4:5, :]) + mod[3:4, :]
    h2_ref[...] = h2.astype(h2_ref.dtype)
    if moe:
        h_hi = h2.astype(BF16)
        h_lo = (h2 - h_hi.astype(F32)).astype(BF16)
        logits = (jnp.dot(h_hi, wr_ref[0], preferred_element_type=F32)
                  + jnp.dot(h_lo, wr_ref[0], preferred_element_type=F32)
                  + jnp.dot(h_hi, wr_ref[1], preferred_element_type=F32))
        lane = lax.broadcasted_iota(jnp.int32, logits.shape, 1)
        lg = jnp.where(lane < N_EXPERTS, logits, -jnp.inf)
        m1 = jnp.max(lg, axis=1, keepdims=True)
        i1 = jnp.min(jnp.where(lg == m1, lane, LANES), axis=1, keepdims=True)
        lg2 = jnp.where(lane == i1, -jnp.inf, lg)
        m2 = jnp.max(lg2, axis=1, keepdims=True)
        i2 = jnp.min(jnp.where(lg2 == m2, lane, LANES), axis=1, keepdims=True)
        e2 = jnp.exp(m2 - m1)
        w1 = 1.0 / (1.0 + e2)
        w2 = e2 * w1
        topw_ref[...] = jnp.where(lane == 0, w1, jnp.where(lane == 1, w2, 0.0))
        topi_ref[...] = jnp.where(lane == 0, i1, jnp.where(lane == 1, i2, 0))


def _postmix(oa, ob, x, mod, ga, gb, wa, wb, gf, wr, *, seq, tm):
    t = x.shape[0]
    nt = seq // tm
    moe = wr is not None
    row = lambda i: (i, 0)
    half = oa.shape[1]
    in_specs = [
        pl.BlockSpec((tm, half), row),
        pl.BlockSpec((tm, half), row),
        pl.BlockSpec((tm, D_MODEL), row),
        pl.BlockSpec((None, 6, D_MODEL), lambda i: (i // nt, 0, 0)),
        _const_spec((1, half)),
        _const_spec((1, half)),
        _const_spec(wa.shape),
        _const_spec(wb.shape),
        _const_spec((1, D_MODEL)),
    ]
    out_specs = [pl.BlockSpec((tm, D_MODEL), row), pl.BlockSpec((tm, D_MODEL), row)]
    out_shape = [jax.ShapeDtypeStruct((t, D_MODEL), F32), jax.ShapeDtypeStruct((t, D_MODEL), F32 if moe else BF16)]
    args = [oa, ob, x, mod, ga, gb, wa, wb, gf]
    if moe:
        in_specs.append(_const_spec(wr.shape))
        out_specs += [pl.BlockSpec((tm, LANES), row), pl.BlockSpec((tm, LANES), row)]
        out_shape += [jax.ShapeDtypeStruct((t, LANES), F32), jax.ShapeDtypeStruct((t, LANES), jnp.int32)]
        args.append(wr)
    return pl.pallas_call(
        functools.partial(_postmix_kernel, moe=moe),
        grid=(t // tm,),
        in_specs=in_specs,
        out_specs=out_specs,
        out_shape=out_shape,
        compiler_params=_params("parallel"),
        name="postmix_moe" if moe else "postmix",
    )(*args)


def _ffn_kernel(x1_ref, h_ref, mod_ref, wg_ref, wu_ref, wd_ref, o_ref, acc_ref):
    f = pl.program_id(1)

    @pl.when(f == 0)
    def _():
        acc_ref[...] = jnp.zeros_like(acc_ref)

    h = h_ref[...]
    a = jnp.dot(h, wg_ref[...], preferred_element_type=F32)
    u = jnp.dot(h, wu_ref[...], preferred_element_type=F32)
    acc_ref[...] += jnp.dot((_silu(a) * u).astype(BF16), wd_ref[...], preferred_element_type=F32)

    @pl.when(f == pl.num_programs(1) - 1)
    def _():
        o_ref[...] = x1_ref[...] + mod_ref[...][5:6, :] * acc_ref[...]


def _ffn(x1, h2, mod, wg, wu, wd, *, seq, tm, tf):
    t = x1.shape[0]
    nt = seq // tm
    dff = wg.shape[1]
    row = lambda i, f: (i, 0)
    return pl.pallas_call(
        _ffn_kernel,
        grid=(t // tm, dff // tf),
        in_specs=[
            pl.BlockSpec((tm, D_MODEL), row),
            pl.BlockSpec((tm, D_MODEL), row),
            pl.BlockSpec((None, 6, D_MODEL), lambda i, f: (i // nt, 0, 0)),
            pl.BlockSpec((D_MODEL, tf), lambda i, f: (0, f)),
            pl.BlockSpec((D_MODEL, tf), lambda i, f: (0, f)),
            pl.BlockSpec((tf, D_MODEL), lambda i, f: (f, 0)),
        ],
        out_specs=pl.BlockSpec((tm, D_MODEL), row),
        out_shape=jax.ShapeDtypeStruct((t, D_MODEL), F32),
        scratch_shapes=[pltpu.VMEM((tm, D_MODEL), F32)],
        compiler_params=_params("parallel", "arbitrary"),
        name="ffn_dense",
    )(x1, h2, mod, wg, wu, wd)


def _route(topi, tm):
    t = topi.shape[0]
    na = 2 * t
    nt = na // tm + N_EXPERTS
    flat_e = topi.reshape(na)
    onehot = (flat_e[:, None] == jnp.arange(N_EXPERTS, dtype=jnp.int32)[None, :]).astype(jnp.int32)
    csum = jnp.cumsum(onehot, axis=0)
    rank = jnp.sum((csum - onehot) * onehot, axis=1)
    padded = ((csum[-1] + tm - 1) // tm) * tm
    ends = jnp.cumsum(padded)
    pos = (ends - padded)[flat_e] + rank
    a = jnp.arange(na, dtype=jnp.int32)
    r = jnp.arange(nt * tm, dtype=jnp.int32)
    src = jnp.zeros((nt * tm,), jnp.int32).at[pos].set(a // 2)
    dst = (na + r % tm).at[pos].set((a % 2) * t + a // 2)
    tile_start = jnp.arange(nt, dtype=jnp.int32) * tm
    tile_valid = (tile_start < ends[-1]).astype(jnp.int32)
    tile_expert = jnp.minimum(jnp.sum((tile_start[:, None] >= ends[None, :]).astype(jnp.int32), axis=1),
                              N_EXPERTS - 1)
    return src.reshape(nt, 1, tm), dst.reshape(nt, 1, tm), tile_expert, tile_valid


def _moe_kernel(te_ref, tv_ref, src_ref, nsrc_ref, dst_ref, h_hbm, wg_ref, wu_ref, wd_ref, y_hbm,
                hbuf, hb16, ybuf, acc, gsem, ssem, *, tm):
    i = pl.program_id(0)
    f = pl.program_id(1)
    nt = pl.num_programs(0)
    nf = pl.num_programs(1)
    slot = i % 2
    valid = tv_ref[i] == 1
    nxt = jnp.minimum(i + 1, nt - 1)
    next_valid = jnp.logical_and(i + 1 < nt, tv_ref[nxt] == 1)

    def gather(idx_ref, s):
        def row(r, c):
            pltpu.make_async_copy(h_hbm.at[pl.ds(idx_ref[0, r], 1), :], hbuf.at[s, pl.ds(r, 1), :],
                                  gsem.at[s]).start()
            return c
        lax.fori_loop(0, tm, row, 0, unroll=8)

    def wait_scatter():
        pltpu.make_async_copy(ybuf, ybuf, ssem).wait()

    @pl.when(jnp.logical_and(valid, f == 0))
    def _():
        @pl.when(i == 0)
        def _():
            gather(src_ref, 0)

        @pl.when(next_valid)
        def _():
            gather(nsrc_ref, 1 - slot)

        pltpu.make_async_copy(hbuf.at[slot], hbuf.at[slot], gsem.at[slot]).wait()
        hb16[...] = hbuf[slot].astype(BF16)
        acc[...] = jnp.zeros_like(acc)

    @pl.when(valid)
    def _():
        h = hb16[...]
        a = jnp.dot(h, wg_ref[...], preferred_element_type=F32)
        u = jnp.dot(h, wu_ref[...], preferred_element_type=F32)
        acc[...] += jnp.dot((_silu(a) * u).astype(BF16), wd_ref[...], preferred_element_type=F32)

    @pl.when(jnp.logical_and(valid, f == nf - 1))
    def _():
        @pl.when(i > 0)
        def _():
            wait_scatter()

        ybuf[...] = acc[...]

        def row(r, c):
            pltpu.make_async_copy(ybuf.at[pl.ds(r, 1), :], y_hbm.at[pl.ds(dst_ref[0, r], 1), :], ssem).start()
            return c
        lax.fori_loop(0, tm, row, 0, unroll=8)

        @pl.when(jnp.logical_not(next_valid))
        def _():
            wait_scatter()
            ybuf[...] = jnp.zeros_like(ybuf)
            spare = pltpu.make_async_copy(ybuf, y_hbm.at[pl.ds(y_hbm.shape[0] - tm, tm), :], ssem)
            spare.start()
            spare.wait()


def _moe_ffn(h2, topi, wg, wu, wd, *, tm, tf):
    t = h2.shape[0]
    dff = wg.shape[2]
    nf = dff // tf
    src, dst, tile_expert, tile_valid = _route(topi, tm)
    nt = src.shape[0]
    idx_spec = lambda fn: pl.BlockSpec((None, 1, tm), fn, memory_space=pltpu.SMEM)
    fsel = lambda i, f, tv: jnp.where(tv[i] == 1, f, nf - 1)
    return pl.pallas_call(
        functools.partial(_moe_kernel, tm=tm),
        grid_spec=pltpu.PrefetchScalarGridSpec(
            num_scalar_prefetch=2,
            grid=(nt, nf),
            in_specs=[
                idx_spec(lambda i, f, te, tv: (i, 0, 0)),
                idx_spec(lambda i, f, te, tv: (jnp.minimum(i + 1, nt - 1), 0, 0)),
                idx_spec(lambda i, f, te, tv: (i, 0, 0)),
                pl.BlockSpec(memory_space=pl.ANY),
                pl.BlockSpec((None, D_MODEL, tf), lambda i, f, te, tv: (te[i], 0, fsel(i, f, tv))),
                pl.BlockSpec((None, D_MODEL, tf), lambda i, f, te, tv: (te[i], 0, fsel(i, f, tv))),
                pl.BlockSpec((None, tf, D_MODEL), lambda i, f, te, tv: (te[i], fsel(i, f, tv), 0)),
            ],
            out_specs=pl.BlockSpec(memory_space=pl.ANY),
            scratch_shapes=[
                pltpu.VMEM((2, tm, D_MODEL), F32),
                pltpu.VMEM((tm, D_MODEL), BF16),
                pltpu.VMEM((tm, D_MODEL), F32),
                pltpu.VMEM((tm, D_MODEL), F32),
                pltpu.SemaphoreType.DMA((2,)),
                pltpu.SemaphoreType.DMA(()),
            ],
        ),
        out_shape=jax.ShapeDtypeStruct((2 * t + tm, D_MODEL), F32),
        compiler_params=_params("arbitrary", "arbitrary"),
        name="ffn_moe",
    )(tile_expert, tile_valid, src, src, dst, h2, wg, wu, wd)


def _combine_kernel(x1_ref, ya_ref, yb_ref, topw_ref, mod_ref, gfin_ref, o_ref):
    w = topw_ref[...]
    f = w[:, 0:1] * ya_ref[...] + w[:, 1:2] * yb_ref[...]
    o_ref[...] = _rms(x1_ref[...] + mod_ref[...][5:6, :] * f, gfin_ref[...])


def _combine(x1, y2, topw, mod, gfin, *, seq, tm):
    t = x1.shape[0]
    nt = seq // tm
    nb = t // tm
    row = lambda i: (i, 0)
    return pl.pallas_call(
        _combine_kernel,
        grid=(nb,),
        in_specs=[
            pl.BlockSpec((tm, D_MODEL), row),
            pl.BlockSpec((tm, D_MODEL), row),
            pl.BlockSpec((tm, D_MODEL), lambda i: (nb + i, 0)),
            pl.BlockSpec((tm, LANES), row),
            pl.BlockSpec((None, 6, D_MODEL), lambda i: (i // nt, 0, 0)),
            _const_spec((1, D_MODEL)),
        ],
        out_specs=pl.BlockSpec((tm, D_MODEL), row),
        out_shape=jax.ShapeDtypeStruct((t, D_MODEL), F32),
        compiler_params=_params("parallel"),
        name="moe_combine",
    )(x1, y2, y2, topw, mod, gfin)


def _rope_tables(seq):
    pos = jnp.arange(seq, dtype=F32)
    inv = 1.0 / (ROPE_THETA ** (jnp.arange(0, MLA_DR, 2, dtype=F32) / MLA_DR))
    ang = pos[:, None] * inv[None, :]
    cos, sin = jnp.cos(ang), jnp.sin(ang)
    zero = jnp.zeros((seq, LANES - MLA_DR), F32)
    return (jnp.concatenate([cos, cos, zero], axis=1), jnp.concatenate([-sin, sin, zero], axis=1))


def _t5_bucket(rel):
    nb = NUM_BUCKETS // 2
    ret = (rel > 0).astype(jnp.int32) * nb
    n = jnp.abs(rel)
    max_exact = nb // 2
    nf = jnp.maximum(n, 1).astype(F32)
    large = max_exact + (jnp.log(nf / max_exact) / math.log(MAX_DISTANCE / max_exact)
                         * (nb - max_exact)).astype(jnp.int32)
    large = jnp.minimum(large, nb - 1)
    return ret + jnp.where(n < max_exact, n, large)


def _window_bias(rel_bias):
    qpos = jnp.arange(WINDOW, dtype=jnp.int32)
    jpos = jnp.arange(3 * WINDOW, dtype=jnp.int32)
    rel = jpos[None, :] - WINDOW - qpos[:, None]
    bias = rel_bias[_t5_bucket(rel)].astype(F32).transpose(2, 0, 1)
    bias = jnp.where((jnp.abs(rel) <= WINDOW)[None], bias, NEG_BIG)
    return bias.reshape(GQA_KV_HEADS, GQA_GROUP * WINDOW, 3 * WINDOW)


def _prep_layer(w_in, w_uq, w_ukv, w_out):
    half = MLA_DR // 2
    zpad = jnp.zeros((D_MODEL, LANES - MLA_DR), F32)
    kr0 = Q_LORA + KV_LORA
    k1 = w_in[:, kr0:kr0 + half]
    k2 = w_in[:, kr0 + half:kr0 + MLA_DR]
    w_in_p = jnp.concatenate(
        [w_in[:, :kr0], k1, k2, zpad, k2, k1, zpad, w_in[:, kr0 + MLA_DR:]], axis=1).astype(BF16)
    wq = w_uq.reshape(Q_LORA, MLA_HEADS, MLA_DN + MLA_DR)
    r1 = wq[:, :, MLA_DN:MLA_DN + half]
    r2 = wq[:, :, MLA_DN + half:]
    zq = jnp.zeros((Q_LORA, MLA_HEADS, LANES - MLA_DR), F32)
    w_uq_a = jnp.concatenate([wq[:, :, :MLA_DN], r1, r2, zq], axis=2).reshape(Q_LORA, MLA_HEADS * QK_PAD)
    w_uq_b = jnp.concatenate([r2, r1, zq], axis=2).reshape(Q_LORA, MLA_HEADS * LANES)
    out_a = MLA_HEADS * MLA_DV
    return (w_in_p, w_uq_a.astype(BF16), w_uq_b.astype(BF16), w_ukv.astype(BF16),
            w_out[:out_a].astype(BF16), w_out[out_a:].astype(BF16))


def _trunk(x, mods, layers, bias, g_final, *, batch, seq):
    tm = min(ROW_TILE, seq)
    cos_t, sin_t = _rope_tables(seq)
    for l, p in enumerate(layers):
        q, k, v, gq, gk, gv = _premix(x, mods[l], p["g_norm_mix"], cos_t, sin_t, p["w_in_p"], p["g_q_lat"],
                                      p["w_uq_a"], p["w_uq_b"], p["g_kv_lat"], p["w_ukv"], seq=seq, tm=tm)
        tq_sub = min(MLA_TQ_SUB, seq)
        tq = min(MLA_TQ, seq)
        oa = _mla(q, k, v, batch=batch, seq=seq, tq=tq, tk=min(MLA_TK, seq), nsub=tq // tq_sub)
        ob = _window(gq, gk, gv, bias, p["sink_col"], seq=seq, rows=min(ROW_TILE, seq))
        res = _postmix(oa, ob, x, mods[l], p["g_out_a"], p["g_out_b"], p["w_out_a"], p["w_out_b"],
                       p["g_norm_ffn"], p.get("w_router"), seq=seq, tm=tm)
        if "w_router" in p:
            assert l == len(layers) - 1
            x1, h2, topw, topi = res
            y2 = _moe_ffn(h2, topi[:, :2], p["w_gate"], p["w_up"], p["w_down"], tm=ROW_TILE, tf=FF_TILE)
            x = _combine(x1, y2, topw, mods[l], g_final, seq=seq, tm=tm)
        else:
            x1, h2 = res
            x = _ffn(x1, h2, mods[l], p["w_gate"], p["w_up"], p["w_down"], seq=seq, tm=tm, tf=FF_TILE)
    return x


def kernel(x_prompt, x_sample, c_prompt, c_sample, rel_bias, w_ada, b_ada, g_norm_mix, g_norm_ffn, w_in, g_q_lat, w_uq, g_kv_lat, w_ukv, sink, g_out_a, g_out_b, w_out, w_gate_d, w_up_d, w_down_d, w_router, w_gate_e, w_up_e, w_down_e, g_final):
    bp, sp, _ = x_prompt.shape
    bs, ss, _ = x_sample.shape
    rows = -(-(bp + bs) // 8) * 8
    c_all = jnp.concatenate([c_prompt, c_sample, jnp.zeros((rows - bp - bs, D_MODEL), F32)], axis=0)
    mod = _ada_mod(c_all, w_ada, b_ada)
    mod_p = [mod[l, :bp].reshape(bp, 6, D_MODEL) for l in range(DEPTH)]
    mod_s = [mod[l, bp:bp + bs].reshape(bs, 6, D_MODEL) for l in range(DEPTH)]
    bias = _window_bias(rel_bias)
    layers = []
    for l in range(DEPTH):
        w_in_p, w_uq_a, w_uq_b, w_ukv_b, w_out_a, w_out_b = _prep_layer(w_in[l], w_uq[l], w_ukv[l], w_out[l])
        out_a = MLA_HEADS * MLA_DV
        p = dict(
            g_norm_mix=g_norm_mix[l][None], g_norm_ffn=g_norm_ffn[l][None],
            w_in_p=w_in_p, g_q_lat=g_q_lat[l][None], w_uq_a=w_uq_a, w_uq_b=w_uq_b,
            g_kv_lat=g_kv_lat[l][None], w_ukv=w_ukv_b,
            sink_col=jnp.repeat(sink[l].astype(F32), WINDOW).reshape(GQA_KV_HEADS, GQA_GROUP * WINDOW, 1),
            g_out_a=g_out_a[l][None], g_out_b=g_out_b[l][None], w_out_a=w_out_a, w_out_b=w_out_b,
        )
        i = l // 2
        if l % 2 == 0:
            p.update(w_gate=w_gate_d[i].astype(BF16), w_up=w_up_d[i].astype(BF16), w_down=w_down_d[i].astype(BF16))
        else:
            wr = jnp.concatenate([w_router[i], jnp.zeros((D_MODEL, LANES - N_EXPERTS), F32)], axis=1)
            wr_hi = wr.astype(BF16)
            wr = jnp.stack([wr_hi, (wr - wr_hi.astype(F32)).astype(BF16)])
            p.update(w_gate=w_gate_e[i].astype(BF16), w_up=w_up_e[i].astype(BF16),
                     w_down=w_down_e[i].astype(BF16), w_router=wr)
        layers.append(p)
    gfin = g_final[None]
    y_s = _trunk(x_sample.reshape(bs * ss, D_MODEL), mod_s, layers, bias, gfin, batch=bs, seq=ss)
    y_p = _trunk(x_prompt.reshape(bp * sp, D_MODEL), mod_p, layers, bias, gfin, batch=bp, seq=sp)
    return (y_p.reshape(bp, sp, D_MODEL), y_s.reshape(bs, ss, D_MODEL))
```

```python
import functools
import math

import jax
import jax.numpy as jnp
from jax import lax
from jax.experimental import pallas as pl
from jax.experimental.pallas import tpu as pltpu

F32 = jnp.float32
BF16 = jnp.bfloat16

D_MODEL = 2048
DEPTH = 2
EPS = 1e-6
MLA_HEADS = 8
MLA_DN = 128
MLA_DR = 64
MLA_DV = 128
Q_LORA = 512
KV_LORA = 512
ROPE_THETA = 10000.0
MLA_SCALE = 1.0 / math.sqrt(MLA_DN + MLA_DR)
MLA_QSCALE = MLA_SCALE * math.log2(math.e)
GQA_HEADS = 8
GQA_KV_HEADS = 2
GQA_GROUP = GQA_HEADS // GQA_KV_HEADS
GQA_DH = 128
WINDOW = 128
GQA_SCALE = 1.0 / math.sqrt(GQA_DH)
NUM_BUCKETS = 32
MAX_DISTANCE = 128
N_EXPERTS = 8
NEG_BIG = -1e30

LANES = 128
SUBLANES = 8
V7X_VMEM_LIMIT = 56 * 1024 * 1024

QK_PAD = 2 * LANES
VT_ROWS = MLA_DV + 16
MLA_TQ = 2048
MLA_TQ_SUB = 256
MLA_TK = 512
ROW_TILE = 512
FF_TILE = 512
Z_CQ = 0
Z_CKV = Z_CQ + Q_LORA
Z_KRA = Z_CKV + KV_LORA
Z_KRB = Z_KRA + LANES
Z_GQ = Z_KRB + LANES
Z_GK = Z_GQ + GQA_HEADS * GQA_DH
Z_GV = Z_GK + GQA_KV_HEADS * GQA_DH
Z_END = Z_GV + GQA_KV_HEADS * GQA_DH


def _params(*sem):
    return pltpu.CompilerParams(dimension_semantics=sem, vmem_limit_bytes=V7X_VMEM_LIMIT)


def _rms(x, g):
    return x * lax.rsqrt(jnp.mean(x * x, axis=-1, keepdims=True) + EPS) * g


def _silu(a):
    return a * (1.0 / (1.0 + jnp.exp(-a)))


def _const_spec(shape):
    nd = len(shape)
    return pl.BlockSpec(shape, lambda *_: (0,) * nd)


def _ada_kernel(c_ref, w_ref, b_ref, o_ref):
    cs = _silu(c_ref[...])
    o_ref[...] = jnp.dot(cs, w_ref[...], preferred_element_type=F32,
                         precision=lax.Precision.HIGHEST) + b_ref[...]


def _ada_mod(c_all, w_ada, b_ada):
    rows = c_all.shape[0]
    n = w_ada.shape[-1]
    tn = 1024
    return pl.pallas_call(
        _ada_kernel,
        grid=(DEPTH, n // tn),
        in_specs=[
            pl.BlockSpec((rows, D_MODEL), lambda l, j: (0, 0)),
            pl.BlockSpec((None, D_MODEL, tn), lambda l, j: (l, 0, j)),
            pl.BlockSpec((None, 1, tn), lambda l, j: (l, 0, j)),
        ],
        out_specs=pl.BlockSpec((None, rows, tn), lambda l, j: (l, 0, j)),
        out_shape=jax.ShapeDtypeStruct((DEPTH, rows, n), F32),
        compiler_params=_params("parallel", "parallel"),
        name="ada_mod",
    )(c_all, w_ada, b_ada.reshape(DEPTH, 1, n))


def _premix_kernel(x_ref, mod_ref, gn_ref, cos_ref, sin_ref, win_ref, gq_ref, wuqa_ref, wuqb_ref,
                   gkv_ref, wukv_ref, q_ref, k_ref, v_ref, wq_ref, wk_ref, wv_ref):
    mod = mod_ref[...]
    h = _rms(x_ref[...], gn_ref[...]) * (1.0 + mod[1:2, :]) + mod[0:1, :]
    z = jnp.dot(h.astype(BF16), win_ref[...], preferred_element_type=F32)
    cqn = _rms(z[:, Z_CQ:Z_CKV], gq_ref[...]).astype(BF16)
    ckvn = _rms(z[:, Z_CKV:Z_KRA], gkv_ref[...]).astype(BF16)
    cos_t = cos_ref[...]
    sin_t = sin_ref[...]
    kr = (z[:, Z_KRA:Z_KRB] * cos_t + z[:, Z_KRB:Z_GQ] * sin_t).astype(BF16)
    wq_ref[...] = (z[:, Z_GQ:Z_GK] * GQA_SCALE).astype(BF16)
    wk_ref[...] = z[:, Z_GK:Z_GV].astype(BF16)
    wv_ref[...] = z[:, Z_GV:Z_END].astype(BF16)
    qa = jnp.dot(cqn, wuqa_ref[...], preferred_element_type=F32)
    qb = jnp.dot(cqn, wuqb_ref[...], preferred_element_type=F32)
    kv = jnp.dot(ckvn, wukv_ref[...], preferred_element_type=F32)
    sub = lax.broadcasted_iota(jnp.int32, (VT_ROWS - MLA_DV, kr.shape[0]), 0)
    ones_row = jnp.where(sub == 0, 1.0, 0.0).astype(BF16)
    for hh in range(MLA_HEADS):
        a0 = hh * QK_PAD
        q_ref[hh, :, 0:LANES] = (qa[:, a0:a0 + LANES] * MLA_QSCALE).astype(BF16)
        q_rope = qa[:, a0 + LANES:a0 + QK_PAD] * cos_t + qb[:, hh * LANES:(hh + 1) * LANES] * sin_t
        q_ref[hh, :, LANES:QK_PAD] = (q_rope * MLA_QSCALE).astype(BF16)
        k_ref[hh, :, 0:LANES] = kv[:, a0:a0 + LANES].astype(BF16)
        k_ref[hh, :, LANES:QK_PAD] = kr
        v_ref[hh, 0:MLA_DV, :] = kv[:, a0 + LANES:a0 + QK_PAD].T.astype(BF16)
        v_ref[hh, MLA_DV:VT_ROWS, :] = ones_row


def _premix(x, mod, gn, cos_t, sin_t, w_in_p, g_q, w_uq_a, w_uq_b, g_kv, w_ukv, *, seq, tm):
    t = x.shape[0]
    nt = seq // tm
    row = lambda i: (i, 0)
    hrow = lambda i: (0, i, 0)
    return pl.pallas_call(
        _premix_kernel,
        grid=(t // tm,),
        in_specs=[
            pl.BlockSpec((tm, D_MODEL), row),
            pl.BlockSpec((None, 6, D_MODEL), lambda i: (i // nt, 0, 0)),
            _const_spec((1, D_MODEL)),
            pl.BlockSpec((tm, LANES), lambda i: (i % nt, 0)),
            pl.BlockSpec((tm, LANES), lambda i: (i % nt, 0)),
            _const_spec(w_in_p.shape),
            _const_spec((1, Q_LORA)),
            _const_spec(w_uq_a.shape),
            _const_spec(w_uq_b.shape),
            _const_spec((1, KV_LORA)),
            _const_spec(w_ukv.shape),
        ],
        out_specs=[
            pl.BlockSpec((MLA_HEADS, tm, QK_PAD), hrow),
            pl.BlockSpec((MLA_HEADS, tm, QK_PAD), hrow),
            pl.BlockSpec((MLA_HEADS, VT_ROWS, tm), lambda i: (0, 0, i)),
            pl.BlockSpec((tm, GQA_HEADS * GQA_DH), row),
            pl.BlockSpec((tm, GQA_KV_HEADS * GQA_DH), row),
            pl.BlockSpec((tm, GQA_KV_HEADS * GQA_DH), row),
        ],
        out_shape=[
            jax.ShapeDtypeStruct((MLA_HEADS, t, QK_PAD), BF16),
            jax.ShapeDtypeStruct((MLA_HEADS, t, QK_PAD), BF16),
            jax.ShapeDtypeStruct((MLA_HEADS, VT_ROWS, t), BF16),
            jax.ShapeDtypeStruct((t, GQA_HEADS * GQA_DH), BF16),
            jax.ShapeDtypeStruct((t, GQA_KV_HEADS * GQA_DH), BF16),
            jax.ShapeDtypeStruct((t, GQA_KV_HEADS * GQA_DH), BF16),
        ],
        compiler_params=_params("parallel"),
        name="premix",
    )(x, mod, gn, cos_t, sin_t, w_in_p, g_q, w_uq_a, w_uq_b, g_kv, w_ukv)


def _mla_kernel(q_ref, k_ref, v_ref, o_ref, qt_scr, s_scr, p_scr, m_scr, a_scr, acc_scr, *, tk, nk, nsub):
    tq = q_ref.shape[0] // nsub
    subs = range(nsub)
    for i in subs:
        qt_scr[i] = q_ref[i * tq:(i + 1) * tq, :].astype(F32).T.astype(BF16)

    def scores(j, buf, i):
        k = k_ref[pl.ds(pl.multiple_of(j * tk, tk), tk), :]
        s_scr[buf, i] = jnp.dot(k, qt_scr[i], preferred_element_type=F32)

    def softmax(buf, i):
        s = s_scr[buf, i]
        m_old = m_scr[i]
        m_new = jnp.maximum(m_old, jnp.max(s, axis=0, keepdims=True))
        m_scr[i] = m_new
        a_scr[buf, i] = jnp.exp2(m_old - m_new)
        p_scr[buf, i] = jnp.exp2(s - m_new).astype(BF16)

    def values(j, buf, i):
        v = v_ref[:, pl.ds(pl.multiple_of(j * tk, tk), tk)]
        pv = jnp.dot(v, p_scr[buf, i], preferred_element_type=F32)
        acc_scr[i] = a_scr[buf, i] * acc_scr[i] + pv

    m_scr[...] = jnp.full(m_scr.shape, -jnp.inf, F32)
    acc_scr[...] = jnp.zeros(acc_scr.shape, F32)
    for i in subs:
        scores(0, 0, i)
        scores(1, 1, i)
        softmax(0, i)

    def body(u, carry):
        t = 2 * u + 1
        for i in subs:
            values(t - 1, 0, i)
            scores(t + 1, 0, i)
            softmax(1, i)
        for i in subs:
            values(t, 1, i)
            scores(t + 2, 1, i)
            softmax(0, i)
        return carry

    lax.fori_loop(0, (nk - 2) // 2, body, 0)
    for i in subs:
        softmax(1, i)
        values(nk - 2, 0, i)
        values(nk - 1, 1, i)
    for i in subs:
        acc = acc_scr[i]
        o_ref[i * tq:(i + 1) * tq, :] = (acc[0:MLA_DV, :] * (1.0 / acc[MLA_DV:MLA_DV + 1, :])).T.astype(BF16)


def _mla(q, k, v, *, batch, seq, tq, tk, nsub):
    t = batch * seq
    nq = seq // tq
    nk = seq // tk
    assert nk >= 2 and nk % 2 == 0, (seq, tk)
    sub = tq // nsub
    return pl.pallas_call(
        functools.partial(_mla_kernel, tk=tk, nk=nk, nsub=nsub),
        grid=(batch, MLA_HEADS, nq),
        in_specs=[
            pl.BlockSpec((None, tq, QK_PAD), lambda b, h, i: (h, b * nq + i, 0)),
            pl.BlockSpec((None, seq, QK_PAD), lambda b, h, i: (h, b, 0)),
            pl.BlockSpec((None, VT_ROWS, seq), lambda b, h, i: (h, 0, b)),
        ],
        out_specs=pl.BlockSpec((tq, MLA_DV), lambda b, h, i: (b * nq + i, h)),
        out_shape=jax.ShapeDtypeStruct((t, MLA_HEADS * MLA_DV), BF16),
        scratch_shapes=[
            pltpu.VMEM((nsub, QK_PAD, sub), BF16),
            pltpu.VMEM((2, nsub, tk, sub), F32),
            pltpu.VMEM((2, nsub, tk, sub), BF16),
            pltpu.VMEM((nsub, 1, sub), F32),
            pltpu.VMEM((2, nsub, 1, sub), F32),
            pltpu.VMEM((nsub, VT_ROWS, sub), F32),
        ],
        compiler_params=_params("parallel", "parallel", "parallel"),
        name="mla_attn",
    )(q, k, v)


def _window_kernel(q_ref, kp_ref, kc_ref, kn_ref, vp_ref, vc_ref, vn_ref, bias_ref, sink_ref, o_ref,
                   kcat, vcat, *, rows, nt):
    i = pl.program_id(0)
    nsub = rows // WINDOW
    kcat[0:WINDOW, :] = kp_ref[...]
    kcat[WINDOW:WINDOW + rows, :] = kc_ref[...]
    kcat[WINDOW + rows:, :] = kn_ref[...]
    vcat[0:WINDOW, :] = vp_ref[...]
    vcat[WINDOW:WINDOW + rows, :] = vc_ref[...]
    vcat[WINDOW + rows:, :] = vn_ref[...]
    first_tile = (i % nt) == 0
    last_tile = (i % nt) == nt - 1
    col = lax.broadcasted_iota(jnp.int32, (1, 3 * WINDOW), 1)

    chains = [(n, kh) for n in range(nsub) for kh in range(GQA_KV_HEADS)]
    s_all = []
    for n, kh in chains:
        r0 = n * WINDOW
        qall = q_ref[r0:r0 + WINDOW, kh * GQA_GROUP * GQA_DH:(kh + 1) * GQA_GROUP * GQA_DH]
        qs = jnp.concatenate([qall[:, g * GQA_DH:(g + 1) * GQA_DH] for g in range(GQA_GROUP)], axis=0)
        kk = kcat[r0:r0 + 3 * WINDOW, kh * GQA_DH:(kh + 1) * GQA_DH]
        s = lax.dot_general(qs, kk, (((1,), (1,)), ((), ())), preferred_element_type=F32) + bias_ref[kh]
        if n == 0:
            s = s + jnp.where(jnp.logical_and(first_tile, col < WINDOW), NEG_BIG, 0.0)
        if n == nsub - 1:
            s = s + jnp.where(jnp.logical_and(last_tile, col >= 2 * WINDOW), NEG_BIG, 0.0)
        s_all.append(s)
    m_all = [jnp.maximum(jnp.max(s, axis=1, keepdims=True), sink_ref[kh]) for s, (n, kh) in zip(s_all, chains)]
    e_all = [jnp.exp(s - m) for s, m in zip(s_all, m_all)]
    d_all = [jnp.sum(e, axis=1, keepdims=True) + jnp.exp(sink_ref[kh] - m)
             for e, m, (n, kh) in zip(e_all, m_all, chains)]
    for e, d, (n, kh) in zip(e_all, d_all, chains):
        r0 = n * WINDOW
        vv = vcat[r0:r0 + 3 * WINDOW, kh * GQA_DH:(kh + 1) * GQA_DH]
        o = jnp.dot(e.astype(BF16), vv, preferred_element_type=F32) * (1.0 / d)
        for g in range(GQA_GROUP):
            hcol = (kh * GQA_GROUP + g) * GQA_DH
            o_ref[r0:r0 + WINDOW, hcol:hcol + GQA_DH] = o[g * WINDOW:(g + 1) * WINDOW].astype(BF16)


def _window(gq, gk, gv, bias, sink_col, *, seq, rows):
    t = gq.shape[0]
    nt = seq // rows
    rb = rows // WINDOW
    nblk = t // WINDOW
    kvw = GQA_KV_HEADS * GQA_DH
    prev = lambda i: (jnp.maximum(i * rb - 1, 0), 0)
    nxt = lambda i: (jnp.minimum((i + 1) * rb, nblk - 1), 0)
    cur = lambda i: (i, 0)
    return pl.pallas_call(
        functools.partial(_window_kernel, rows=rows, nt=nt),
        grid=(t // rows,),
        in_specs=[
            pl.BlockSpec((rows, GQA_HEADS * GQA_DH), cur),
            pl.BlockSpec((WINDOW, kvw), prev),
            pl.BlockSpec((rows, kvw), cur),
            pl.BlockSpec((WINDOW, kvw), nxt),
            pl.BlockSpec((WINDOW, kvw), prev),
            pl.BlockSpec((rows, kvw), cur),
            pl.BlockSpec((WINDOW, kvw), nxt),
            _const_spec(bias.shape),
            _const_spec(sink_col.shape),
        ],
        out_specs=pl.BlockSpec((rows, GQA_HEADS * GQA_DH), cur),
        out_shape=jax.ShapeDtypeStruct((t, GQA_HEADS * GQA_DH), BF16),
        scratch_shapes=[pltpu.VMEM((rows + 2 * WINDOW, kvw), BF16),
                        pltpu.VMEM((rows + 2 * WINDOW, kvw), BF16)],
        compiler_params=_params("parallel"),
        name="window_attn",
    )(gq, gk, gk, gk, gv, gv, gv, bias, sink_col)


def _postmix_kernel(oa_ref, ob_ref, x_ref, mod_ref, ga_ref, gb_ref, wa_ref, wb_ref, gf_ref, *rest, moe):
    if moe:
        wr_ref, x1_ref, h2_ref, topw_ref, topi_ref = rest
    else:
        x1_ref, h2_ref = rest
    mod = mod_ref[...]
    na = _rms(oa_ref[...].astype(F32), ga_ref[...]).astype(BF16)
    nb = _rms(ob_ref[...].astype(F32), gb_ref[...]).astype(BF16)
    mix = (jnp.dot(na, wa_ref[...], preferred_element_type=F32)
           + jnp.dot(nb, wb_ref[...], preferred_element_type=F32))
    x1 = x_ref[...] + mod[2:3, :] * mix
    x1_ref[...] = x1
    h2 = _rms(x1, gf_ref[...]) * (1.0 + mod[4:5, :]) + mod[3:4, :]
    h2_ref[...] = h2.astype(h2_ref.dtype)
    if moe:
        h_hi = h2.astype(BF16)
        h_lo = (h2 - h_hi.astype(F32)).astype(BF16)
        logits = (jnp.dot(h_hi, wr_ref[0], preferred_element_type=F32)
                  + jnp.dot(h_lo, wr_ref[0], preferred_element_type=F32)
                  + jnp.dot(h_hi, wr_ref[1], preferred_element_type=F32))
        lane = lax.broadcasted_iota(jnp.int32, logits.shape, 1)
        lg = jnp.where(lane < N_EXPERTS, logits, -jnp.inf)
        m1 = jnp.max(lg, axis=1, keepdims=True)
        i1 = jnp.min(jnp.where(lg == m1, lane, LANES), axis=1, keepdims=True)
        lg2 = jnp.where(lane == i1, -jnp.inf, lg)
        m2 = jnp.max(lg2, axis=1, keepdims=True)
        i2 = jnp.min(jnp.where(lg2 == m2, lane, LANES), axis=1, keepdims=True)
        e2 = jnp.exp(m2 - m1)
        w1 = 1.0 / (1.0 + e2)
        w2 = e2 * w1
        topw_ref[...] = jnp.where(lane == 0, w1, jnp.where(lane == 1, w2, 0.0))
        topi_ref[...] = jnp.where(lane == 0, i1, jnp.where(lane == 1, i2, 0))


def _postmix(oa, ob, x, mod, ga, gb, wa, wb, gf, wr, *, seq, tm):
    t = x.shape[0]
    nt = seq // tm
    moe = wr is not None
    row = lambda i: (i, 0)
    half = oa.shape[1]
    in_specs = [
        pl.BlockSpec((tm, half), row),
        pl.BlockSpec((tm, half), row),
        pl.BlockSpec((tm, D_MODEL), row),
        pl.BlockSpec((None, 6, D_MODEL), lambda i: (i // nt, 0, 0)),
        _const_spec((1, half)),
        _const_spec((1, half)),
        _const_spec(wa.shape),
        _const_spec(wb.shape),
        _const_spec((1, D_MODEL)),
    ]
    out_specs = [pl.BlockSpec((tm, D_MODEL), row), pl.BlockSpec((tm, D_MODEL), row)]
    out_shape = [jax.ShapeDtypeStruct((t, D_MODEL), F32), jax.ShapeDtypeStruct((t, D_MODEL), F32 if moe else BF16)]
    args = [oa, ob, x, mod, ga, gb, wa, wb, gf]
    if moe:
        in_specs.append(_const_spec(wr.shape))
        out_specs += [pl.BlockSpec((tm, LANES), row), pl.BlockSpec((tm, LANES), row)]
        out_shape += [jax.ShapeDtypeStruct((t, LANES), F32), jax.ShapeDtypeStruct((t, LANES), jnp.int32)]
        args.append(wr)
    return pl.pallas_call(
        functools.partial(_postmix_kernel, moe=moe),
        grid=(t // tm,),
        in_specs=in_specs,
        out_specs=out_specs,
        out_shape=out_shape,
        compiler_params=_params("parallel"),
        name="postmix_moe" if moe else "postmix",
    )(*args)


def _ffn_kernel(x1_ref, h_ref, mod_ref, wg_ref, wu_ref, wd_ref, o_ref, acc_ref):
    f = pl.program_id(1)

    @pl.when(f == 0)
    def _():
        acc_ref[...] = jnp.zeros_like(acc_ref)

    h = h_ref[...]
    a = jnp.dot(h, wg_ref[...], preferred_element_type=F32)
    u = jnp.dot(h, wu_ref[...], preferred_element_type=F32)
    acc_ref[...] += jnp.dot((_silu(a) * u).astype(BF16), wd_ref[...], preferred_element_type=F32)

    @pl.when(f == pl.num_programs(1) - 1)
    def _():
        o_ref[...] = x1_ref[...] + mod_ref[...][5:6, :] * acc_ref[...]


def _ffn(x1, h2, mod, wg, wu, wd, *, seq, tm, tf):
    t = x1.shape[0]
    nt = seq // tm
    dff = wg.shape[1]
    row = lambda i, f: (i, 0)
    return pl.pallas_call(
        _ffn_kernel,
        grid=(t // tm, dff // tf),
        in_specs=[
            pl.BlockSpec((tm, D_MODEL), row),
            pl.BlockSpec((tm, D_MODEL), row),
            pl.BlockSpec((None, 6, D_MODEL), lambda i, f: (i // nt, 0, 0)),
            pl.BlockSpec((D_MODEL, tf), lambda i, f: (0, f)),
            pl.BlockSpec((D_MODEL, tf), lambda i, f: (0, f)),
            pl.BlockSpec((tf, D_MODEL), lambda i, f: (f, 0)),
        ],
        out_specs=pl.BlockSpec((tm, D_MODEL), row),
        out_shape=jax.ShapeDtypeStruct((t, D_MODEL), F32),
        scratch_shapes=[pltpu.VMEM((tm, D_MODEL), F32)],
        compiler_params=_params("parallel", "arbitrary"),
        name="ffn_dense",
    )(x1, h2, mod, wg, wu, wd)


def _route(topi, tm, rp):
    t = topi.shape[0]
    na = 2 * t
    nt = na // tm + N_EXPERTS
    flat_e = topi.reshape(na)
    onehot = (flat_e[:, None] == jnp.arange(N_EXPERTS, dtype=jnp.int32)[None, :]).astype(jnp.int32)
    csum = jnp.cumsum(onehot, axis=0)
    rank = jnp.sum((csum - onehot) * onehot, axis=1)
    padded = ((csum[-1] + tm - 1) // tm) * tm
    ends = jnp.cumsum(padded)
    pos = (ends - padded)[flat_e] + rank
    pair = jnp.full((nt * tm,), -1, jnp.int32).at[pos].set(jnp.arange(na, dtype=jnp.int32))
    pair = jnp.pad(pair.reshape(nt, tm), ((0, 0), (0, rp - tm)), constant_values=-1)
    spare = na + (jnp.arange(nt + 1, dtype=jnp.int32)[:, None] % 3) * rp + jnp.arange(rp, dtype=jnp.int32)[None, :]
    src = jnp.where(pair >= 0, pair // 2, 0)
    dst = jnp.where(pair >= 0, (pair % 2) * t + pair // 2, spare[1:])
    dst = jnp.concatenate([spare[:1], dst], axis=0)
    tile_start = jnp.arange(nt, dtype=jnp.int32) * tm
    tile_valid = (tile_start < ends[-1]).astype(jnp.int32)
    tile_expert = jnp.minimum(jnp.sum((tile_start[:, None] >= ends[None, :]).astype(jnp.int32), axis=1),
                              N_EXPERTS - 1)
    return src.reshape(nt, 1, rp), dst.reshape(nt + 1, 1, rp), tile_expert, tile_valid


def _moe_kernel(te_ref, tv_ref, src0_ref, src1_ref, src2_ref, pdst_ref, dst_ref, h_hbm, wg_ref, wu_ref, wd_ref,
                y_hbm, hbuf, hb16, acc, gsem, ssem, *, tm, ch):
    i = pl.program_id(0)
    f = pl.program_id(1)
    nt = pl.num_programs(0)
    nf = pl.num_programs(1)
    rp = nf * ch
    cur = i % 3
    oth = (i + 2) % 3
    valid = tv_ref[i] == 1
    last_valid = jnp.logical_and(valid, jnp.logical_or(i == nt - 1, tv_ref[jnp.minimum(i + 1, nt - 1)] == 0))

    def gather_row(idx_ref, r, b):
        pltpu.make_async_copy(h_hbm.at[pl.ds(idx_ref[0, r], 1), :], hbuf.at[b, pl.ds(r, 1), :],
                              gsem.at[b]).start()

    def scatter_row(idx_ref, r, b):
        pltpu.make_async_copy(acc.at[b, pl.ds(r, 1), :], y_hbm.at[pl.ds(idx_ref[0, r], 1), :],
                              ssem.at[b]).start()

    def wait_rows(buf, sem, b):
        pltpu.make_async_copy(buf.at[b], buf.at[b], sem.at[b]).wait()

    @pl.when(jnp.logical_and(i == 0, f == 0))
    def _():
        acc[...] = jnp.zeros_like(acc)

        def first(r, c):
            gather_row(src0_ref, r, 0)
            gather_row(src1_ref, r, 1)
            return c
        lax.fori_loop(0, rp, first, 0, unroll=8)

    @pl.when(jnp.logical_and(valid, f == 0))
    def _():
        wait_rows(hbuf, gsem, cur)
        hb16[...] = hbuf[cur, 0:tm, :].astype(BF16)

        @pl.when(i >= 2)
        def _():
            wait_rows(acc, ssem, cur)

        acc[cur, 0:tm, :] = jnp.zeros((tm, D_MODEL), F32)

    @pl.when(valid)
    def _():
        for c in range(ch):
            r = f * ch + c
            gather_row(src2_ref, r, oth)
            scatter_row(pdst_ref, r, oth)
        h = hb16[...]
        a = jnp.dot(h, wg_ref[...], preferred_element_type=F32)
        u = jnp.dot(h, wu_ref[...], preferred_element_type=F32)
        acc[cur, 0:tm, :] += jnp.dot((_silu(a) * u).astype(BF16), wd_ref[...], preferred_element_type=F32)

    @pl.when(jnp.logical_and(last_valid, f == nf - 1))
    def _():
        def own(r, c):
            scatter_row(dst_ref, r, cur)
            return c
        lax.fori_loop(0, rp, own, 0, unroll=8)
        @pl.when(i >= 1)
        def _():
            wait_rows(acc, ssem, (i + 1) % 3)

        wait_rows(acc, ssem, oth)
        wait_rows(acc, ssem, cur)
        wait_rows(hbuf, gsem, (i + 1) % 3)
        wait_rows(hbuf, gsem, oth)
        hbuf[0] = jnp.zeros((rp, D_MODEL), F32)
        for b in range(3):
            spare = pltpu.make_async_copy(hbuf.at[0], y_hbm.at[pl.ds(y_hbm.shape[0] - (b + 1) * rp, rp), :],
                                          gsem.at[0])
            spare.start()
            spare.wait()


def _moe_ffn(h2, topi, wg, wu, wd, *, tm, tf):
    t = h2.shape[0]
    dff = wg.shape[2]
    nf = dff // tf
    ch = -(-tm // (nf * SUBLANES)) * SUBLANES
    rp = nf * ch
    src, dst, tile_expert, tile_valid = _route(topi, tm, rp)
    nt = src.shape[0]
    idx_spec = lambda fn: pl.BlockSpec((None, 1, rp), fn, memory_space=pltpu.SMEM)
    fsel = lambda i, f, tv: jnp.where(tv[i] == 1, f, nf - 1)
    return pl.pallas_call(
        functools.partial(_moe_kernel, tm=tm, ch=ch),
        grid_spec=pltpu.PrefetchScalarGridSpec(
            num_scalar_prefetch=2,
            grid=(nt, nf),
            in_specs=[
                idx_spec(lambda i, f, te, tv: (i, 0, 0)),
                idx_spec(lambda i, f, te, tv: (jnp.minimum(i + 1, nt - 1), 0, 0)),
                idx_spec(lambda i, f, te, tv: (jnp.minimum(i + 2, nt - 1), 0, 0)),
                idx_spec(lambda i, f, te, tv: (i, 0, 0)),
                idx_spec(lambda i, f, te, tv: (i + 1, 0, 0)),
                pl.BlockSpec(memory_space=pl.ANY),
                pl.BlockSpec((None, D_MODEL, tf), lambda i, f, te, tv: (te[i], 0, fsel(i, f, tv))),
                pl.BlockSpec((None, D_MODEL, tf), lambda i, f, te, tv: (te[i], 0, fsel(i, f, tv))),
                pl.BlockSpec((None, tf, D_MODEL), lambda i, f, te, tv: (te[i], fsel(i, f, tv), 0)),
            ],
            out_specs=pl.BlockSpec(memory_space=pl.ANY),
            scratch_shapes=[
                pltpu.VMEM((3, rp, D_MODEL), F32),
                pltpu.VMEM((tm, D_MODEL), BF16),
                pltpu.VMEM((3, rp, D_MODEL), F32),
                pltpu.SemaphoreType.DMA((3,)),
                pltpu.SemaphoreType.DMA((3,)),
            ],
        ),
        out_shape=jax.ShapeDtypeStruct((2 * t + 3 * rp, D_MODEL), F32),
        compiler_params=_params("arbitrary", "arbitrary"),
        name="ffn_moe",
    )(tile_expert, tile_valid, src, src, src, dst, dst, h2, wg, wu, wd)


def _combine_kernel(x1_ref, ya_ref, yb_ref, topw_ref, mod_ref, gfin_ref, o_ref):
    w = topw_ref[...]
    f = w[:, 0:1] * ya_ref[...] + w[:, 1:2] * yb_ref[...]
    o_ref[...] = _rms(x1_ref[...] + mod_ref[...][5:6, :] * f, gfin_ref[...])


def _combine(x1, y2, topw, mod, gfin, *, seq, tm):
    t = x1.shape[0]
    nt = seq // tm
    nb = t // tm
    row = lambda i: (i, 0)
    return pl.pallas_call(
        _combine_kernel,
        grid=(nb,),
        in_specs=[
            pl.BlockSpec((tm, D_MODEL), row),
            pl.BlockSpec((tm, D_MODEL), row),
            pl.BlockSpec((tm, D_MODEL), lambda i: (nb + i, 0)),
            pl.BlockSpec((tm, LANES), row),
            pl.BlockSpec((None, 6, D_MODEL), lambda i: (i // nt, 0, 0)),
            _const_spec((1, D_MODEL)),
        ],
        out_specs=pl.BlockSpec((tm, D_MODEL), row),
        out_shape=jax.ShapeDtypeStruct((t, D_MODEL), F32),
        compiler_params=_params("parallel"),
        name="moe_combine",
    )(x1, y2, y2, topw, mod, gfin)


def _rope_tables(seq):
    pos = jnp.arange(seq, dtype=F32)
    inv = 1.0 / (ROPE_THETA ** (jnp.arange(0, MLA_DR, 2, dtype=F32) / MLA_DR))
    ang = pos[:, None] * inv[None, :]
    cos, sin = jnp.cos(ang), jnp.sin(ang)
    zero = jnp.zeros((seq, LANES - MLA_DR), F32)
    return (jnp.concatenate([cos, cos, zero], axis=1), jnp.concatenate([-sin, sin, zero], axis=1))


def _t5_bucket(rel):
    nb = NUM_BUCKETS // 2
    ret = (rel > 0).astype(jnp.int32) * nb
    n = jnp.abs(rel)
    max_exact = nb // 2
    nf = jnp.maximum(n, 1).astype(F32)
    large = max_exact + (jnp.log(nf / max_exact) / math.log(MAX_DISTANCE / max_exact)
                         * (nb - max_exact)).astype(jnp.int32)
    large = jnp.minimum(large, nb - 1)
    return ret + jnp.where(n < max_exact, n, large)


def _window_bias(rel_bias):
    qpos = jnp.arange(WINDOW, dtype=jnp.int32)
    jpos = jnp.arange(3 * WINDOW, dtype=jnp.int32)
    rel = jpos[None, :] - WINDOW - qpos[:, None]
    bias = rel_bias[_t5_bucket(rel)].astype(F32).transpose(2, 0, 1)
    bias = jnp.where((jnp.abs(rel) <= WINDOW)[None], bias, NEG_BIG)
    return bias.reshape(GQA_KV_HEADS, GQA_GROUP * WINDOW, 3 * WINDOW)


def _prep_layer(w_in, w_uq, w_ukv, w_out):
    half = MLA_DR // 2
    zpad = jnp.zeros((D_MODEL, LANES - MLA_DR), F32)
    kr0 = Q_LORA + KV_LORA
    k1 = w_in[:, kr0:kr0 + half]
    k2 = w_in[:, kr0 + half:kr0 + MLA_DR]
    w_in_p = jnp.concatenate(
        [w_in[:, :kr0], k1, k2, zpad, k2, k1, zpad, w_in[:, kr0 + MLA_DR:]], axis=1).astype(BF16)
    wq = w_uq.reshape(Q_LORA, MLA_HEADS, MLA_DN + MLA_DR)
    r1 = wq[:, :, MLA_DN:MLA_DN + half]
    r2 = wq[:, :, MLA_DN + half:]
    zq = jnp.zeros((Q_LORA, MLA_HEADS, LANES - MLA_DR), F32)
    w_uq_a = jnp.concatenate([wq[:, :, :MLA_DN], r1, r2, zq], axis=2).reshape(Q_LORA, MLA_HEADS * QK_PAD)
    w_uq_b = jnp.concatenate([r2, r1, zq], axis=2).reshape(Q_LORA, MLA_HEADS * LANES)
    out_a = MLA_HEADS * MLA_DV
    return (w_in_p, w_uq_a.astype(BF16), w_uq_b.astype(BF16), w_ukv.astype(BF16),
            w_out[:out_a].astype(BF16), w_out[out_a:].astype(BF16))


def _trunk(x, mods, layers, bias, g_final, *, batch, seq):
    tm = min(ROW_TILE, seq)
    cos_t, sin_t = _rope_tables(seq)
    for l, p in enumerate(layers):
        q, k, v, gq, gk, gv = _premix(x, mods[l], p["g_norm_mix"], cos_t, sin_t, p["w_in_p"], p["g_q_lat"],
                                      p["w_uq_a"], p["w_uq_b"], p["g_kv_lat"], p["w_ukv"], seq=seq, tm=tm)
        tq_sub = min(MLA_TQ_SUB, seq)
        tq = min(MLA_TQ, seq)
        oa = _mla(q, k, v, batch=batch, seq=seq, tq=tq, tk=min(MLA_TK, seq), nsub=tq // tq_sub)
        ob = _window(gq, gk, gv, bias, p["sink_col"], seq=seq, rows=min(ROW_TILE, seq))
        res = _postmix(oa, ob, x, mods[l], p["g_out_a"], p["g_out_b"], p["w_out_a"], p["w_out_b"],
                       p["g_norm_ffn"], p.get("w_router"), seq=seq, tm=tm)
        if "w_router" in p:
            assert l == len(layers) - 1
            x1, h2, topw, topi = res
            y2 = _moe_ffn(h2, topi[:, :2], p["w_gate"], p["w_up"], p["w_down"], tm=ROW_TILE, tf=FF_TILE)
            x = _combine(x1, y2, topw, mods[l], g_final, seq=seq, tm=tm)
        else:
            x1, h2 = res
            x = _ffn(x1, h2, mods[l], p["w_gate"], p["w_up"], p["w_down"], seq=seq, tm=tm, tf=FF_TILE)
    return x


def kernel(x_prompt, x_sample, c_prompt, c_sample, rel_bias, w_ada, b_ada, g_norm_mix, g_norm_ffn, w_in, g_q_lat, w_uq, g_kv_lat, w_ukv, sink, g_out_a, g_out_b, w_out, w_gate_d, w_up_d, w_down_d, w_router, w_gate_e, w_up_e, w_down_e, g_final):
    bp, sp, _ = x_prompt.shape
    bs, ss, _ = x_sample.shape
    rows = -(-(bp + bs) // 8) * 8
    c_all = jnp.concatenate([c_prompt, c_sample, jnp.zeros((rows - bp - bs, D_MODEL), F32)], axis=0)
    mod = _ada_mod(c_all, w_ada, b_ada)
    mod_p = [mod[l, :bp].reshape(bp, 6, D_MODEL) for l in range(DEPTH)]
    mod_s = [mod[l, bp:bp + bs].reshape(bs, 6, D_MODEL) for l in range(DEPTH)]
    bias = _window_bias(rel_bias)
    layers = []
    for l in range(DEPTH):
        w_in_p, w_uq_a, w_uq_b, w_ukv_b, w_out_a, w_out_b = _prep_layer(w_in[l], w_uq[l], w_ukv[l], w_out[l])
        out_a = MLA_HEADS * MLA_DV
        p = dict(
            g_norm_mix=g_norm_mix[l][None], g_norm_ffn=g_norm_ffn[l][None],
            w_in_p=w_in_p, g_q_lat=g_q_lat[l][None], w_uq_a=w_uq_a, w_uq_b=w_uq_b,
            g_kv_lat=g_kv_lat[l][None], w_ukv=w_ukv_b,
            sink_col=jnp.repeat(sink[l].astype(F32), WINDOW).reshape(GQA_KV_HEADS, GQA_GROUP * WINDOW, 1),
            g_out_a=g_out_a[l][None], g_out_b=g_out_b[l][None], w_out_a=w_out_a, w_out_b=w_out_b,
        )
        i = l // 2
        if l % 2 == 0:
            p.update(w_gate=w_gate_d[i].astype(BF16), w_up=w_up_d[i].astype(BF16), w_down=w_down_d[i].astype(BF16))
        else:
            wr = jnp.concatenate([w_router[i], jnp.zeros((D_MODEL, LANES - N_EXPERTS), F32)], axis=1)
            wr_hi = wr.astype(BF16)
            wr = jnp.stack([wr_hi, (wr - wr_hi.astype(F32)).astype(BF16)])
            p.update(w_gate=w_gate_e[i].astype(BF16), w_up=w_up_e[i].astype(BF16),
                     w_down=w_down_e[i].astype(BF16), w_router=wr)
        layers.append(p)
    gfin = g_final[None]
    y_s = _trunk(x_sample.reshape(bs * ss, D_MODEL), mod_s, layers, bias, gfin, batch=bs, seq=ss)
    y_p = _trunk(x_prompt.reshape(bp * sp, D_MODEL), mod_p, layers, bias, gfin, batch=bp, seq=sp)
    return (y_p.reshape(bp, sp, D_MODEL), y_s.reshape(bs, ss, D_MODEL))
```

```python
import functools
import math

import jax
import jax.numpy as jnp
from jax import lax
from jax.experimental import pallas as pl
from jax.experimental.pallas import tpu as pltpu

F32 = jnp.float32
BF16 = jnp.bfloat16

D_MODEL = 2048
DEPTH = 2
EPS = 1e-6
MLA_HEADS = 8
MLA_DN = 128
MLA_DR = 64
MLA_DV = 128
Q_LORA = 512
KV_LORA = 512
ROPE_THETA = 10000.0
MLA_SCALE = 1.0 / math.sqrt(MLA_DN + MLA_DR)
MLA_QSCALE = MLA_SCALE * math.log2(math.e)
GQA_HEADS = 8
GQA_KV_HEADS = 2
GQA_GROUP = GQA_HEADS // GQA_KV_HEADS
GQA_DH = 128
WINDOW = 128
GQA_SCALE = 1.0 / math.sqrt(GQA_DH)
NUM_BUCKETS = 32
MAX_DISTANCE = 128
N_EXPERTS = 8
NEG_BIG = -1e30

LANES = 128
SUBLANES = 8
V7X_VMEM_LIMIT = 56 * 1024 * 1024

QK_PAD = 2 * LANES
VT_ROWS = MLA_DV + 16
MLA_TQ = 2048
MLA_TQ_SUB = 256
MLA_TK = 512
ROW_TILE = 512
FF_TILE = 512
Z_CQ = 0
Z_CKV = Z_CQ + Q_LORA
Z_KRA = Z_CKV + KV_LORA
Z_KRB = Z_KRA + LANES
Z_GQ = Z_KRB + LANES
Z_GK = Z_GQ + GQA_HEADS * GQA_DH
Z_GV = Z_GK + GQA_KV_HEADS * GQA_DH
Z_END = Z_GV + GQA_KV_HEADS * GQA_DH


def _params(*sem):
    return pltpu.CompilerParams(dimension_semantics=sem, vmem_limit_bytes=V7X_VMEM_LIMIT)


def _rms(x, g):
    return x * lax.rsqrt(jnp.mean(x * x, axis=-1, keepdims=True) + EPS) * g


def _silu(a):
    return a * (1.0 / (1.0 + jnp.exp(-a)))


def _const_spec(shape):
    nd = len(shape)
    return pl.BlockSpec(shape, lambda *_: (0,) * nd)


def _ada_kernel(c_ref, w_ref, b_ref, o_ref):
    cs = _silu(c_ref[...])
    o_ref[...] = jnp.dot(cs, w_ref[...], preferred_element_type=F32,
                         precision=lax.Precision.HIGHEST) + b_ref[...]


def _ada_mod(c_all, w_ada, b_ada):
    rows = c_all.shape[0]
    n = w_ada.shape[-1]
    tn = 1024
    return pl.pallas_call(
        _ada_kernel,
        grid=(DEPTH, n // tn),
        in_specs=[
            pl.BlockSpec((rows, D_MODEL), lambda l, j: (0, 0)),
            pl.BlockSpec((None, D_MODEL, tn), lambda l, j: (l, 0, j)),
            pl.BlockSpec((None, 1, tn), lambda l, j: (l, 0, j)),
        ],
        out_specs=pl.BlockSpec((None, rows, tn), lambda l, j: (l, 0, j)),
        out_shape=jax.ShapeDtypeStruct((DEPTH, rows, n), F32),
        compiler_params=_params("parallel", "parallel"),
        name="ada_mod",
    )(c_all, w_ada, b_ada.reshape(DEPTH, 1, n))


def _premix_kernel(x_ref, mod_ref, gn_ref, cos_ref, sin_ref, win_ref, gq_ref, wuqa_ref, wuqb_ref,
                   gkv_ref, wukv_ref, q_ref, k_ref, v_ref, wq_ref, wk_ref, wv_ref):
    mod = mod_ref[...]
    h = _rms(x_ref[...], gn_ref[...]) * (1.0 + mod[1:2, :]) + mod[0:1, :]
    z = jnp.dot(h.astype(BF16), win_ref[...], preferred_element_type=F32)
    cqn = _rms(z[:, Z_CQ:Z_CKV], gq_ref[...]).astype(BF16)
    ckvn = _rms(z[:, Z_CKV:Z_KRA], gkv_ref[...]).astype(BF16)
    cos_t = cos_ref[...]
    sin_t = sin_ref[...]
    kr = (z[:, Z_KRA:Z_KRB] * cos_t + z[:, Z_KRB:Z_GQ] * sin_t).astype(BF16)
    wq_ref[...] = (z[:, Z_GQ:Z_GK] * GQA_SCALE).astype(BF16)
    wk_ref[...] = z[:, Z_GK:Z_GV].astype(BF16)
    wv_ref[...] = z[:, Z_GV:Z_END].astype(BF16)
    qa = jnp.dot(cqn, wuqa_ref[...], preferred_element_type=F32)
    qb = jnp.dot(cqn, wuqb_ref[...], preferred_element_type=F32)
    kv = jnp.dot(ckvn, wukv_ref[...], preferred_element_type=F32)
    sub = lax.broadcasted_iota(jnp.int32, (VT_ROWS - MLA_DV, kr.shape[0]), 0)
    ones_row = jnp.where(sub == 0, 1.0, 0.0).astype(BF16)
    for hh in range(MLA_HEADS):
        a0 = hh * QK_PAD
        q_ref[hh, :, 0:LANES] = (qa[:, a0:a0 + LANES] * MLA_QSCALE).astype(BF16)
        q_rope = qa[:, a0 + LANES:a0 + QK_PAD] * cos_t + qb[:, hh * LANES:(hh + 1) * LANES] * sin_t
        q_ref[hh, :, LANES:QK_PAD] = (q_rope * MLA_QSCALE).astype(BF16)
        k_ref[hh, :, 0:LANES] = kv[:, a0:a0 + LANES].astype(BF16)
        k_ref[hh, :, LANES:QK_PAD] = kr
        v_ref[hh, 0:MLA_DV, :] = kv[:, a0 + LANES:a0 + QK_PAD].T.astype(BF16)
        v_ref[hh, MLA_DV:VT_ROWS, :] = ones_row


def _premix(x, mod, gn, cos_t, sin_t, w_in_p, g_q, w_uq_a, w_uq_b, g_kv, w_ukv, *, seq, tm):
    t = x.shape[0]
    nt = seq // tm
    row = lambda i: (i, 0)
    hrow = lambda i: (0, i, 0)
    return pl.pallas_call(
        _premix_kernel,
        grid=(t // tm,),
        in_specs=[
            pl.BlockSpec((tm, D_MODEL), row),
            pl.BlockSpec((None, 6, D_MODEL), lambda i: (i // nt, 0, 0)),
            _const_spec((1, D_MODEL)),
            pl.BlockSpec((tm, LANES), lambda i: (i % nt, 0)),
            pl.BlockSpec((tm, LANES), lambda i: (i % nt, 0)),
            _const_spec(w_in_p.shape),
            _const_spec((1, Q_LORA)),
            _const_spec(w_uq_a.shape),
            _const_spec(w_uq_b.shape),
            _const_spec((1, KV_LORA)),
            _const_spec(w_ukv.shape),
        ],
        out_specs=[
            pl.BlockSpec((MLA_HEADS, tm, QK_PAD), hrow),
            pl.BlockSpec((MLA_HEADS, tm, QK_PAD), hrow),
            pl.BlockSpec((MLA_HEADS, VT_ROWS, tm), lambda i: (0, 0, i)),
            pl.BlockSpec((tm, GQA_HEADS * GQA_DH), row),
            pl.BlockSpec((tm, GQA_KV_HEADS * GQA_DH), row),
            pl.BlockSpec((tm, GQA_KV_HEADS * GQA_DH), row),
        ],
        out_shape=[
            jax.ShapeDtypeStruct((MLA_HEADS, t, QK_PAD), BF16),
            jax.ShapeDtypeStruct((MLA_HEADS, t, QK_PAD), BF16),
            jax.ShapeDtypeStruct((MLA_HEADS, VT_ROWS, t), BF16),
            jax.ShapeDtypeStruct((t, GQA_HEADS * GQA_DH), BF16),
            jax.ShapeDtypeStruct((t, GQA_KV_HEADS * GQA_DH), BF16),
            jax.ShapeDtypeStruct((t, GQA_KV_HEADS * GQA_DH), BF16),
        ],
        compiler_params=_params("parallel"),
        name="premix",
    )(x, mod, gn, cos_t, sin_t, w_in_p, g_q, w_uq_a, w_uq_b, g_kv, w_ukv)


def _mla_kernel(q_ref, k_ref, v_ref, o_ref, qt_scr, s_scr, p_scr, m_scr, a_scr, acc_scr, *, tk, nk, nsub):
    tq = q_ref.shape[0] // nsub
    subs = range(nsub)
    for i in subs:
        qt_scr[i] = q_ref[i * tq:(i + 1) * tq, :].astype(F32).T.astype(BF16)

    def scores(j, buf, i):
        k = k_ref[pl.ds(pl.multiple_of(j * tk, tk), tk), :]
        s_scr[buf, i] = jnp.dot(k, qt_scr[i], preferred_element_type=F32)

    def softmax(buf, i):
        s = s_scr[buf, i]
        m_old = m_scr[i]
        m_new = jnp.maximum(m_old, jnp.max(s, axis=0, keepdims=True))
        m_scr[i] = m_new
        a_scr[buf, i] = jnp.exp2(m_old - m_new)
        p_scr[buf, i] = jnp.exp2(s - m_new).astype(BF16)

    def values(j, buf, i):
        v = v_ref[:, pl.ds(pl.multiple_of(j * tk, tk), tk)]
        pv = jnp.dot(v, p_scr[buf, i], preferred_element_type=F32)
        acc_scr[i] = a_scr[buf, i] * acc_scr[i] + pv

    m_scr[...] = jnp.full(m_scr.shape, -jnp.inf, F32)
    acc_scr[...] = jnp.zeros(acc_scr.shape, F32)
    for i in subs:
        scores(0, 0, i)
        scores(1, 1, i)
        softmax(0, i)

    def body(u, carry):
        t = 2 * u + 1
        for i in subs:
            scores(t + 1, 0, i)
            values(t - 1, 0, i)
            softmax(1, i)
        for i in subs:
            scores(t + 2, 1, i)
            values(t, 1, i)
            softmax(0, i)
        return carry

    lax.fori_loop(0, (nk - 2) // 2, body, 0)
    for i in subs:
        softmax(1, i)
        values(nk - 2, 0, i)
        values(nk - 1, 1, i)
    for i in subs:
        acc = acc_scr[i]
        o_ref[i * tq:(i + 1) * tq, :] = (acc[0:MLA_DV, :] * (1.0 / acc[MLA_DV:MLA_DV + 1, :])).T.astype(BF16)


def _mla(q, k, v, *, batch, seq, tq, tk, nsub):
    t = batch * seq
    nq = seq // tq
    nk = seq // tk
    assert nk >= 2 and nk % 2 == 0, (seq, tk)
    sub = tq // nsub
    return pl.pallas_call(
        functools.partial(_mla_kernel, tk=tk, nk=nk, nsub=nsub),
        grid=(batch, MLA_HEADS, nq),
        in_specs=[
            pl.BlockSpec((None, tq, QK_PAD), lambda b, h, i: (h, b * nq + i, 0)),
            pl.BlockSpec((None, seq, QK_PAD), lambda b, h, i: (h, b, 0)),
            pl.BlockSpec((None, VT_ROWS, seq), lambda b, h, i: (h, 0, b)),
        ],
        out_specs=pl.BlockSpec((tq, MLA_DV), lambda b, h, i: (b * nq + i, h)),
        out_shape=jax.ShapeDtypeStruct((t, MLA_HEADS * MLA_DV), BF16),
        scratch_shapes=[
            pltpu.VMEM((nsub, QK_PAD, sub), BF16),
            pltpu.VMEM((2, nsub, tk, sub), F32),
            pltpu.VMEM((2, nsub, tk, sub), BF16),
            pltpu.VMEM((nsub, 1, sub), F32),
            pltpu.VMEM((2, nsub, 1, sub), F32),
            pltpu.VMEM((nsub, VT_ROWS, sub), F32),
        ],
        compiler_params=_params("parallel", "parallel", "parallel"),
        name="mla_attn",
    )(q, k, v)


def _window_kernel(q_ref, kp_ref, kc_ref, kn_ref, vp_ref, vc_ref, vn_ref, bias_ref, sink_ref, o_ref,
                   kcat, vcat, *, rows, nt):
    i = pl.program_id(0)
    nsub = rows // WINDOW
    kcat[0:WINDOW, :] = kp_ref[...]
    kcat[WINDOW:WINDOW + rows, :] = kc_ref[...]
    kcat[WINDOW + rows:, :] = kn_ref[...]
    vcat[0:WINDOW, :] = vp_ref[...]
    vcat[WINDOW:WINDOW + rows, :] = vc_ref[...]
    vcat[WINDOW + rows:, :] = vn_ref[...]
    first_tile = (i % nt) == 0
    last_tile = (i % nt) == nt - 1
    col = lax.broadcasted_iota(jnp.int32, (1, 3 * WINDOW), 1)

    chains = [(n, kh) for n in range(nsub) for kh in range(GQA_KV_HEADS)]
    s_all = []
    for n, kh in chains:
        r0 = n * WINDOW
        qall = q_ref[r0:r0 + WINDOW, kh * GQA_GROUP * GQA_DH:(kh + 1) * GQA_GROUP * GQA_DH]
        qs = jnp.concatenate([qall[:, g * GQA_DH:(g + 1) * GQA_DH] for g in range(GQA_GROUP)], axis=0)
        kk = kcat[r0:r0 + 3 * WINDOW, kh * GQA_DH:(kh + 1) * GQA_DH]
        s = lax.dot_general(qs, kk, (((1,), (1,)), ((), ())), preferred_element_type=F32) + bias_ref[kh]
        if n == 0:
            s = s + jnp.where(jnp.logical_and(first_tile, col < WINDOW), NEG_BIG, 0.0)
        if n == nsub - 1:
            s = s + jnp.where(jnp.logical_and(last_tile, col >= 2 * WINDOW), NEG_BIG, 0.0)
        s_all.append(s)
    m_all = [jnp.maximum(jnp.max(s, axis=1, keepdims=True), sink_ref[kh]) for s, (n, kh) in zip(s_all, chains)]
    e_all = [jnp.exp(s - m) for s, m in zip(s_all, m_all)]
    d_all = [jnp.sum(e, axis=1, keepdims=True) + jnp.exp(sink_ref[kh] - m)
             for e, m, (n, kh) in zip(e_all, m_all, chains)]
    for e, d, (n, kh) in zip(e_all, d_all, chains):
        r0 = n * WINDOW
        vv = vcat[r0:r0 + 3 * WINDOW, kh * GQA_DH:(kh + 1) * GQA_DH]
        o = jnp.dot(e.astype(BF16), vv, preferred_element_type=F32) * (1.0 / d)
        for g in range(GQA_GROUP):
            hcol = (kh * GQA_GROUP + g) * GQA_DH
            o_ref[r0:r0 + WINDOW, hcol:hcol + GQA_DH] = o[g * WINDOW:(g + 1) * WINDOW].astype(BF16)


def _window(gq, gk, gv, bias, sink_col, *, seq, rows):
    t = gq.shape[0]
    nt = seq // rows
    rb = rows // WINDOW
    nblk = t // WINDOW
    kvw = GQA_KV_HEADS * GQA_DH
    prev = lambda i: (jnp.maximum(i * rb - 1, 0), 0)
    nxt = lambda i: (jnp.minimum((i + 1) * rb, nblk - 1), 0)
    cur = lambda i: (i, 0)
    return pl.pallas_call(
        functools.partial(_window_kernel, rows=rows, nt=nt),
        grid=(t // rows,),
        in_specs=[
            pl.BlockSpec((rows, GQA_HEADS * GQA_DH), cur),
            pl.BlockSpec((WINDOW, kvw), prev),
            pl.BlockSpec((rows, kvw), cur),
            pl.BlockSpec((WINDOW, kvw), nxt),
            pl.BlockSpec((WINDOW, kvw), prev),
            pl.BlockSpec((rows, kvw), cur),
            pl.BlockSpec((WINDOW, kvw), nxt),
            _const_spec(bias.shape),
            _const_spec(sink_col.shape),
        ],
        out_specs=pl.BlockSpec((rows, GQA_HEADS * GQA_DH), cur),
        out_shape=jax.ShapeDtypeStruct((t, GQA_HEADS * GQA_DH), BF16),
        scratch_shapes=[pltpu.VMEM((rows + 2 * WINDOW, kvw), BF16),
                        pltpu.VMEM((rows + 2 * WINDOW, kvw), BF16)],
        compiler_params=_params("parallel"),
        name="window_attn",
    )(gq, gk, gk, gk, gv, gv, gv, bias, sink_col)


def _router(h2, wr_ref, topw_ref, topi_ref):
    h_hi = h2.astype(BF16)
    h_lo = (h2 - h_hi.astype(F32)).astype(BF16)
    logits = (jnp.dot(h_hi, wr_ref[0], preferred_element_type=F32)
              + jnp.dot(h_lo, wr_ref[0], preferred_element_type=F32)
              + jnp.dot(h_hi, wr_ref[1], preferred_element_type=F32))
    lane = lax.broadcasted_iota(jnp.int32, logits.shape, 1)
    lg = jnp.where(lane < N_EXPERTS, logits, -jnp.inf)
    m1 = jnp.max(lg, axis=1, keepdims=True)
    i1 = jnp.min(jnp.where(lg == m1, lane, LANES), axis=1, keepdims=True)
    lg2 = jnp.where(lane == i1, -jnp.inf, lg)
    m2 = jnp.max(lg2, axis=1, keepdims=True)
    i2 = jnp.min(jnp.where(lg2 == m2, lane, LANES), axis=1, keepdims=True)
    e2 = jnp.exp(m2 - m1)
    w1 = 1.0 / (1.0 + e2)
    w2 = e2 * w1
    topw_ref[...] = jnp.where(lane == 0, w1, jnp.where(lane == 1, w2, 0.0))
    topi_ref[...] = jnp.where(lane == 0, i1, jnp.where(lane == 1, i2, 0))


def _postmix_kernel(oa_ref, ob_ref, x_ref, mod_ref, ga_ref, gb_ref, wa_ref, wb_ref, gf_ref, *rest, moe):
    if moe:
        wr_ref, x1_ref, h2_ref, topw_ref, topi_ref, hprev = rest

        @pl.when(pl.program_id(0) == 0)
        def _():
            hprev[...] = jnp.zeros_like(hprev)

        _router(hprev[...], wr_ref, topw_ref, topi_ref)
    else:
        x1_ref, h2_ref = rest
    mod = mod_ref[...]
    na = _rms(oa_ref[...].astype(F32), ga_ref[...]).astype(BF16)
    nb = _rms(ob_ref[...].astype(F32), gb_ref[...]).astype(BF16)
    mix = (jnp.dot(na, wa_ref[...], preferred_element_type=F32)
           + jnp.dot(nb, wb_ref[...], preferred_element_type=F32))
    x1 = x_ref[...] + mod[2:3, :] * mix
    x1_ref[...] = x1
    h2 = _rms(x1, gf_ref[...]) * (1.0 + mod[4:5, :]) + mod[3:4, :]
    h2_ref[...] = h2.astype(h2_ref.dtype)
    if moe:
        hprev[...] = h2


def _postmix(oa, ob, x, mod, ga, gb, wa, wb, gf, wr, *, seq, tm):
    t = x.shape[0]
    nt = seq // tm
    nb = t // tm
    moe = wr is not None
    row = (lambda i: (jnp.minimum(i, nb - 1), 0)) if moe else (lambda i: (i, 0))
    lag = lambda i: (jnp.maximum(i - 1, 0), 0)
    half = oa.shape[1]
    in_specs = [
        pl.BlockSpec((tm, half), row),
        pl.BlockSpec((tm, half), row),
        pl.BlockSpec((tm, D_MODEL), row),
        pl.BlockSpec((None, 6, D_MODEL), lambda i: (row(i)[0] // nt, 0, 0)),
        _const_spec((1, half)),
        _const_spec((1, half)),
        _const_spec(wa.shape),
        _const_spec(wb.shape),
        _const_spec((1, D_MODEL)),
    ]
    out_specs = [pl.BlockSpec((tm, D_MODEL), row), pl.BlockSpec((tm, D_MODEL), row)]
    out_shape = [jax.ShapeDtypeStruct((t, D_MODEL), F32), jax.ShapeDtypeStruct((t, D_MODEL), F32 if moe else BF16)]
    args = [oa, ob, x, mod, ga, gb, wa, wb, gf]
    scratch = []
    if moe:
        in_specs.append(_const_spec(wr.shape))
        out_specs += [pl.BlockSpec((tm, LANES), lag), pl.BlockSpec((tm, LANES), lag)]
        out_shape += [jax.ShapeDtypeStruct((t, LANES), F32), jax.ShapeDtypeStruct((t, LANES), jnp.int32)]
        args.append(wr)
        scratch.append(pltpu.VMEM((tm, D_MODEL), F32))
    return pl.pallas_call(
        functools.partial(_postmix_kernel, moe=moe),
        grid=(nb + 1 if moe else nb,),
        in_specs=in_specs,
        out_specs=out_specs,
        out_shape=out_shape,
        scratch_shapes=scratch,
        compiler_params=_params("arbitrary" if moe else "parallel"),
        name="postmix_moe" if moe else "postmix",
    )(*args)


def _ffn_kernel(x1_ref, h_ref, mod_ref, wg_ref, wu_ref, wd_ref, o_ref, acc_ref):
    f = pl.program_id(1)

    @pl.when(f == 0)
    def _():
        acc_ref[...] = jnp.zeros_like(acc_ref)

    h = h_ref[...]
    a = jnp.dot(h, wg_ref[...], preferred_element_type=F32)
    u = jnp.dot(h, wu_ref[...], preferred_element_type=F32)
    acc_ref[...] += jnp.dot((_silu(a) * u).astype(BF16), wd_ref[...], preferred_element_type=F32)

    @pl.when(f == pl.num_programs(1) - 1)
    def _():
        o_ref[...] = x1_ref[...] + mod_ref[...][5:6, :] * acc_ref[...]


def _ffn(x1, h2, mod, wg, wu, wd, *, seq, tm, tf):
    t = x1.shape[0]
    nt = seq // tm
    dff = wg.shape[1]
    row = lambda i, f: (i, 0)
    return pl.pallas_call(
        _ffn_kernel,
        grid=(t // tm, dff // tf),
        in_specs=[
            pl.BlockSpec((tm, D_MODEL), row),
            pl.BlockSpec((tm, D_MODEL), row),
            pl.BlockSpec((None, 6, D_MODEL), lambda i, f: (i // nt, 0, 0)),
            pl.BlockSpec((D_MODEL, tf), lambda i, f: (0, f)),
            pl.BlockSpec((D_MODEL, tf), lambda i, f: (0, f)),
            pl.BlockSpec((tf, D_MODEL), lambda i, f: (f, 0)),
        ],
        out_specs=pl.BlockSpec((tm, D_MODEL), row),
        out_shape=jax.ShapeDtypeStruct((t, D_MODEL), F32),
        scratch_shapes=[pltpu.VMEM((tm, D_MODEL), F32)],
        compiler_params=_params("parallel", "arbitrary"),
        name="ffn_dense",
    )(x1, h2, mod, wg, wu, wd)


def _route(topi, tm, rp):
    t = topi.shape[0]
    na = 2 * t
    nt = na // tm + N_EXPERTS
    flat_e = topi.reshape(na)
    onehot = (flat_e[:, None] == jnp.arange(N_EXPERTS, dtype=jnp.int32)[None, :]).astype(jnp.int32)
    csum = jnp.cumsum(onehot, axis=0)
    rank = jnp.sum((csum - onehot) * onehot, axis=1)
    padded = ((csum[-1] + tm - 1) // tm) * tm
    ends = jnp.cumsum(padded)
    pos = (ends - padded)[flat_e] + rank
    pair = jnp.full((nt * tm,), -1, jnp.int32).at[pos].set(jnp.arange(na, dtype=jnp.int32))
    pair = jnp.pad(pair.reshape(nt, tm), ((0, 0), (0, rp - tm)), constant_values=-1)
    spare = na + (jnp.arange(nt + 1, dtype=jnp.int32)[:, None] % 3) * rp + jnp.arange(rp, dtype=jnp.int32)[None, :]
    src = jnp.where(pair >= 0, pair // 2, 0)
    dst = jnp.where(pair >= 0, (pair % 2) * t + pair // 2, spare[1:])
    dst = jnp.concatenate([spare[:1], dst], axis=0)
    tile_start = jnp.arange(nt, dtype=jnp.int32) * tm
    tile_valid = (tile_start < ends[-1]).astype(jnp.int32)
    tile_expert = jnp.minimum(jnp.sum((tile_start[:, None] >= ends[None, :]).astype(jnp.int32), axis=1),
                              N_EXPERTS - 1)
    return src.reshape(nt, 1, rp), dst.reshape(nt + 1, 1, rp), tile_expert, tile_valid


def _moe_kernel(te_ref, tv_ref, src0_ref, src1_ref, src2_ref, pdst_ref, dst_ref, h_hbm, wg_ref, wu_ref, wd_ref,
                y_hbm, hbuf, hb16, acc, gsem, ssem, *, tm, ch):
    i = pl.program_id(0)
    f = pl.program_id(1)
    nt = pl.num_programs(0)
    nf = pl.num_programs(1)
    rp = nf * ch
    cur = i % 3
    oth = (i + 2) % 3
    valid = tv_ref[i] == 1
    last_valid = jnp.logical_and(valid, jnp.logical_or(i == nt - 1, tv_ref[jnp.minimum(i + 1, nt - 1)] == 0))

    def gather_row(idx_ref, r, b):
        pltpu.make_async_copy(h_hbm.at[pl.ds(idx_ref[0, r], 1), :], hbuf.at[b, pl.ds(r, 1), :],
                              gsem.at[b]).start()

    def scatter_row(idx_ref, r, b):
        pltpu.make_async_copy(acc.at[b, pl.ds(r, 1), :], y_hbm.at[pl.ds(idx_ref[0, r], 1), :],
                              ssem.at[b]).start()

    def wait_rows(buf, sem, b):
        pltpu.make_async_copy(buf.at[b], buf.at[b], sem.at[b]).wait()

    @pl.when(jnp.logical_and(i == 0, f == 0))
    def _():
        acc[...] = jnp.zeros_like(acc)

        def first(r, c):
            gather_row(src0_ref, r, 0)
            gather_row(src1_ref, r, 1)
            return c
        lax.fori_loop(0, rp, first, 0, unroll=8)

    @pl.when(jnp.logical_and(valid, f == 0))
    def _():
        wait_rows(hbuf, gsem, cur)
        hb16[...] = hbuf[cur, 0:tm, :].astype(BF16)

        @pl.when(i >= 2)
        def _():
            wait_rows(acc, ssem, cur)

        acc[cur, 0:tm, :] = jnp.zeros((tm, D_MODEL), F32)

    @pl.when(valid)
    def _():
        for c in range(ch):
            r = f * ch + c
            gather_row(src2_ref, r, oth)
            scatter_row(pdst_ref, r, oth)
        h = hb16[...]
        a = jnp.dot(h, wg_ref[...], preferred_element_type=F32)
        u = jnp.dot(h, wu_ref[...], preferred_element_type=F32)
        acc[cur, 0:tm, :] += jnp.dot((_silu(a) * u).astype(BF16), wd_ref[...], preferred_element_type=F32)

    @pl.when(jnp.logical_and(last_valid, f == nf - 1))
    def _():
        def own(r, c):
            scatter_row(dst_ref, r, cur)
            return c
        lax.fori_loop(0, rp, own, 0, unroll=8)
        @pl.when(i >= 1)
        def _():
            wait_rows(acc, ssem, (i + 1) % 3)

        wait_rows(acc, ssem, oth)
        wait_rows(acc, ssem, cur)
        wait_rows(hbuf, gsem, (i + 1) % 3)
        wait_rows(hbuf, gsem, oth)
        hbuf[0] = jnp.zeros((rp, D_MODEL), F32)
        for b in range(3):
            spare = pltpu.make_async_copy(hbuf.at[0], y_hbm.at[pl.ds(y_hbm.shape[0] - (b + 1) * rp, rp), :],
                                          gsem.at[0])
            spare.start()
            spare.wait()


def _moe_ffn(h2, topi, wg, wu, wd, *, tm, tf):
    t = h2.shape[0]
    dff = wg.shape[2]
    nf = dff // tf
    ch = -(-tm // (nf * SUBLANES)) * SUBLANES
    rp = nf * ch
    src, dst, tile_expert, tile_valid = _route(topi, tm, rp)
    nt = src.shape[0]
    idx_spec = lambda fn: pl.BlockSpec((None, 1, rp), fn, memory_space=pltpu.SMEM)
    fsel = lambda i, f, tv: jnp.where(tv[i] == 1, f, nf - 1)
    return pl.pallas_call(
        functools.partial(_moe_kernel, tm=tm, ch=ch),
        grid_spec=pltpu.PrefetchScalarGridSpec(
            num_scalar_prefetch=2,
            grid=(nt, nf),
            in_specs=[
                idx_spec(lambda i, f, te, tv: (i, 0, 0)),
                idx_spec(lambda i, f, te, tv: (jnp.minimum(i + 1, nt - 1), 0, 0)),
                idx_spec(lambda i, f, te, tv: (jnp.minimum(i + 2, nt - 1), 0, 0)),
                idx_spec(lambda i, f, te, tv: (i, 0, 0)),
                idx_spec(lambda i, f, te, tv: (i + 1, 0, 0)),
                pl.BlockSpec(memory_space=pl.ANY),
                pl.BlockSpec((None, D_MODEL, tf), lambda i, f, te, tv: (te[i], 0, fsel(i, f, tv))),
                pl.BlockSpec((None, D_MODEL, tf), lambda i, f, te, tv: (te[i], 0, fsel(i, f, tv))),
                pl.BlockSpec((None, tf, D_MODEL), lambda i, f, te, tv: (te[i], fsel(i, f, tv), 0)),
            ],
            out_specs=pl.BlockSpec(memory_space=pl.ANY),
            scratch_shapes=[
                pltpu.VMEM((3, rp, D_MODEL), F32),
                pltpu.VMEM((tm, D_MODEL), BF16),
                pltpu.VMEM((3, rp, D_MODEL), F32),
                pltpu.SemaphoreType.DMA((3,)),
                pltpu.SemaphoreType.DMA((3,)),
            ],
        ),
        out_shape=jax.ShapeDtypeStruct((2 * t + 3 * rp, D_MODEL), F32),
        compiler_params=_params("arbitrary", "arbitrary"),
        name="ffn_moe",
    )(tile_expert, tile_valid, src, src, src, dst, dst, h2, wg, wu, wd)


def _combine_kernel(x1_ref, ya_ref, yb_ref, topw_ref, mod_ref, gfin_ref, o_ref):
    w = topw_ref[...]
    f = w[:, 0:1] * ya_ref[...] + w[:, 1:2] * yb_ref[...]
    o_ref[...] = _rms(x1_ref[...] + mod_ref[...][5:6, :] * f, gfin_ref[...])


def _combine(x1, y2, topw, mod, gfin, *, seq, tm):
    t = x1.shape[0]
    nt = seq // tm
    nb = t // tm
    row = lambda i: (i, 0)
    return pl.pallas_call(
        _combine_kernel,
        grid=(nb,),
        in_specs=[
            pl.BlockSpec((tm, D_MODEL), row),
            pl.BlockSpec((tm, D_MODEL), row),
            pl.BlockSpec((tm, D_MODEL), lambda i: (nb + i, 0)),
            pl.BlockSpec((tm, LANES), row),
            pl.BlockSpec((None, 6, D_MODEL), lambda i: (i // nt, 0, 0)),
            _const_spec((1, D_MODEL)),
        ],
        out_specs=pl.BlockSpec((tm, D_MODEL), row),
        out_shape=jax.ShapeDtypeStruct((t, D_MODEL), F32),
        compiler_params=_params("parallel"),
        name="moe_combine",
    )(x1, y2, y2, topw, mod, gfin)


def _rope_tables(seq):
    pos = jnp.arange(seq, dtype=F32)
    inv = 1.0 / (ROPE_THETA ** (jnp.arange(0, MLA_DR, 2, dtype=F32) / MLA_DR))
    ang = pos[:, None] * inv[None, :]
    cos, sin = jnp.cos(ang), jnp.sin(ang)
    zero = jnp.zeros((seq, LANES - MLA_DR), F32)
    return (jnp.concatenate([cos, cos, zero], axis=1), jnp.concatenate([-sin, sin, zero], axis=1))


def _t5_bucket(rel):
    nb = NUM_BUCKETS // 2
    ret = (rel > 0).astype(jnp.int32) * nb
    n = jnp.abs(rel)
    max_exact = nb // 2
    nf = jnp.maximum(n, 1).astype(F32)
    large = max_exact + (jnp.log(nf / max_exact) / math.log(MAX_DISTANCE / max_exact)
                         * (nb - max_exact)).astype(jnp.int32)
    large = jnp.minimum(large, nb - 1)
    return ret + jnp.where(n < max_exact, n, large)


def _window_bias(rel_bias):
    qpos = jnp.arange(WINDOW, dtype=jnp.int32)
    jpos = jnp.arange(3 * WINDOW, dtype=jnp.int32)
    rel = jpos[None, :] - WINDOW - qpos[:, None]
    bias = rel_bias[_t5_bucket(rel)].astype(F32).transpose(2, 0, 1)
    bias = jnp.where((jnp.abs(rel) <= WINDOW)[None], bias, NEG_BIG)
    return bias.reshape(GQA_KV_HEADS, GQA_GROUP * WINDOW, 3 * WINDOW)


def _prep_layer(w_in, w_uq, w_ukv, w_out):
    half = MLA_DR // 2
    zpad = jnp.zeros((D_MODEL, LANES - MLA_DR), F32)
    kr0 = Q_LORA + KV_LORA
    k1 = w_in[:, kr0:kr0 + half]
    k2 = w_in[:, kr0 + half:kr0 + MLA_DR]
    w_in_p = jnp.concatenate(
        [w_in[:, :kr0], k1, k2, zpad, k2, k1, zpad, w_in[:, kr0 + MLA_DR:]], axis=1).astype(BF16)
    wq = w_uq.reshape(Q_LORA, MLA_HEADS, MLA_DN + MLA_DR)
    r1 = wq[:, :, MLA_DN:MLA_DN + half]
    r2 = wq[:, :, MLA_DN + half:]
    zq = jnp.zeros((Q_LORA, MLA_HEADS, LANES - MLA_DR), F32)
    w_uq_a = jnp.concatenate([wq[:, :, :MLA_DN], r1, r2, zq], axis=2).reshape(Q_LORA, MLA_HEADS * QK_PAD)
    w_uq_b = jnp.concatenate([r2, r1, zq], axis=2).reshape(Q_LORA, MLA_HEADS * LANES)
    out_a = MLA_HEADS * MLA_DV
    return (w_in_p, w_uq_a.astype(BF16), w_uq_b.astype(BF16), w_ukv.astype(BF16),
            w_out[:out_a].astype(BF16), w_out[out_a:].astype(BF16))


def _trunk(x, mods, layers, bias, g_final, *, batch, seq):
    tm = min(ROW_TILE, seq)
    cos_t, sin_t = _rope_tables(seq)
    for l, p in enumerate(layers):
        q, k, v, gq, gk, gv = _premix(x, mods[l], p["g_norm_mix"], cos_t, sin_t, p["w_in_p"], p["g_q_lat"],
                                      p["w_uq_a"], p["w_uq_b"], p["g_kv_lat"], p["w_ukv"], seq=seq, tm=tm)
        tq_sub = min(MLA_TQ_SUB, seq)
        tq = min(MLA_TQ, seq)
        oa = _mla(q, k, v, batch=batch, seq=seq, tq=tq, tk=min(MLA_TK, seq), nsub=tq // tq_sub)
        ob = _window(gq, gk, gv, bias, p["sink_col"], seq=seq, rows=min(ROW_TILE, seq))
        res = _postmix(oa, ob, x, mods[l], p["g_out_a"], p["g_out_b"], p["w_out_a"], p["w_out_b"],
                       p["g_norm_ffn"], p.get("w_router"), seq=seq, tm=tm)
        if "w_router" in p:
            assert l == len(layers) - 1
            x1, h2, topw, topi = res
            y2 = _moe_ffn(h2, topi[:, :2], p["w_gate"], p["w_up"], p["w_down"], tm=ROW_TILE, tf=FF_TILE)
            x = _combine(x1, y2, topw, mods[l], g_final, seq=seq, tm=tm)
        else:
            x1, h2 = res
            x = _ffn(x1, h2, mods[l], p["w_gate"], p["w_up"], p["w_down"], seq=seq, tm=tm, tf=FF_TILE)
    return x


def kernel(x_prompt, x_sample, c_prompt, c_sample, rel_bias, w_ada, b_ada, g_norm_mix, g_norm_ffn, w_in, g_q_lat, w_uq, g_kv_lat, w_ukv, sink, g_out_a, g_out_b, w_out, w_gate_d, w_up_d, w_down_d, w_router, w_gate_e, w_up_e, w_down_e, g_final):
    bp, sp, _ = x_prompt.shape
    bs, ss, _ = x_sample.shape
    rows = -(-(bp + bs) // 8) * 8
    c_all = jnp.concatenate([c_prompt, c_sample, jnp.zeros((rows - bp - bs, D_MODEL), F32)], axis=0)
    mod = _ada_mod(c_all, w_ada, b_ada)
    mod_p = [mod[l, :bp].reshape(bp, 6, D_MODEL) for l in range(DEPTH)]
    mod_s = [mod[l, bp:bp + bs].reshape(bs, 6, D_MODEL) for l in range(DEPTH)]
    bias = _window_bias(rel_bias)
    layers = []
    for l in range(DEPTH):
        w_in_p, w_uq_a, w_uq_b, w_ukv_b, w_out_a, w_out_b = _prep_layer(w_in[l], w_uq[l], w_ukv[l], w_out[l])
        out_a = MLA_HEADS * MLA_DV
        p = dict(
            g_norm_mix=g_norm_mix[l][None], g_norm_ffn=g_norm_ffn[l][None],
            w_in_p=w_in_p, g_q_lat=g_q_lat[l][None], w_uq_a=w_uq_a, w_uq_b=w_uq_b,
            g_kv_lat=g_kv_lat[l][None], w_ukv=w_ukv_b,
            sink_col=jnp.repeat(sink[l].astype(F32), WINDOW).reshape(GQA_KV_HEADS, GQA_GROUP * WINDOW, 1),
            g_out_a=g_out_a[l][None], g_out_b=g_out_b[l][None], w_out_a=w_out_a, w_out_b=w_out_b,
        )
        i = l // 2
        if l % 2 == 0:
            p.update(w_gate=w_gate_d[i].astype(BF16), w_up=w_up_d[i].astype(BF16), w_down=w_down_d[i].astype(BF16))
        else:
            wr = jnp.concatenate([w_router[i], jnp.zeros((D_MODEL, LANES - N_EXPERTS), F32)], axis=1)
            wr_hi = wr.astype(BF16)
            wr = jnp.stack([wr_hi, (wr - wr_hi.astype(F32)).astype(BF16)])
            p.update(w_gate=w_gate_e[i].astype(BF16), w_up=w_up_e[i].astype(BF16),
                     w_down=w_down_e[i].astype(BF16), w_router=wr)
        layers.append(p)
    gfin = g_final[None]
    y_s = _trunk(x_sample.reshape(bs * ss, D_MODEL), mod_s, layers, bias, gfin, batch=bs, seq=ss)
    y_p = _trunk(x_prompt.reshape(bp * sp, D_MODEL), mod_p, layers, bias, gfin, batch=bp, seq=sp)
    return (y_p.reshape(bp, sp, D_MODEL), y_s.reshape(bs, ss, D_MODEL))
```

```python
import functools
import math

import jax
import jax.numpy as jnp
from jax import lax
from jax.experimental import pallas as pl
from jax.experimental.pallas import tpu as pltpu

F32 = jnp.float32
BF16 = jnp.bfloat16

D_MODEL = 2048
DEPTH = 2
EPS = 1e-6
MLA_HEADS = 8
MLA_DN = 128
MLA_DR = 64
MLA_DV = 128
Q_LORA = 512
KV_LORA = 512
ROPE_THETA = 10000.0
MLA_SCALE = 1.0 / math.sqrt(MLA_DN + MLA_DR)
MLA_QSCALE = MLA_SCALE * math.log2(math.e)
GQA_HEADS = 8
GQA_KV_HEADS = 2
GQA_GROUP = GQA_HEADS // GQA_KV_HEADS
GQA_DH = 128
WINDOW = 128
GQA_SCALE = 1.0 / math.sqrt(GQA_DH)
NUM_BUCKETS = 32
MAX_DISTANCE = 128
N_EXPERTS = 8
NEG_BIG = -1e30

LANES = 128
SUBLANES = 8
V7X_VMEM_LIMIT = 56 * 1024 * 1024

QK_PAD = 2 * LANES
VT_ROWS = MLA_DV + 16
MLA_TQ = 2048
MLA_TQ_SUB = 256
MLA_TK = 512
ROW_TILE = 512
FF_TILE = 512
Z_CQ = 0
Z_CKV = Z_CQ + Q_LORA
Z_KRA = Z_CKV + KV_LORA
Z_KRB = Z_KRA + LANES
Z_GQ = Z_KRB + LANES
Z_GK = Z_GQ + GQA_HEADS * GQA_DH
Z_GV = Z_GK + GQA_KV_HEADS * GQA_DH
Z_END = Z_GV + GQA_KV_HEADS * GQA_DH


def _params(*sem):
    return pltpu.CompilerParams(dimension_semantics=sem, vmem_limit_bytes=V7X_VMEM_LIMIT)


def _rms(x, g):
    return x * lax.rsqrt(jnp.mean(x * x, axis=-1, keepdims=True) + EPS) * g


def _silu(a):
    return a * (1.0 / (1.0 + jnp.exp(-a)))


def _const_spec(shape):
    nd = len(shape)
    return pl.BlockSpec(shape, lambda *_: (0,) * nd)


def _ada_kernel(c_ref, w_ref, b_ref, o_ref):
    cs = _silu(c_ref[...])
    o_ref[...] = jnp.dot(cs, w_ref[...], preferred_element_type=F32,
                         precision=lax.Precision.HIGHEST) + b_ref[...]


def _ada_mod(c_all, w_ada, b_ada):
    rows = c_all.shape[0]
    n = w_ada.shape[-1]
    tn = 1024
    return pl.pallas_call(
        _ada_kernel,
        grid=(DEPTH, n // tn),
        in_specs=[
            pl.BlockSpec((rows, D_MODEL), lambda l, j: (0, 0)),
            pl.BlockSpec((None, D_MODEL, tn), lambda l, j: (l, 0, j)),
            pl.BlockSpec((None, 1, tn), lambda l, j: (l, 0, j)),
        ],
        out_specs=pl.BlockSpec((None, rows, tn), lambda l, j: (l, 0, j)),
        out_shape=jax.ShapeDtypeStruct((DEPTH, rows, n), F32),
        compiler_params=_params("parallel", "parallel"),
        name="ada_mod",
    )(c_all, w_ada, b_ada.reshape(DEPTH, 1, n))


def _premix_kernel(x_ref, mod_ref, gn_ref, cos_ref, sin_ref, win_ref, gq_ref, wuqa_ref, wuqb_ref,
                   gkv_ref, wukv_ref, q_ref, k_ref, v_ref, wq_ref, wk_ref, wv_ref):
    mod = mod_ref[...]
    h = _rms(x_ref[...], gn_ref[...]) * (1.0 + mod[1:2, :]) + mod[0:1, :]
    z = jnp.dot(h.astype(BF16), win_ref[...], preferred_element_type=F32)
    cqn = _rms(z[:, Z_CQ:Z_CKV], gq_ref[...]).astype(BF16)
    ckvn = _rms(z[:, Z_CKV:Z_KRA], gkv_ref[...]).astype(BF16)
    cos_t = cos_ref[...]
    sin_t = sin_ref[...]
    kr = (z[:, Z_KRA:Z_KRB] * cos_t + z[:, Z_KRB:Z_GQ] * sin_t).astype(BF16)
    wq_ref[...] = (z[:, Z_GQ:Z_GK] * GQA_SCALE).astype(BF16)
    wk_ref[...] = z[:, Z_GK:Z_GV].astype(BF16)
    wv_ref[...] = z[:, Z_GV:Z_END].astype(BF16)
    qa = jnp.dot(cqn, wuqa_ref[...], preferred_element_type=F32)
    qb = jnp.dot(cqn, wuqb_ref[...], preferred_element_type=F32)
    kv = jnp.dot(ckvn, wukv_ref[...], preferred_element_type=F32)
    sub = lax.broadcasted_iota(jnp.int32, (VT_ROWS - MLA_DV, kr.shape[0]), 0)
    ones_row = jnp.where(sub == 0, 1.0, 0.0).astype(BF16)
    for hh in range(MLA_HEADS):
        a0 = hh * QK_PAD
        q_ref[hh, :, 0:LANES] = (qa[:, a0:a0 + LANES] * MLA_QSCALE).astype(BF16)
        q_rope = qa[:, a0 + LANES:a0 + QK_PAD] * cos_t + qb[:, hh * LANES:(hh + 1) * LANES] * sin_t
        q_ref[hh, :, LANES:QK_PAD] = (q_rope * MLA_QSCALE).astype(BF16)
        k_ref[hh, :, 0:LANES] = kv[:, a0:a0 + LANES].astype(BF16)
        k_ref[hh, :, LANES:QK_PAD] = kr
        v_ref[hh, 0:MLA_DV, :] = kv[:, a0 + LANES:a0 + QK_PAD].T.astype(BF16)
        v_ref[hh, MLA_DV:VT_ROWS, :] = ones_row


def _premix(x, mod, gn, cos_t, sin_t, w_in_p, g_q, w_uq_a, w_uq_b, g_kv, w_ukv, *, seq, tm):
    t = x.shape[0]
    nt = seq // tm
    row = lambda i: (i, 0)
    hrow = lambda i: (0, i, 0)
    return pl.pallas_call(
        _premix_kernel,
        grid=(t // tm,),
        in_specs=[
            pl.BlockSpec((tm, D_MODEL), row),
            pl.BlockSpec((None, 6, D_MODEL), lambda i: (i // nt, 0, 0)),
            _const_spec((1, D_MODEL)),
            pl.BlockSpec((tm, LANES), lambda i: (i % nt, 0)),
            pl.BlockSpec((tm, LANES), lambda i: (i % nt, 0)),
            _const_spec(w_in_p.shape),
            _const_spec((1, Q_LORA)),
            _const_spec(w_uq_a.shape),
            _const_spec(w_uq_b.shape),
            _const_spec((1, KV_LORA)),
            _const_spec(w_ukv.shape),
        ],
        out_specs=[
            pl.BlockSpec((MLA_HEADS, tm, QK_PAD), hrow),
            pl.BlockSpec((MLA_HEADS, tm, QK_PAD), hrow),
            pl.BlockSpec((MLA_HEADS, VT_ROWS, tm), lambda i: (0, 0, i)),
            pl.BlockSpec((tm, GQA_HEADS * GQA_DH), row),
            pl.BlockSpec((tm, GQA_KV_HEADS * GQA_DH), row),
            pl.BlockSpec((tm, GQA_KV_HEADS * GQA_DH), row),
        ],
        out_shape=[
            jax.ShapeDtypeStruct((MLA_HEADS, t, QK_PAD), BF16),
            jax.ShapeDtypeStruct((MLA_HEADS, t, QK_PAD), BF16),
            jax.ShapeDtypeStruct((MLA_HEADS, VT_ROWS, t), BF16),
            jax.ShapeDtypeStruct((t, GQA_HEADS * GQA_DH), BF16),
            jax.ShapeDtypeStruct((t, GQA_KV_HEADS * GQA_DH), BF16),
            jax.ShapeDtypeStruct((t, GQA_KV_HEADS * GQA_DH), BF16),
        ],
        compiler_params=_params("parallel"),
        name="premix",
    )(x, mod, gn, cos_t, sin_t, w_in_p, g_q, w_uq_a, w_uq_b, g_kv, w_ukv)


def _mla_kernel(q_ref, k_ref, v_ref, o_ref, qt_scr, s_scr, p_scr, m_scr, a_scr, acc_scr, pm_scr, *, tk, nk, nsub):
    tq = q_ref.shape[0] // nsub
    subs = range(nsub)
    for i in subs:
        qt_scr[i] = q_ref[i * tq:(i + 1) * tq, :].astype(F32).T.astype(BF16)

    def scores(j, buf, i):
        k = k_ref[pl.ds(pl.multiple_of(j * tk, tk), tk), :]
        s = jnp.dot(k, qt_scr[i], preferred_element_type=F32)
        s_scr[buf, i] = s
        pm = s[0:SUBLANES, :]
        for g in range(1, tk // SUBLANES):
            pm = jnp.maximum(pm, s[g * SUBLANES:(g + 1) * SUBLANES, :])
        pm_scr[buf, i] = pm

    def smax(buf, i):
        m_old = m_scr[i]
        m_new = jnp.maximum(m_old, jnp.max(pm_scr[buf, i], axis=0, keepdims=True))
        m_scr[i] = m_new
        a_scr[buf, i] = jnp.exp2(m_old - m_new)

    def sexp(buf, i):
        p_scr[buf, i] = jnp.exp2(s_scr[buf, i] - m_scr[i]).astype(BF16)

    def softmax(buf, i):
        smax(buf, i)
        sexp(buf, i)

    def values(j, buf, i):
        v = v_ref[:, pl.ds(pl.multiple_of(j * tk, tk), tk)]
        pv = jnp.dot(v, p_scr[buf, i], preferred_element_type=F32)
        acc_scr[i] = a_scr[buf, i] * acc_scr[i] + pv

    m_scr[...] = jnp.full(m_scr.shape, -jnp.inf, F32)
    acc_scr[...] = jnp.zeros(acc_scr.shape, F32)
    for i in subs:
        scores(0, 0, i)
        scores(1, 1, i)
        softmax(0, i)

    def body(u, carry):
        t = 2 * u + 1
        for i in subs:
            smax(1, i)
            scores(t + 1, 0, i)
            values(t - 1, 0, i)
            sexp(1, i)
        for i in subs:
            smax(0, i)
            scores(t + 2, 1, i)
            values(t, 1, i)
            sexp(0, i)
        return carry

    lax.fori_loop(0, (nk - 2) // 2, body, 0)
    for i in subs:
        softmax(1, i)
        values(nk - 2, 0, i)
        values(nk - 1, 1, i)
    for i in subs:
        acc = acc_scr[i]
        o_ref[i * tq:(i + 1) * tq, :] = (acc[0:MLA_DV, :] * (1.0 / acc[MLA_DV:MLA_DV + 1, :])).T.astype(BF16)


def _mla(q, k, v, *, batch, seq, tq, tk, nsub):
    t = batch * seq
    nq = seq // tq
    nk = seq // tk
    assert nk >= 2 and nk % 2 == 0, (seq, tk)
    sub = tq // nsub
    return pl.pallas_call(
        functools.partial(_mla_kernel, tk=tk, nk=nk, nsub=nsub),
        grid=(batch, MLA_HEADS, nq),
        in_specs=[
            pl.BlockSpec((None, tq, QK_PAD), lambda b, h, i: (h, b * nq + i, 0)),
            pl.BlockSpec((None, seq, QK_PAD), lambda b, h, i: (h, b, 0)),
            pl.BlockSpec((None, VT_ROWS, seq), lambda b, h, i: (h, 0, b)),
        ],
        out_specs=pl.BlockSpec((tq, MLA_DV), lambda b, h, i: (b * nq + i, h)),
        out_shape=jax.ShapeDtypeStruct((t, MLA_HEADS * MLA_DV), BF16),
        scratch_shapes=[
            pltpu.VMEM((nsub, QK_PAD, sub), BF16),
            pltpu.VMEM((2, nsub, tk, sub), F32),
            pltpu.VMEM((2, nsub, tk, sub), BF16),
            pltpu.VMEM((nsub, 1, sub), F32),
            pltpu.VMEM((2, nsub, 1, sub), F32),
            pltpu.VMEM((nsub, VT_ROWS, sub), F32),
            pltpu.VMEM((2, nsub, SUBLANES, sub), F32),
        ],
        compiler_params=_params("parallel", "parallel", "parallel"),
        name="mla_attn",
    )(q, k, v)


def _window_kernel(q_ref, kp_ref, kc_ref, kn_ref, vp_ref, vc_ref, vn_ref, bias_ref, sink_ref, o_ref,
                   kcat, vcat, *, rows, nt):
    i = pl.program_id(0)
    nsub = rows // WINDOW
    kcat[0:WINDOW, :] = kp_ref[...]
    kcat[WINDOW:WINDOW + rows, :] = kc_ref[...]
    kcat[WINDOW + rows:, :] = kn_ref[...]
    vcat[0:WINDOW, :] = vp_ref[...]
    vcat[WINDOW:WINDOW + rows, :] = vc_ref[...]
    vcat[WINDOW + rows:, :] = vn_ref[...]
    first_tile = (i % nt) == 0
    last_tile = (i % nt) == nt - 1
    col = lax.broadcasted_iota(jnp.int32, (1, 3 * WINDOW), 1)

    chains = [(n, kh) for n in range(nsub) for kh in range(GQA_KV_HEADS)]
    s_all = []
    for n, kh in chains:
        r0 = n * WINDOW
        qall = q_ref[r0:r0 + WINDOW, kh * GQA_GROUP * GQA_DH:(kh + 1) * GQA_GROUP * GQA_DH]
        qs = jnp.concatenate([qall[:, g * GQA_DH:(g + 1) * GQA_DH] for g in range(GQA_GROUP)], axis=0)
        kk = kcat[r0:r0 + 3 * WINDOW, kh * GQA_DH:(kh + 1) * GQA_DH]
        s = lax.dot_general(qs, kk, (((1,), (1,)), ((), ())), preferred_element_type=F32) + bias_ref[kh]
        if n == 0:
            s = s + jnp.where(jnp.logical_and(first_tile, col < WINDOW), NEG_BIG, 0.0)
        if n == nsub - 1:
            s = s + jnp.where(jnp.logical_and(last_tile, col >= 2 * WINDOW), NEG_BIG, 0.0)
        s_all.append(s)
    m_all = [jnp.maximum(jnp.max(s, axis=1, keepdims=True), sink_ref[kh]) for s, (n, kh) in zip(s_all, chains)]
    e_all = [jnp.exp(s - m) for s, m in zip(s_all, m_all)]
    d_all = [jnp.sum(e, axis=1, keepdims=True) + jnp.exp(sink_ref[kh] - m)
             for e, m, (n, kh) in zip(e_all, m_all, chains)]
    for e, d, (n, kh) in zip(e_all, d_all, chains):
        r0 = n * WINDOW
        vv = vcat[r0:r0 + 3 * WINDOW, kh * GQA_DH:(kh + 1) * GQA_DH]
        o = jnp.dot(e.astype(BF16), vv, preferred_element_type=F32) * (1.0 / d)
        for g in range(GQA_GROUP):
            hcol = (kh * GQA_GROUP + g) * GQA_DH
            o_ref[r0:r0 + WINDOW, hcol:hcol + GQA_DH] = o[g * WINDOW:(g + 1) * WINDOW].astype(BF16)


def _window(gq, gk, gv, bias, sink_col, *, seq, rows):
    t = gq.shape[0]
    nt = seq // rows
    rb = rows // WINDOW
    nblk = t // WINDOW
    kvw = GQA_KV_HEADS * GQA_DH
    prev = lambda i: (jnp.maximum(i * rb - 1, 0), 0)
    nxt = lambda i: (jnp.minimum((i + 1) * rb, nblk - 1), 0)
    cur = lambda i: (i, 0)
    return pl.pallas_call(
        functools.partial(_window_kernel, rows=rows, nt=nt),
        grid=(t // rows,),
        in_specs=[
            pl.BlockSpec((rows, GQA_HEADS * GQA_DH), cur),
            pl.BlockSpec((WINDOW, kvw), prev),
            pl.BlockSpec((rows, kvw), cur),
            pl.BlockSpec((WINDOW, kvw), nxt),
            pl.BlockSpec((WINDOW, kvw), prev),
            pl.BlockSpec((rows, kvw), cur),
            pl.BlockSpec((WINDOW, kvw), nxt),
            _const_spec(bias.shape),
            _const_spec(sink_col.shape),
        ],
        out_specs=pl.BlockSpec((rows, GQA_HEADS * GQA_DH), cur),
        out_shape=jax.ShapeDtypeStruct((t, GQA_HEADS * GQA_DH), BF16),
        scratch_shapes=[pltpu.VMEM((rows + 2 * WINDOW, kvw), BF16),
                        pltpu.VMEM((rows + 2 * WINDOW, kvw), BF16)],
        compiler_params=_params("parallel"),
        name="window_attn",
    )(gq, gk, gk, gk, gv, gv, gv, bias, sink_col)


def _router(h2, wr_ref, topw_ref, topi_ref):
    h_hi = h2.astype(BF16)
    h_lo = (h2 - h_hi.astype(F32)).astype(BF16)
    logits = (jnp.dot(h_hi, wr_ref[0], preferred_element_type=F32)
              + jnp.dot(h_lo, wr_ref[0], preferred_element_type=F32)
              + jnp.dot(h_hi, wr_ref[1], preferred_element_type=F32))
    lane = lax.broadcasted_iota(jnp.int32, logits.shape, 1)
    lg = jnp.where(lane < N_EXPERTS, logits, -jnp.inf)
    m1 = jnp.max(lg, axis=1, keepdims=True)
    i1 = jnp.min(jnp.where(lg == m1, lane, LANES), axis=1, keepdims=True)
    lg2 = jnp.where(lane == i1, -jnp.inf, lg)
    m2 = jnp.max(lg2, axis=1, keepdims=True)
    i2 = jnp.min(jnp.where(lg2 == m2, lane, LANES), axis=1, keepdims=True)
    e2 = jnp.exp(m2 - m1)
    w1 = 1.0 / (1.0 + e2)
    w2 = e2 * w1
    topw_ref[...] = jnp.where(lane == 0, w1, jnp.where(lane == 1, w2, 0.0))
    topi_ref[...] = jnp.where(lane == 0, i1, jnp.where(lane == 1, i2, 0))


def _postmix_kernel(oa_ref, ob_ref, x_ref, mod_ref, ga_ref, gb_ref, wa_ref, wb_ref, gf_ref, *rest, moe):
    if moe:
        wr_ref, x1_ref, h2_ref, topw_ref, topi_ref, hprev = rest

        @pl.when(pl.program_id(0) == 0)
        def _():
            hprev[...] = jnp.zeros_like(hprev)

        _router(hprev[...], wr_ref, topw_ref, topi_ref)
    else:
        x1_ref, h2_ref = rest
    mod = mod_ref[...]
    na = _rms(oa_ref[...].astype(F32), ga_ref[...]).astype(BF16)
    nb = _rms(ob_ref[...].astype(F32), gb_ref[...]).astype(BF16)
    mix = (jnp.dot(na, wa_ref[...], preferred_element_type=F32)
           + jnp.dot(nb, wb_ref[...], preferred_element_type=F32))
    x1 = x_ref[...] + mod[2:3, :] * mix
    x1_ref[...] = x1
    h2 = _rms(x1, gf_ref[...]) * (1.0 + mod[4:5, :]) + mod[3:4, :]
    h2_ref[...] = h2.astype(h2_ref.dtype)
    if moe:
        hprev[...] = h2


def _postmix(oa, ob, x, mod, ga, gb, wa, wb, gf, wr, *, seq, tm):
    t = x.shape[0]
    nt = seq // tm
    nb = t // tm
    moe = wr is not None
    row = (lambda i: (jnp.minimum(i, nb - 1), 0)) if moe else (lambda i: (i, 0))
    lag = lambda i: (jnp.maximum(i - 1, 0), 0)
    half = oa.shape[1]
    in_specs = [
        pl.BlockSpec((tm, half), row),
        pl.BlockSpec((tm, half), row),
        pl.BlockSpec((tm, D_MODEL), row),
        pl.BlockSpec((None, 6, D_MODEL), lambda i: (row(i)[0] // nt, 0, 0)),
        _const_spec((1, half)),
        _const_spec((1, half)),
        _const_spec(wa.shape),
        _const_spec(wb.shape),
        _const_spec((1, D_MODEL)),
    ]
    out_specs = [pl.BlockSpec((tm, D_MODEL), row), pl.BlockSpec((tm, D_MODEL), row)]
    out_shape = [jax.ShapeDtypeStruct((t, D_MODEL), F32), jax.ShapeDtypeStruct((t, D_MODEL), F32 if moe else BF16)]
    args = [oa, ob, x, mod, ga, gb, wa, wb, gf]
    scratch = []
    if moe:
        in_specs.append(_const_spec(wr.shape))
        out_specs += [pl.BlockSpec((tm, LANES), lag), pl.BlockSpec((tm, LANES), lag)]
        out_shape += [jax.ShapeDtypeStruct((t, LANES), F32), jax.ShapeDtypeStruct((t, LANES), jnp.int32)]
        args.append(wr)
        scratch.append(pltpu.VMEM((tm, D_MODEL), F32))
    return pl.pallas_call(
        functools.partial(_postmix_kernel, moe=moe),
        grid=(nb + 1 if moe else nb,),
        in_specs=in_specs,
        out_specs=out_specs,
        out_shape=out_shape,
        scratch_shapes=scratch,
        compiler_params=_params("arbitrary" if moe else "parallel"),
        name="postmix_moe" if moe else "postmix",
    )(*args)


def _ffn_kernel(x1_ref, h_ref, mod_ref, wg_ref, wu_ref, wd_ref, o_ref, acc_ref):
    f = pl.program_id(1)

    @pl.when(f == 0)
    def _():
        acc_ref[...] = jnp.zeros_like(acc_ref)

    h = h_ref[...]
    a = jnp.dot(h, wg_ref[...], preferred_element_type=F32)
    u = jnp.dot(h, wu_ref[...], preferred_element_type=F32)
    acc_ref[...] += jnp.dot((_silu(a) * u).astype(BF16), wd_ref[...], preferred_element_type=F32)

    @pl.when(f == pl.num_programs(1) - 1)
    def _():
        o_ref[...] = x1_ref[...] + mod_ref[...][5:6, :] * acc_ref[...]


def _ffn(x1, h2, mod, wg, wu, wd, *, seq, tm, tf):
    t = x1.shape[0]
    nt = seq // tm
    dff = wg.shape[1]
    row = lambda i, f: (i, 0)
    return pl.pallas_call(
        _ffn_kernel,
        grid=(t // tm, dff // tf),
        in_specs=[
            pl.BlockSpec((tm, D_MODEL), row),
            pl.BlockSpec((tm, D_MODEL), row),
            pl.BlockSpec((None, 6, D_MODEL), lambda i, f: (i // nt, 0, 0)),
            pl.BlockSpec((D_MODEL, tf), lambda i, f: (0, f)),
            pl.BlockSpec((D_MODEL, tf), lambda i, f: (0, f)),
            pl.BlockSpec((tf, D_MODEL), lambda i, f: (f, 0)),
        ],
        out_specs=pl.BlockSpec((tm, D_MODEL), row),
        out_shape=jax.ShapeDtypeStruct((t, D_MODEL), F32),
        scratch_shapes=[pltpu.VMEM((tm, D_MODEL), F32)],
        compiler_params=_params("parallel", "arbitrary"),
        name="ffn_dense",
    )(x1, h2, mod, wg, wu, wd)


def _route(topi, tm, rp):
    t = topi.shape[0]
    na = 2 * t
    nt = na // tm + N_EXPERTS
    flat_e = topi.reshape(na)
    onehot = (flat_e[:, None] == jnp.arange(N_EXPERTS, dtype=jnp.int32)[None, :]).astype(jnp.int32)
    csum = jnp.cumsum(onehot, axis=0)
    rank = jnp.sum((csum - onehot) * onehot, axis=1)
    padded = ((csum[-1] + tm - 1) // tm) * tm
    ends = jnp.cumsum(padded)
    pos = (ends - padded)[flat_e] + rank
    pair = jnp.full((nt * tm,), -1, jnp.int32).at[pos].set(jnp.arange(na, dtype=jnp.int32))
    pair = jnp.pad(pair.reshape(nt, tm), ((0, 0), (0, rp - tm)), constant_values=-1)
    spare = na + (jnp.arange(nt + 1, dtype=jnp.int32)[:, None] % 3) * rp + jnp.arange(rp, dtype=jnp.int32)[None, :]
    src = jnp.where(pair >= 0, pair // 2, 0)
    dst = jnp.where(pair >= 0, (pair % 2) * t + pair // 2, spare[1:])
    dst = jnp.concatenate([spare[:1], dst], axis=0)
    tile_start = jnp.arange(nt, dtype=jnp.int32) * tm
    tile_valid = (tile_start < ends[-1]).astype(jnp.int32)
    tile_expert = jnp.minimum(jnp.sum((tile_start[:, None] >= ends[None, :]).astype(jnp.int32), axis=1),
                              N_EXPERTS - 1)
    return src.reshape(nt, 1, rp), dst.reshape(nt + 1, 1, rp), tile_expert, tile_valid


def _moe_kernel(te_ref, tv_ref, src0_ref, src1_ref, src2_ref, pdst_ref, dst_ref, h_hbm, wg_ref, wu_ref, wd_ref,
                y_hbm, hbuf, hb16, acc, gsem, ssem, *, tm, ch):
    i = pl.program_id(0)
    f = pl.program_id(1)
    nt = pl.num_programs(0)
    nf = pl.num_programs(1)
    rp = nf * ch
    cur = i % 3
    oth = (i + 2) % 3
    valid = tv_ref[i] == 1
    last_valid = jnp.logical_and(valid, jnp.logical_or(i == nt - 1, tv_ref[jnp.minimum(i + 1, nt - 1)] == 0))

    def gather_row(idx_ref, r, b):
        pltpu.make_async_copy(h_hbm.at[pl.ds(idx_ref[0, r], 1), :], hbuf.at[b, pl.ds(r, 1), :],
                              gsem.at[b]).start()

    def scatter_row(idx_ref, r, b):
        pltpu.make_async_copy(acc.at[b, pl.ds(r, 1), :], y_hbm.at[pl.ds(idx_ref[0, r], 1), :],
                              ssem.at[b]).start()

    def wait_rows(buf, sem, b):
        pltpu.make_async_copy(buf.at[b], buf.at[b], sem.at[b]).wait()

    @pl.when(jnp.logical_and(i == 0, f == 0))
    def _():
        acc[...] = jnp.zeros_like(acc)

        def first(r, c):
            gather_row(src0_ref, r, 0)
            gather_row(src1_ref, r, 1)
            return c
        lax.fori_loop(0, rp, first, 0, unroll=8)

    @pl.when(jnp.logical_and(valid, f == 0))
    def _():
        wait_rows(hbuf, gsem, cur)
        hb16[...] = hbuf[cur, 0:tm, :].astype(BF16)

        @pl.when(i >= 2)
        def _():
            wait_rows(acc, ssem, cur)

        acc[cur, 0:tm, :] = jnp.zeros((tm, D_MODEL), F32)

    @pl.when(valid)
    def _():
        for c in range(ch):
            r = f * ch + c
            gather_row(src2_ref, r, oth)
            scatter_row(pdst_ref, r, oth)
        h = hb16[...]
        a = jnp.dot(h, wg_ref[...], preferred_element_type=F32)
        u = jnp.dot(h, wu_ref[...], preferred_element_type=F32)
        acc[cur, 0:tm, :] += jnp.dot((_silu(a) * u).astype(BF16), wd_ref[...], preferred_element_type=F32)

    @pl.when(jnp.logical_and(last_valid, f == nf - 1))
    def _():
        def own(r, c):
            scatter_row(dst_ref, r, cur)
            return c
        lax.fori_loop(0, rp, own, 0, unroll=8)
        @pl.when(i >= 1)
        def _():
            wait_rows(acc, ssem, (i + 1) % 3)

        wait_rows(acc, ssem, oth)
        wait_rows(acc, ssem, cur)
        wait_rows(hbuf, gsem, (i + 1) % 3)
        wait_rows(hbuf, gsem, oth)
        hbuf[0] = jnp.zeros((rp, D_MODEL), F32)
        for b in range(3):
            spare = pltpu.make_async_copy(hbuf.at[0], y_hbm.at[pl.ds(y_hbm.shape[0] - (b + 1) * rp, rp), :],
                                          gsem.at[0])
            spare.start()
            spare.wait()


def _moe_ffn(h2, topi, wg, wu, wd, *, tm, tf):
    t = h2.shape[0]
    dff = wg.shape[2]
    nf = dff // tf
    ch = -(-tm // (nf * SUBLANES)) * SUBLANES
    rp = nf * ch
    src, dst, tile_expert, tile_valid = _route(topi, tm, rp)
    nt = src.shape[0]
    idx_spec = lambda fn: pl.BlockSpec((None, 1, rp), fn, memory_space=pltpu.SMEM)
    fsel = lambda i, f, tv: jnp.where(tv[i] == 1, f, nf - 1)
    return pl.pallas_call(
        functools.partial(_moe_kernel, tm=tm, ch=ch),
        grid_spec=pltpu.PrefetchScalarGridSpec(
            num_scalar_prefetch=2,
            grid=(nt, nf),
            in_specs=[
                idx_spec(lambda i, f, te, tv: (i, 0, 0)),
                idx_spec(lambda i, f, te, tv: (jnp.minimum(i + 1, nt - 1), 0, 0)),
                idx_spec(lambda i, f, te, tv: (jnp.minimum(i + 2, nt - 1), 0, 0)),
                idx_spec(lambda i, f, te, tv: (i, 0, 0)),
                idx_spec(lambda i, f, te, tv: (i + 1, 0, 0)),
                pl.BlockSpec(memory_space=pl.ANY),
                pl.BlockSpec((None, D_MODEL, tf), lambda i, f, te, tv: (te[i], 0, fsel(i, f, tv))),
                pl.BlockSpec((None, D_MODEL, tf), lambda i, f, te, tv: (te[i], 0, fsel(i, f, tv))),
                pl.BlockSpec((None, tf, D_MODEL), lambda i, f, te, tv: (te[i], fsel(i, f, tv), 0)),
            ],
            out_specs=pl.BlockSpec(memory_space=pl.ANY),
            scratch_shapes=[
                pltpu.VMEM((3, rp, D_MODEL), F32),
                pltpu.VMEM((tm, D_MODEL), BF16),
                pltpu.VMEM((3, rp, D_MODEL), F32),
                pltpu.SemaphoreType.DMA((3,)),
                pltpu.SemaphoreType.DMA((3,)),
            ],
        ),
        out_shape=jax.ShapeDtypeStruct((2 * t + 3 * rp, D_MODEL), F32),
        compiler_params=_params("arbitrary", "arbitrary"),
        name="ffn_moe",
    )(tile_expert, tile_valid, src, src, src, dst, dst, h2, wg, wu, wd)


def _combine_kernel(x1_ref, ya_ref, yb_ref, topw_ref, mod_ref, gfin_ref, o_ref):
    w = topw_ref[...]
    f = w[:, 0:1] * ya_ref[...] + w[:, 1:2] * yb_ref[...]
    o_ref[...] = _rms(x1_ref[...] + mod_ref[...][5:6, :] * f, gfin_ref[...])


def _combine(x1, y2, topw, mod, gfin, *, seq, tm):
    t = x1.shape[0]
    nt = seq // tm
    nb = t // tm
    row = lambda i: (i, 0)
    return pl.pallas_call(
        _combine_kernel,
        grid=(nb,),
        in_specs=[
            pl.BlockSpec((tm, D_MODEL), row),
            pl.BlockSpec((tm, D_MODEL), row),
            pl.BlockSpec((tm, D_MODEL), lambda i: (nb + i, 0)),
            pl.BlockSpec((tm, LANES), row),
            pl.BlockSpec((None, 6, D_MODEL), lambda i: (i // nt, 0, 0)),
            _const_spec((1, D_MODEL)),
        ],
        out_specs=pl.BlockSpec((tm, D_MODEL), row),
        out_shape=jax.ShapeDtypeStruct((t, D_MODEL), F32),
        compiler_params=_params("parallel"),
        name="moe_combine",
    )(x1, y2, y2, topw, mod, gfin)


def _rope_tables(seq):
    pos = jnp.arange(seq, dtype=F32)
    inv = 1.0 / (ROPE_THETA ** (jnp.arange(0, MLA_DR, 2, dtype=F32) / MLA_DR))
    ang = pos[:, None] * inv[None, :]
    cos, sin = jnp.cos(ang), jnp.sin(ang)
    zero = jnp.zeros((seq, LANES - MLA_DR), F32)
    return (jnp.concatenate([cos, cos, zero], axis=1), jnp.concatenate([-sin, sin, zero], axis=1))


def _t5_bucket(rel):
    nb = NUM_BUCKETS // 2
    ret = (rel > 0).astype(jnp.int32) * nb
    n = jnp.abs(rel)
    max_exact = nb // 2
    nf = jnp.maximum(n, 1).astype(F32)
    large = max_exact + (jnp.log(nf / max_exact) / math.log(MAX_DISTANCE / max_exact)
                         * (nb - max_exact)).astype(jnp.int32)
    large = jnp.minimum(large, nb - 1)
    return ret + jnp.where(n < max_exact, n, large)


def _window_bias(rel_bias):
    qpos = jnp.arange(WINDOW, dtype=jnp.int32)
    jpos = jnp.arange(3 * WINDOW, dtype=jnp.int32)
    rel = jpos[None, :] - WINDOW - qpos[:, None]
    bias = rel_bias[_t5_bucket(rel)].astype(F32).transpose(2, 0, 1)
    bias = jnp.where((jnp.abs(rel) <= WINDOW)[None], bias, NEG_BIG)
    return bias.reshape(GQA_KV_HEADS, GQA_GROUP * WINDOW, 3 * WINDOW)


def _prep_layer(w_in, w_uq, w_ukv, w_out):
    half = MLA_DR // 2
    zpad = jnp.zeros((D_MODEL, LANES - MLA_DR), F32)
    kr0 = Q_LORA + KV_LORA
    k1 = w_in[:, kr0:kr0 + half]
    k2 = w_in[:, kr0 + half:kr0 + MLA_DR]
    w_in_p = jnp.concatenate(
        [w_in[:, :kr0], k1, k2, zpad, k2, k1, zpad, w_in[:, kr0 + MLA_DR:]], axis=1).astype(BF16)
    wq = w_uq.reshape(Q_LORA, MLA_HEADS, MLA_DN + MLA_DR)
    r1 = wq[:, :, MLA_DN:MLA_DN + half]
    r2 = wq[:, :, MLA_DN + half:]
    zq = jnp.zeros((Q_LORA, MLA_HEADS, LANES - MLA_DR), F32)
    w_uq_a = jnp.concatenate([wq[:, :, :MLA_DN], r1, r2, zq], axis=2).reshape(Q_LORA, MLA_HEADS * QK_PAD)
    w_uq_b = jnp.concatenate([r2, r1, zq], axis=2).reshape(Q_LORA, MLA_HEADS * LANES)
    out_a = MLA_HEADS * MLA_DV
    return (w_in_p, w_uq_a.astype(BF16), w_uq_b.astype(BF16), w_ukv.astype(BF16),
            w_out[:out_a].astype(BF16), w_out[out_a:].astype(BF16))


def _trunk(x, mods, layers, bias, g_final, *, batch, seq):
    tm = min(ROW_TILE, seq)
    cos_t, sin_t = _rope_tables(seq)
    for l, p in enumerate(layers):
        q, k, v, gq, gk, gv = _premix(x, mods[l], p["g_norm_mix"], cos_t, sin_t, p["w_in_p"], p["g_q_lat"],
                                      p["w_uq_a"], p["w_uq_b"], p["g_kv_lat"], p["w_ukv"], seq=seq, tm=tm)
        tq_sub = min(MLA_TQ_SUB, seq)
        tq = min(MLA_TQ, seq)
        oa = _mla(q, k, v, batch=batch, seq=seq, tq=tq, tk=min(MLA_TK, seq), nsub=tq // tq_sub)
        ob = _window(gq, gk, gv, bias, p["sink_col"], seq=seq, rows=min(ROW_TILE, seq))
        res = _postmix(oa, ob, x, mods[l], p["g_out_a"], p["g_out_b"], p["w_out_a"], p["w_out_b"],
                       p["g_norm_ffn"], p.get("w_router"), seq=seq, tm=tm)
        if "w_router" in p:
            assert l == len(layers) - 1
            x1, h2, topw, topi = res
            y2 = _moe_ffn(h2, topi[:, :2], p["w_gate"], p["w_up"], p["w_down"], tm=ROW_TILE, tf=FF_TILE)
            x = _combine(x1, y2, topw, mods[l], g_final, seq=seq, tm=tm)
        else:
            x1, h2 = res
            x = _ffn(x1, h2, mods[l], p["w_gate"], p["w_up"], p["w_down"], seq=seq, tm=tm, tf=FF_TILE)
    return x


def kernel(x_prompt, x_sample, c_prompt, c_sample, rel_bias, w_ada, b_ada, g_norm_mix, g_norm_ffn, w_in, g_q_lat, w_uq, g_kv_lat, w_ukv, sink, g_out_a, g_out_b, w_out, w_gate_d, w_up_d, w_down_d, w_router, w_gate_e, w_up_e, w_down_e, g_final):
    bp, sp, _ = x_prompt.shape
    bs, ss, _ = x_sample.shape
    rows = -(-(bp + bs) // 8) * 8
    c_all = jnp.concatenate([c_prompt, c_sample, jnp.zeros((rows - bp - bs, D_MODEL), F32)], axis=0)
    mod = _ada_mod(c_all, w_ada, b_ada)
    mod_p = [mod[l, :bp].reshape(bp, 6, D_MODEL) for l in range(DEPTH)]
    mod_s = [mod[l, bp:bp + bs].reshape(bs, 6, D_MODEL) for l in range(DEPTH)]
    bias = _window_bias(rel_bias)
    layers = []
    for l in range(DEPTH):
        w_in_p, w_uq_a, w_uq_b, w_ukv_b, w_out_a, w_out_b = _prep_layer(w_in[l], w_uq[l], w_ukv[l], w_out[l])
        out_a = MLA_HEADS * MLA_DV
        p = dict(
            g_norm_mix=g_norm_mix[l][None], g_norm_ffn=g_norm_ffn[l][None],
            w_in_p=w_in_p, g_q_lat=g_q_lat[l][None], w_uq_a=w_uq_a, w_uq_b=w_uq_b,
            g_kv_lat=g_kv_lat[l][None], w_ukv=w_ukv_b,
            sink_col=jnp.repeat(sink[l].astype(F32), WINDOW).reshape(GQA_KV_HEADS, GQA_GROUP * WINDOW, 1),
            g_out_a=g_out_a[l][None], g_out_b=g_out_b[l][None], w_out_a=w_out_a, w_out_b=w_out_b,
        )
        i = l // 2
        if l % 2 == 0:
            p.update(w_gate=w_gate_d[i].astype(BF16), w_up=w_up_d[i].astype(BF16), w_down=w_down_d[i].astype(BF16))
        else:
            wr = jnp.concatenate([w_router[i], jnp.zeros((D_MODEL, LANES - N_EXPERTS), F32)], axis=1)
            wr_hi = wr.astype(BF16)
            wr = jnp.stack([wr_hi, (wr - wr_hi.astype(F32)).astype(BF16)])
            p.update(w_gate=w_gate_e[i].astype(BF16), w_up=w_up_e[i].astype(BF16),
                     w_down=w_down_e[i].astype(BF16), w_router=wr)
        layers.append(p)
    gfin = g_final[None]
    y_s = _trunk(x_sample.reshape(bs * ss, D_MODEL), mod_s, layers, bias, gfin, batch=bs, seq=ss)
    y_p = _trunk(x_prompt.reshape(bp * sp, D_MODEL), mod_p, layers, bias, gfin, batch=bp, seq=sp)
    return (y_p.reshape(bp, sp, D_MODEL), y_s.reshape(bs, ss, D_MODEL))
```

```python
import functools
import math

import jax
import jax.numpy as jnp
from jax import lax
from jax.experimental import pallas as pl
from jax.experimental.pallas import tpu as pltpu

F32 = jnp.float32
BF16 = jnp.bfloat16

D_MODEL = 2048
DEPTH = 2
EPS = 1e-6
MLA_HEADS = 8
MLA_DN = 128
MLA_DR = 64
MLA_DV = 128
Q_LORA = 512
KV_LORA = 512
ROPE_THETA = 10000.0
MLA_SCALE = 1.0 / math.sqrt(MLA_DN + MLA_DR)
MLA_QSCALE = MLA_SCALE * math.log2(math.e)
GQA_HEADS = 8
GQA_KV_HEADS = 2
GQA_GROUP = GQA_HEADS // GQA_KV_HEADS
GQA_DH = 128
WINDOW = 128
GQA_SCALE = 1.0 / math.sqrt(GQA_DH)
NUM_BUCKETS = 32
MAX_DISTANCE = 128
N_EXPERTS = 8
NEG_BIG = -1e30

LANES = 128
SUBLANES = 8
V7X_VMEM_LIMIT = 56 * 1024 * 1024

QK_PAD = 2 * LANES
VT_ROWS = MLA_DV + 16
MLA_TQ = 2048
MLA_TQ_SUB = 256
MLA_TK = 512
ROW_TILE = 512
FF_TILE = 512
Z_CQ = 0
Z_CKV = Z_CQ + Q_LORA
Z_KRA = Z_CKV + KV_LORA
Z_KRB = Z_KRA + LANES
Z_GQ = Z_KRB + LANES
Z_GK = Z_GQ + GQA_HEADS * GQA_DH
Z_GV = Z_GK + GQA_KV_HEADS * GQA_DH
Z_END = Z_GV + GQA_KV_HEADS * GQA_DH


def _params(*sem):
    return pltpu.CompilerParams(dimension_semantics=sem, vmem_limit_bytes=V7X_VMEM_LIMIT)


def _rms(x, g):
    return x * lax.rsqrt(jnp.mean(x * x, axis=-1, keepdims=True) + EPS) * g


def _silu(a):
    return a * (1.0 / (1.0 + jnp.exp(-a)))


def _const_spec(shape):
    nd = len(shape)
    return pl.BlockSpec(shape, lambda *_: (0,) * nd)


def _ada_kernel(c_ref, w_ref, b_ref, o_ref):
    cs = _silu(c_ref[...])
    o_ref[...] = jnp.dot(cs, w_ref[...], preferred_element_type=F32,
                         precision=lax.Precision.HIGHEST) + b_ref[...]


def _ada_mod(c_all, w_ada, b_ada):
    rows = c_all.shape[0]
    n = w_ada.shape[-1]
    tn = 1024
    return pl.pallas_call(
        _ada_kernel,
        grid=(DEPTH, n // tn),
        in_specs=[
            pl.BlockSpec((rows, D_MODEL), lambda l, j: (0, 0)),
            pl.BlockSpec((None, D_MODEL, tn), lambda l, j: (l, 0, j)),
            pl.BlockSpec((None, 1, tn), lambda l, j: (l, 0, j)),
        ],
        out_specs=pl.BlockSpec((None, rows, tn), lambda l, j: (l, 0, j)),
        out_shape=jax.ShapeDtypeStruct((DEPTH, rows, n), F32),
        compiler_params=_params("parallel", "parallel"),
        name="ada_mod",
    )(c_all, w_ada, b_ada.reshape(DEPTH, 1, n))


def _premix_kernel(x_ref, mod_ref, gn_ref, cos_ref, sin_ref, win_ref, gq_ref, wuqa_ref, wuqb_ref,
                   gkv_ref, wukv_ref, q_ref, k_ref, v_ref, wq_ref, wk_ref, wv_ref):
    mod = mod_ref[...]
    h = _rms(x_ref[...], gn_ref[...]) * (1.0 + mod[1:2, :]) + mod[0:1, :]
    z = jnp.dot(h.astype(BF16), win_ref[...], preferred_element_type=F32)
    cqn = _rms(z[:, Z_CQ:Z_CKV], gq_ref[...]).astype(BF16)
    ckvn = _rms(z[:, Z_CKV:Z_KRA], gkv_ref[...]).astype(BF16)
    cos_t = cos_ref[...]
    sin_t = sin_ref[...]
    kr = (z[:, Z_KRA:Z_KRB] * cos_t + z[:, Z_KRB:Z_GQ] * sin_t).astype(BF16)
    wq_ref[...] = (z[:, Z_GQ:Z_GK] * GQA_SCALE).astype(BF16)
    wk_ref[...] = z[:, Z_GK:Z_GV].astype(BF16)
    wv_ref[...] = z[:, Z_GV:Z_END].astype(BF16)
    qa = jnp.dot(cqn, wuqa_ref[...], preferred_element_type=F32)
    qb = jnp.dot(cqn, wuqb_ref[...], preferred_element_type=F32)
    kv = jnp.dot(ckvn, wukv_ref[...], preferred_element_type=F32)
    vchunk = v_ref.shape[3]
    sub = lax.broadcasted_iota(jnp.int32, (VT_ROWS - MLA_DV, vchunk), 0)
    ones_row = jnp.where(sub == 0, 1.0, 0.0).astype(BF16)
    for hh in range(MLA_HEADS):
        a0 = hh * QK_PAD
        q_ref[hh, :, 0:LANES] = (qa[:, a0:a0 + LANES] * MLA_QSCALE).astype(BF16)
        q_rope = qa[:, a0 + LANES:a0 + QK_PAD] * cos_t + qb[:, hh * LANES:(hh + 1) * LANES] * sin_t
        q_ref[hh, :, LANES:QK_PAD] = (q_rope * MLA_QSCALE).astype(BF16)
        k_ref[hh, :, 0:LANES] = kv[:, a0:a0 + LANES].astype(BF16)
        k_ref[hh, :, LANES:QK_PAD] = kr
        vt = kv[:, a0 + LANES:a0 + QK_PAD].T.astype(BF16)
        for c in range(v_ref.shape[1]):
            v_ref[hh, c, 0:MLA_DV, :] = vt[:, c * vchunk:(c + 1) * vchunk]
            v_ref[hh, c, MLA_DV:VT_ROWS, :] = ones_row


def _premix(x, mod, gn, cos_t, sin_t, w_in_p, g_q, w_uq_a, w_uq_b, g_kv, w_ukv, *, seq, tm, vchunk):
    t = x.shape[0]
    nt = seq // tm
    cpt = tm // vchunk
    row = lambda i: (i, 0)
    hrow = lambda i: (0, i, 0)
    return pl.pallas_call(
        _premix_kernel,
        grid=(t // tm,),
        in_specs=[
            pl.BlockSpec((tm, D_MODEL), row),
            pl.BlockSpec((None, 6, D_MODEL), lambda i: (i // nt, 0, 0)),
            _const_spec((1, D_MODEL)),
            pl.BlockSpec((tm, LANES), lambda i: (i % nt, 0)),
            pl.BlockSpec((tm, LANES), lambda i: (i % nt, 0)),
            _const_spec(w_in_p.shape),
            _const_spec((1, Q_LORA)),
            _const_spec(w_uq_a.shape),
            _const_spec(w_uq_b.shape),
            _const_spec((1, KV_LORA)),
            _const_spec(w_ukv.shape),
        ],
        out_specs=[
            pl.BlockSpec((MLA_HEADS, tm, QK_PAD), hrow),
            pl.BlockSpec((MLA_HEADS, tm, QK_PAD), hrow),
            pl.BlockSpec((MLA_HEADS, cpt, VT_ROWS, vchunk), lambda i: (0, i, 0, 0)),
            pl.BlockSpec((tm, GQA_HEADS * GQA_DH), row),
            pl.BlockSpec((tm, GQA_KV_HEADS * GQA_DH), row),
            pl.BlockSpec((tm, GQA_KV_HEADS * GQA_DH), row),
        ],
        out_shape=[
            jax.ShapeDtypeStruct((MLA_HEADS, t, QK_PAD), BF16),
            jax.ShapeDtypeStruct((MLA_HEADS, t, QK_PAD), BF16),
            jax.ShapeDtypeStruct((MLA_HEADS, t // vchunk, VT_ROWS, vchunk), BF16),
            jax.ShapeDtypeStruct((t, GQA_HEADS * GQA_DH), BF16),
            jax.ShapeDtypeStruct((t, GQA_KV_HEADS * GQA_DH), BF16),
            jax.ShapeDtypeStruct((t, GQA_KV_HEADS * GQA_DH), BF16),
        ],
        compiler_params=_params("parallel"),
        name="premix",
    )(x, mod, gn, cos_t, sin_t, w_in_p, g_q, w_uq_a, w_uq_b, g_kv, w_ukv)


def _mla_kernel(q_ref, k_ref, v_ref, o_ref, qt_scr, s_scr, p_scr, m_scr, a_scr, acc_scr, pm_scr, *, tk, nk, nsub):
    tq = q_ref.shape[0] // nsub
    subs = range(nsub)
    for i in subs:
        qt_scr[i] = q_ref[i * tq:(i + 1) * tq, :].astype(F32).T.astype(BF16)

    def scores(j, buf, i):
        k = k_ref[pl.ds(pl.multiple_of(j * tk, tk), tk), :]
        s = jnp.dot(k, qt_scr[i], preferred_element_type=F32)
        s_scr[buf, i] = s
        pm = s[0:SUBLANES, :]
        for g in range(1, tk // SUBLANES):
            pm = jnp.maximum(pm, s[g * SUBLANES:(g + 1) * SUBLANES, :])
        pm_scr[buf, i] = pm

    def smax(buf, i):
        m_old = m_scr[i]
        m_new = jnp.maximum(m_old, jnp.max(pm_scr[buf, i], axis=0, keepdims=True))
        m_scr[i] = m_new
        a_scr[buf, i] = jnp.exp2(m_old - m_new)

    def sexp(buf, i):
        p_scr[buf, i] = jnp.exp2(s_scr[buf, i] - m_scr[i]).astype(BF16)

    def softmax(buf, i):
        smax(buf, i)
        sexp(buf, i)

    def values(j, buf, i):
        v = v_ref[j]
        pv = jnp.dot(v, p_scr[buf, i], preferred_element_type=F32)
        acc_scr[i] = a_scr[buf, i] * acc_scr[i] + pv

    m_scr[...] = jnp.full(m_scr.shape, -jnp.inf, F32)
    acc_scr[...] = jnp.zeros(acc_scr.shape, F32)
    for i in subs:
        scores(0, 0, i)
        scores(1, 1, i)
        softmax(0, i)

    def body(u, carry):
        t = 2 * u + 1
        for i in subs:
            smax(1, i)
            scores(t + 1, 0, i)
            values(t - 1, 0, i)
            sexp(1, i)
        for i in subs:
            smax(0, i)
            scores(t + 2, 1, i)
            values(t, 1, i)
            sexp(0, i)
        return carry

    lax.fori_loop(0, (nk - 2) // 2, body, 0)
    for i in subs:
        softmax(1, i)
        values(nk - 2, 0, i)
        values(nk - 1, 1, i)
    for i in subs:
        acc = acc_scr[i]
        o_ref[i * tq:(i + 1) * tq, :] = (acc[0:MLA_DV, :] * (1.0 / acc[MLA_DV:MLA_DV + 1, :])).T.astype(BF16)


def _mla(q, k, v, *, batch, seq, tq, tk, nsub):
    t = batch * seq
    nq = seq // tq
    nk = seq // tk
    assert nk >= 2 and nk % 2 == 0, (seq, tk)
    sub = tq // nsub
    return pl.pallas_call(
        functools.partial(_mla_kernel, tk=tk, nk=nk, nsub=nsub),
        grid=(batch, MLA_HEADS, nq),
        in_specs=[
            pl.BlockSpec((None, tq, QK_PAD), lambda b, h, i: (h, b * nq + i, 0)),
            pl.BlockSpec((None, seq, QK_PAD), lambda b, h, i: (h, b, 0)),
            pl.BlockSpec((None, nk, VT_ROWS, tk), lambda b, h, i: (h, b, 0, 0)),
        ],
        out_specs=pl.BlockSpec((tq, MLA_DV), lambda b, h, i: (b * nq + i, h)),
        out_shape=jax.ShapeDtypeStruct((t, MLA_HEADS * MLA_DV), BF16),
        scratch_shapes=[
            pltpu.VMEM((nsub, QK_PAD, sub), BF16),
            pltpu.VMEM((2, nsub, tk, sub), F32),
            pltpu.VMEM((2, nsub, tk, sub), BF16),
            pltpu.VMEM((nsub, 1, sub), F32),
            pltpu.VMEM((2, nsub, 1, sub), F32),
            pltpu.VMEM((nsub, VT_ROWS, sub), F32),
            pltpu.VMEM((2, nsub, SUBLANES, sub), F32),
        ],
        compiler_params=_params("parallel", "parallel", "parallel"),
        name="mla_attn",
    )(q, k, v)


def _window_kernel(q_ref, kp_ref, kc_ref, kn_ref, vp_ref, vc_ref, vn_ref, bias_ref, sink_ref, o_ref,
                   kcat, vcat, *, rows, nt):
    i = pl.program_id(0)
    nsub = rows // WINDOW
    kcat[0:WINDOW, :] = kp_ref[...]
    kcat[WINDOW:WINDOW + rows, :] = kc_ref[...]
    kcat[WINDOW + rows:, :] = kn_ref[...]
    vcat[0:WINDOW, :] = vp_ref[...]
    vcat[WINDOW:WINDOW + rows, :] = vc_ref[...]
    vcat[WINDOW + rows:, :] = vn_ref[...]
    first_tile = (i % nt) == 0
    last_tile = (i % nt) == nt - 1
    col = lax.broadcasted_iota(jnp.int32, (1, 3 * WINDOW), 1)

    chains = [(n, kh) for n in range(nsub) for kh in range(GQA_KV_HEADS)]
    s_all = []
    for n, kh in chains:
        r0 = n * WINDOW
        qall = q_ref[r0:r0 + WINDOW, kh * GQA_GROUP * GQA_DH:(kh + 1) * GQA_GROUP * GQA_DH]
        qs = jnp.concatenate([qall[:, g * GQA_DH:(g + 1) * GQA_DH] for g in range(GQA_GROUP)], axis=0)
        kk = kcat[r0:r0 + 3 * WINDOW, kh * GQA_DH:(kh + 1) * GQA_DH]
        s = lax.dot_general(qs, kk, (((1,), (1,)), ((), ())), preferred_element_type=F32) + bias_ref[kh]
        if n == 0:
            s = s + jnp.where(jnp.logical_and(first_tile, col < WINDOW), NEG_BIG, 0.0)
        if n == nsub - 1:
            s = s + jnp.where(jnp.logical_and(last_tile, col >= 2 * WINDOW), NEG_BIG, 0.0)
        s_all.append(s)
    m_all = [jnp.maximum(jnp.max(s, axis=1, keepdims=True), sink_ref[kh]) for s, (n, kh) in zip(s_all, chains)]
    e_all = [jnp.exp(s - m) for s, m in zip(s_all, m_all)]
    d_all = [jnp.sum(e, axis=1, keepdims=True) + jnp.exp(sink_ref[kh] - m)
             for e, m, (n, kh) in zip(e_all, m_all, chains)]
    for e, d, (n, kh) in zip(e_all, d_all, chains):
        r0 = n * WINDOW
        vv = vcat[r0:r0 + 3 * WINDOW, kh * GQA_DH:(kh + 1) * GQA_DH]
        o = jnp.dot(e.astype(BF16), vv, preferred_element_type=F32) * (1.0 / d)
        for g in range(GQA_GROUP):
            hcol = (kh * GQA_GROUP + g) * GQA_DH
            o_ref[r0:r0 + WINDOW, hcol:hcol + GQA_DH] = o[g * WINDOW:(g + 1) * WINDOW].astype(BF16)


def _window(gq, gk, gv, bias, sink_col, *, seq, rows):
    t = gq.shape[0]
    nt = seq // rows
    rb = rows // WINDOW
    nblk = t // WINDOW
    kvw = GQA_KV_HEADS * GQA_DH
    prev = lambda i: (jnp.maximum(i * rb - 1, 0), 0)
    nxt = lambda i: (jnp.minimum((i + 1) * rb, nblk - 1), 0)
    cur = lambda i: (i, 0)
    return pl.pallas_call(
        functools.partial(_window_kernel, rows=rows, nt=nt),
        grid=(t // rows,),
        in_specs=[
            pl.BlockSpec((rows, GQA_HEADS * GQA_DH), cur),
            pl.BlockSpec((WINDOW, kvw), prev),
            pl.BlockSpec((rows, kvw), cur),
            pl.BlockSpec((WINDOW, kvw), nxt),
            pl.BlockSpec((WINDOW, kvw), prev),
            pl.BlockSpec((rows, kvw), cur),
            pl.BlockSpec((WINDOW, kvw), nxt),
            _const_spec(bias.shape),
            _const_spec(sink_col.shape),
        ],
        out_specs=pl.BlockSpec((rows, GQA_HEADS * GQA_DH), cur),
        out_shape=jax.ShapeDtypeStruct((t, GQA_HEADS * GQA_DH), BF16),
        scratch_shapes=[pltpu.VMEM((rows + 2 * WINDOW, kvw), BF16),
                        pltpu.VMEM((rows + 2 * WINDOW, kvw), BF16)],
        compiler_params=_params("parallel"),
        name="window_attn",
    )(gq, gk, gk, gk, gv, gv, gv, bias, sink_col)


def _router(h2, wr_ref, topw_ref, topi_ref):
    h_hi = h2.astype(BF16)
    h_lo = (h2 - h_hi.astype(F32)).astype(BF16)
    logits = (jnp.dot(h_hi, wr_ref[0], preferred_element_type=F32)
              + jnp.dot(h_lo, wr_ref[0], preferred_element_type=F32)
              + jnp.dot(h_hi, wr_ref[1], preferred_element_type=F32))
    lane = lax.broadcasted_iota(jnp.int32, logits.shape, 1)
    lg = jnp.where(lane < N_EXPERTS, logits, -jnp.inf)
    m1 = jnp.max(lg, axis=1, keepdims=True)
    i1 = jnp.min(jnp.where(lg == m1, lane, LANES), axis=1, keepdims=True)
    lg2 = jnp.where(lane == i1, -jnp.inf, lg)
    m2 = jnp.max(lg2, axis=1, keepdims=True)
    i2 = jnp.min(jnp.where(lg2 == m2, lane, LANES), axis=1, keepdims=True)
    e2 = jnp.exp(m2 - m1)
    w1 = 1.0 / (1.0 + e2)
    w2 = e2 * w1
    topw_ref[...] = jnp.where(lane == 0, w1, jnp.where(lane == 1, w2, 0.0))
    topi_ref[...] = jnp.where(lane == 0, i1, jnp.where(lane == 1, i2, 0))


def _postmix_kernel(oa_ref, ob_ref, x_ref, mod_ref, ga_ref, gb_ref, wa_ref, wb_ref, gf_ref, *rest, moe):
    if moe:
        wr_ref, x1_ref, h2_ref, topw_ref, topi_ref, hprev = rest

        @pl.when(pl.program_id(0) == 0)
        def _():
            hprev[...] = jnp.zeros_like(hprev)

        _router(hprev[...], wr_ref, topw_ref, topi_ref)
    else:
        x1_ref, h2_ref = rest
    mod = mod_ref[...]
    na = _rms(oa_ref[...].astype(F32), ga_ref[...]).astype(BF16)
    nb = _rms(ob_ref[...].astype(F32), gb_ref[...]).astype(BF16)
    mix = (jnp.dot(na, wa_ref[...], preferred_element_type=F32)
           + jnp.dot(nb, wb_ref[...], preferred_element_type=F32))
    x1 = x_ref[...] + mod[2:3, :] * mix
    x1_ref[...] = x1
    h2 = _rms(x1, gf_ref[...]) * (1.0 + mod[4:5, :]) + mod[3:4, :]
    h2_ref[...] = h2.astype(h2_ref.dtype)
    if moe:
        hprev[...] = h2


def _postmix(oa, ob, x, mod, ga, gb, wa, wb, gf, wr, *, seq, tm):
    t = x.shape[0]
    nt = seq // tm
    nb = t // tm
    moe = wr is not None
    row = (lambda i: (jnp.minimum(i, nb - 1), 0)) if moe else (lambda i: (i, 0))
    lag = lambda i: (jnp.maximum(i - 1, 0), 0)
    half = oa.shape[1]
    in_specs = [
        pl.BlockSpec((tm, half), row),
        pl.BlockSpec((tm, half), row),
        pl.BlockSpec((tm, D_MODEL), row),
        pl.BlockSpec((None, 6, D_MODEL), lambda i: (row(i)[0] // nt, 0, 0)),
        _const_spec((1, half)),
        _const_spec((1, half)),
        _const_spec(wa.shape),
        _const_spec(wb.shape),
        _const_spec((1, D_MODEL)),
    ]
    out_specs = [pl.BlockSpec((tm, D_MODEL), row), pl.BlockSpec((tm, D_MODEL), row)]
    out_shape = [jax.ShapeDtypeStruct((t, D_MODEL), F32), jax.ShapeDtypeStruct((t, D_MODEL), F32 if moe else BF16)]
    args = [oa, ob, x, mod, ga, gb, wa, wb, gf]
    scratch = []
    if moe:
        in_specs.append(_const_spec(wr.shape))
        out_specs += [pl.BlockSpec((tm, LANES), lag), pl.BlockSpec((tm, LANES), lag)]
        out_shape += [jax.ShapeDtypeStruct((t, LANES), F32), jax.ShapeDtypeStruct((t, LANES), jnp.int32)]
        args.append(wr)
        scratch.append(pltpu.VMEM((tm, D_MODEL), F32))
    return pl.pallas_call(
        functools.partial(_postmix_kernel, moe=moe),
        grid=(nb + 1 if moe else nb,),
        in_specs=in_specs,
        out_specs=out_specs,
        out_shape=out_shape,
        scratch_shapes=scratch,
        compiler_params=_params("arbitrary" if moe else "parallel"),
        name="postmix_moe" if moe else "postmix",
    )(*args)


def _ffn_kernel(x1_ref, h_ref, mod_ref, wg_ref, wu_ref, wd_ref, o_ref, acc_ref):
    f = pl.program_id(1)

    @pl.when(f == 0)
    def _():
        acc_ref[...] = jnp.zeros_like(acc_ref)

    h = h_ref[...]
    a = jnp.dot(h, wg_ref[...], preferred_element_type=F32)
    u = jnp.dot(h, wu_ref[...], preferred_element_type=F32)
    acc_ref[...] += jnp.dot((_silu(a) * u).astype(BF16), wd_ref[...], preferred_element_type=F32)

    @pl.when(f == pl.num_programs(1) - 1)
    def _():
        o_ref[...] = x1_ref[...] + mod_ref[...][5:6, :] * acc_ref[...]


def _ffn(x1, h2, mod, wg, wu, wd, *, seq, tm, tf):
    t = x1.shape[0]
    nt = seq // tm
    dff = wg.shape[1]
    row = lambda i, f: (i, 0)
    return pl.pallas_call(
        _ffn_kernel,
        grid=(t // tm, dff // tf),
        in_specs=[
            pl.BlockSpec((tm, D_MODEL), row),
            pl.BlockSpec((tm, D_MODEL), row),
            pl.BlockSpec((None, 6, D_MODEL), lambda i, f: (i // nt, 0, 0)),
            pl.BlockSpec((D_MODEL, tf), lambda i, f: (0, f)),
            pl.BlockSpec((D_MODEL, tf), lambda i, f: (0, f)),
            pl.BlockSpec((tf, D_MODEL), lambda i, f: (f, 0)),
        ],
        out_specs=pl.BlockSpec((tm, D_MODEL), row),
        out_shape=jax.ShapeDtypeStruct((t, D_MODEL), F32),
        scratch_shapes=[pltpu.VMEM((tm, D_MODEL), F32)],
        compiler_params=_params("parallel", "arbitrary"),
        name="ffn_dense",
    )(x1, h2, mod, wg, wu, wd)


def _route(topi, tm, rp):
    t = topi.shape[0]
    na = 2 * t
    nt = na // tm + N_EXPERTS
    flat_e = topi.reshape(na)
    onehot = (flat_e[:, None] == jnp.arange(N_EXPERTS, dtype=jnp.int32)[None, :]).astype(jnp.int32)
    csum = jnp.cumsum(onehot, axis=0)
    rank = jnp.sum((csum - onehot) * onehot, axis=1)
    padded = ((csum[-1] + tm - 1) // tm) * tm
    ends = jnp.cumsum(padded)
    pos = (ends - padded)[flat_e] + rank
    pair = jnp.full((nt * tm,), -1, jnp.int32).at[pos].set(jnp.arange(na, dtype=jnp.int32))
    pair = jnp.pad(pair.reshape(nt, tm), ((0, 0), (0, rp - tm)), constant_values=-1)
    spare = na + (jnp.arange(nt + 1, dtype=jnp.int32)[:, None] % 3) * rp + jnp.arange(rp, dtype=jnp.int32)[None, :]
    src = jnp.where(pair >= 0, pair // 2, 0)
    dst = jnp.where(pair >= 0, (pair % 2) * t + pair // 2, spare[1:])
    dst = jnp.concatenate([spare[:1], dst], axis=0)
    tile_start = jnp.arange(nt, dtype=jnp.int32) * tm
    tile_valid = (tile_start < ends[-1]).astype(jnp.int32)
    tile_expert = jnp.minimum(jnp.sum((tile_start[:, None] >= ends[None, :]).astype(jnp.int32), axis=1),
                              N_EXPERTS - 1)
    return src.reshape(nt, 1, rp), dst.reshape(nt + 1, 1, rp), tile_expert, tile_valid


def _moe_kernel(te_ref, tv_ref, src0_ref, src1_ref, src2_ref, pdst_ref, dst_ref, h_hbm, wg_ref, wu_ref, wd_ref,
                y_hbm, hbuf, hb16, acc, gsem, ssem, *, tm, ch):
    i = pl.program_id(0)
    f = pl.program_id(1)
    nt = pl.num_programs(0)
    nf = pl.num_programs(1)
    rp = nf * ch
    cur = i % 3
    oth = (i + 2) % 3
    valid = tv_ref[i] == 1
    last_valid = jnp.logical_and(valid, jnp.logical_or(i == nt - 1, tv_ref[jnp.minimum(i + 1, nt - 1)] == 0))

    def gather_row(idx_ref, r, b):
        pltpu.make_async_copy(h_hbm.at[pl.ds(idx_ref[0, r], 1), :], hbuf.at[b, pl.ds(r, 1), :],
                              gsem.at[b]).start()

    def scatter_row(idx_ref, r, b):
        pltpu.make_async_copy(acc.at[b, pl.ds(r, 1), :], y_hbm.at[pl.ds(idx_ref[0, r], 1), :],
                              ssem.at[b]).start()

    def wait_rows(buf, sem, b):
        pltpu.make_async_copy(buf.at[b], buf.at[b], sem.at[b]).wait()

    @pl.when(jnp.logical_and(i == 0, f == 0))
    def _():
        acc[...] = jnp.zeros_like(acc)

        def first(r, c):
            gather_row(src0_ref, r, 0)
            gather_row(src1_ref, r, 1)
            return c
        lax.fori_loop(0, rp, first, 0, unroll=8)

    @pl.when(jnp.logical_and(valid, f == 0))
    def _():
        wait_rows(hbuf, gsem, cur)
        hb16[...] = hbuf[cur, 0:tm, :].astype(BF16)

        @pl.when(i >= 2)
        def _():
            wait_rows(acc, ssem, cur)

        acc[cur, 0:tm, :] = jnp.zeros((tm, D_MODEL), F32)

    @pl.when(valid)
    def _():
        for c in range(ch):
            r = f * ch + c
            gather_row(src2_ref, r, oth)
            scatter_row(pdst_ref, r, oth)
        h = hb16[...]
        a = jnp.dot(h, wg_ref[...], preferred_element_type=F32)
        u = jnp.dot(h, wu_ref[...], preferred_element_type=F32)
        acc[cur, 0:tm, :] += jnp.dot((_silu(a) * u).astype(BF16), wd_ref[...], preferred_element_type=F32)

    @pl.when(jnp.logical_and(last_valid, f == nf - 1))
    def _():
        def own(r, c):
            scatter_row(dst_ref, r, cur)
            return c
        lax.fori_loop(0, rp, own, 0, unroll=8)
        @pl.when(i >= 1)
        def _():
            wait_rows(acc, ssem, (i + 1) % 3)

        wait_rows(acc, ssem, oth)
        wait_rows(acc, ssem, cur)
        wait_rows(hbuf, gsem, (i + 1) % 3)
        wait_rows(hbuf, gsem, oth)
        hbuf[0] = jnp.zeros((rp, D_MODEL), F32)
        for b in range(3):
            spare = pltpu.make_async_copy(hbuf.at[0], y_hbm.at[pl.ds(y_hbm.shape[0] - (b + 1) * rp, rp), :],
                                          gsem.at[0])
            spare.start()
            spare.wait()


def _moe_ffn(h2, topi, wg, wu, wd, *, tm, tf):
    t = h2.shape[0]
    dff = wg.shape[2]
    nf = dff // tf
    ch = -(-tm // (nf * SUBLANES)) * SUBLANES
    rp = nf * ch
    src, dst, tile_expert, tile_valid = _route(topi, tm, rp)
    nt = src.shape[0]
    idx_spec = lambda fn: pl.BlockSpec((None, 1, rp), fn, memory_space=pltpu.SMEM)
    fsel = lambda i, f, tv: jnp.where(tv[i] == 1, f, nf - 1)
    return pl.pallas_call(
        functools.partial(_moe_kernel, tm=tm, ch=ch),
        grid_spec=pltpu.PrefetchScalarGridSpec(
            num_scalar_prefetch=2,
            grid=(nt, nf),
            in_specs=[
                idx_spec(lambda i, f, te, tv: (i, 0, 0)),
                idx_spec(lambda i, f, te, tv: (jnp.minimum(i + 1, nt - 1), 0, 0)),
                idx_spec(lambda i, f, te, tv: (jnp.minimum(i + 2, nt - 1), 0, 0)),
                idx_spec(lambda i, f, te, tv: (i, 0, 0)),
                idx_spec(lambda i, f, te, tv: (i + 1, 0, 0)),
                pl.BlockSpec(memory_space=pl.ANY),
                pl.BlockSpec((None, D_MODEL, tf), lambda i, f, te, tv: (te[i], 0, fsel(i, f, tv))),
                pl.BlockSpec((None, D_MODEL, tf), lambda i, f, te, tv: (te[i], 0, fsel(i, f, tv))),
                pl.BlockSpec((None, tf, D_MODEL), lambda i, f, te, tv: (te[i], fsel(i, f, tv), 0)),
            ],
            out_specs=pl.BlockSpec(memory_space=pl.ANY),
            scratch_shapes=[
                pltpu.VMEM((3, rp, D_MODEL), F32),
                pltpu.VMEM((tm, D_MODEL), BF16),
                pltpu.VMEM((3, rp, D_MODEL), F32),
                pltpu.SemaphoreType.DMA((3,)),
                pltpu.SemaphoreType.DMA((3,)),
            ],
        ),
        out_shape=jax.ShapeDtypeStruct((2 * t + 3 * rp, D_MODEL), F32),
        compiler_params=_params("arbitrary", "arbitrary"),
        name="ffn_moe",
    )(tile_expert, tile_valid, src, src, src, dst, dst, h2, wg, wu, wd)


def _combine_kernel(x1_ref, ya_ref, yb_ref, topw_ref, mod_ref, gfin_ref, o_ref):
    w = topw_ref[...]
    f = w[:, 0:1] * ya_ref[...] + w[:, 1:2] * yb_ref[...]
    o_ref[...] = _rms(x1_ref[...] + mod_ref[...][5:6, :] * f, gfin_ref[...])


def _combine(x1, y2, topw, mod, gfin, *, seq, tm):
    t = x1.shape[0]
    nt = seq // tm
    nb = t // tm
    row = lambda i: (i, 0)
    return pl.pallas_call(
        _combine_kernel,
        grid=(nb,),
        in_specs=[
            pl.BlockSpec((tm, D_MODEL), row),
            pl.BlockSpec((tm, D_MODEL), row),
            pl.BlockSpec((tm, D_MODEL), lambda i: (nb + i, 0)),
            pl.BlockSpec((tm, LANES), row),
            pl.BlockSpec((None, 6, D_MODEL), lambda i: (i // nt, 0, 0)),
            _const_spec((1, D_MODEL)),
        ],
        out_specs=pl.BlockSpec((tm, D_MODEL), row),
        out_shape=jax.ShapeDtypeStruct((t, D_MODEL), F32),
        compiler_params=_params("parallel"),
        name="moe_combine",
    )(x1, y2, y2, topw, mod, gfin)


def _rope_tables(seq):
    pos = jnp.arange(seq, dtype=F32)
    inv = 1.0 / (ROPE_THETA ** (jnp.arange(0, MLA_DR, 2, dtype=F32) / MLA_DR))
    ang = pos[:, None] * inv[None, :]
    cos, sin = jnp.cos(ang), jnp.sin(ang)
    zero = jnp.zeros((seq, LANES - MLA_DR), F32)
    return (jnp.concatenate([cos, cos, zero], axis=1), jnp.concatenate([-sin, sin, zero], axis=1))


def _t5_bucket(rel):
    nb = NUM_BUCKETS // 2
    ret = (rel > 0).astype(jnp.int32) * nb
    n = jnp.abs(rel)
    max_exact = nb // 2
    nf = jnp.maximum(n, 1).astype(F32)
    large = max_exact + (jnp.log(nf / max_exact) / math.log(MAX_DISTANCE / max_exact)
                         * (nb - max_exact)).astype(jnp.int32)
    large = jnp.minimum(large, nb - 1)
    return ret + jnp.where(n < max_exact, n, large)


def _window_bias(rel_bias):
    qpos = jnp.arange(WINDOW, dtype=jnp.int32)
    jpos = jnp.arange(3 * WINDOW, dtype=jnp.int32)
    rel = jpos[None, :] - WINDOW - qpos[:, None]
    bucket = _t5_bucket(rel)
    bias = jnp.zeros((GQA_HEADS, WINDOW, 3 * WINDOW), F32)
    for b in range(NUM_BUCKETS):
        bias = bias + jnp.where((bucket == b)[None], rel_bias[b].astype(F32)[:, None, None], 0.0)
    bias = jnp.where((jnp.abs(rel) <= WINDOW)[None], bias, NEG_BIG)
    return bias.reshape(GQA_KV_HEADS, GQA_GROUP * WINDOW, 3 * WINDOW)


def _prep_layer(w_in, w_uq, w_ukv, w_out):
    half = MLA_DR // 2
    zpad = jnp.zeros((D_MODEL, LANES - MLA_DR), F32)
    kr0 = Q_LORA + KV_LORA
    k1 = w_in[:, kr0:kr0 + half]
    k2 = w_in[:, kr0 + half:kr0 + MLA_DR]
    w_in_p = jnp.concatenate(
        [w_in[:, :kr0], k1, k2, zpad, k2, k1, zpad, w_in[:, kr0 + MLA_DR:]], axis=1).astype(BF16)
    wq = w_uq.reshape(Q_LORA, MLA_HEADS, MLA_DN + MLA_DR)
    r1 = wq[:, :, MLA_DN:MLA_DN + half]
    r2 = wq[:, :, MLA_DN + half:]
    zq = jnp.zeros((Q_LORA, MLA_HEADS, LANES - MLA_DR), F32)
    w_uq_a = jnp.concatenate([wq[:, :, :MLA_DN], r1, r2, zq], axis=2).reshape(Q_LORA, MLA_HEADS * QK_PAD)
    w_uq_b = jnp.concatenate([r2, r1, zq], axis=2).reshape(Q_LORA, MLA_HEADS * LANES)
    out_a = MLA_HEADS * MLA_DV
    return (w_in_p, w_uq_a.astype(BF16), w_uq_b.astype(BF16), w_ukv.astype(BF16),
            w_out[:out_a].astype(BF16), w_out[out_a:].astype(BF16))


def _trunk(x, mods, layers, bias, g_final, *, batch, seq):
    tm = min(ROW_TILE, seq)
    cos_t, sin_t = _rope_tables(seq)
    for l, p in enumerate(layers):
        tk = min(MLA_TK, tm)
        q, k, v, gq, gk, gv = _premix(x, mods[l], p["g_norm_mix"], cos_t, sin_t, p["w_in_p"], p["g_q_lat"],
                                      p["w_uq_a"], p["w_uq_b"], p["g_kv_lat"], p["w_ukv"], seq=seq, tm=tm,
                                      vchunk=tk)
        tq_sub = min(MLA_TQ_SUB, seq)
        tq = min(MLA_TQ, seq)
        oa = _mla(q, k, v, batch=batch, seq=seq, tq=tq, tk=tk, nsub=tq // tq_sub)
        ob = _window(gq, gk, gv, bias, p["sink_col"], seq=seq, rows=min(ROW_TILE, seq))
        res = _postmix(oa, ob, x, mods[l], p["g_out_a"], p["g_out_b"], p["w_out_a"], p["w_out_b"],
                       p["g_norm_ffn"], p.get("w_router"), seq=seq, tm=tm)
        if "w_router" in p:
            assert l == len(layers) - 1
            x1, h2, topw, topi = res
            y2 = _moe_ffn(h2, topi[:, :2], p["w_gate"], p["w_up"], p["w_down"], tm=ROW_TILE, tf=FF_TILE)
            x = _combine(x1, y2, topw, mods[l], g_final, seq=seq, tm=tm)
        else:
            x1, h2 = res
            x = _ffn(x1, h2, mods[l], p["w_gate"], p["w_up"], p["w_down"], seq=seq, tm=tm, tf=FF_TILE)
    return x


def kernel(x_prompt, x_sample, c_prompt, c_sample, rel_bias, w_ada, b_ada, g_norm_mix, g_norm_ffn, w_in, g_q_lat, w_uq, g_kv_lat, w_ukv, sink, g_out_a, g_out_b, w_out, w_gate_d, w_up_d, w_down_d, w_router, w_gate_e, w_up_e, w_down_e, g_final):
    bp, sp, _ = x_prompt.shape
    bs, ss, _ = x_sample.shape
    rows = -(-(bp + bs) // 8) * 8
    c_all = jnp.concatenate([c_prompt, c_sample, jnp.zeros((rows - bp - bs, D_MODEL), F32)], axis=0)
    mod = _ada_mod(c_all, w_ada, b_ada)
    mod_p = [mod[l, :bp].reshape(bp, 6, D_MODEL) for l in range(DEPTH)]
    mod_s = [mod[l, bp:bp + bs].reshape(bs, 6, D_MODEL) for l in range(DEPTH)]
    bias = _window_bias(rel_bias)
    layers = []
    for l in range(DEPTH):
        w_in_p, w_uq_a, w_uq_b, w_ukv_b, w_out_a, w_out_b = _prep_layer(w_in[l], w_uq[l], w_ukv[l], w_out[l])
        out_a = MLA_HEADS * MLA_DV
        p = dict(
            g_norm_mix=g_norm_mix[l][None], g_norm_ffn=g_norm_ffn[l][None],
            w_in_p=w_in_p, g_q_lat=g_q_lat[l][None], w_uq_a=w_uq_a, w_uq_b=w_uq_b,
            g_kv_lat=g_kv_lat[l][None], w_ukv=w_ukv_b,
            sink_col=jnp.repeat(sink[l].astype(F32), WINDOW).reshape(GQA_KV_HEADS, GQA_GROUP * WINDOW, 1),
            g_out_a=g_out_a[l][None], g_out_b=g_out_b[l][None], w_out_a=w_out_a, w_out_b=w_out_b,
        )
        i = l // 2
        if l % 2 == 0:
            p.update(w_gate=w_gate_d[i].astype(BF16), w_up=w_up_d[i].astype(BF16), w_down=w_down_d[i].astype(BF16))
        else:
            wr = jnp.concatenate([w_router[i], jnp.zeros((D_MODEL, LANES - N_EXPERTS), F32)], axis=1)
            wr_hi = wr.astype(BF16)
            wr = jnp.stack([wr_hi, (wr - wr_hi.astype(F32)).astype(BF16)])
            p.update(w_gate=w_gate_e[i].astype(BF16), w_up=w_up_e[i].astype(BF16),
                     w_down=w_down_e[i].astype(BF16), w_router=wr)
        layers.append(p)
    gfin = g_final[None]
    y_s = _trunk(x_sample.reshape(bs * ss, D_MODEL), mod_s, layers, bias, gfin, batch=bs, seq=ss)
    y_p = _trunk(x_prompt.reshape(bp * sp, D_MODEL), mod_p, layers, bias, gfin, batch=bp, seq=sp)
    return (y_p.reshape(bp, sp, D_MODEL), y_s.reshape(bs, ss, D_MODEL))
```

```python
import functools
import math

import jax
import jax.numpy as jnp
from jax import lax
from jax.experimental import pallas as pl
from jax.experimental.pallas import tpu as pltpu

F32 = jnp.float32
BF16 = jnp.bfloat16

D_MODEL = 2048
DEPTH = 2
EPS = 1e-6
MLA_HEADS = 8
MLA_DN = 128
MLA_DR = 64
MLA_DV = 128
Q_LORA = 512
KV_LORA = 512
ROPE_THETA = 10000.0
MLA_SCALE = 1.0 / math.sqrt(MLA_DN + MLA_DR)
MLA_QSCALE = MLA_SCALE * math.log2(math.e)
GQA_HEADS = 8
GQA_KV_HEADS = 2
GQA_GROUP = GQA_HEADS // GQA_KV_HEADS
GQA_DH = 128
WINDOW = 128
GQA_SCALE = 1.0 / math.sqrt(GQA_DH)
NUM_BUCKETS = 32
MAX_DISTANCE = 128
N_EXPERTS = 8
NEG_BIG = -1e30

LANES = 128
SUBLANES = 8
BF16_SUBLANES = 16
V7X_VMEM_LIMIT = 56 * 1024 * 1024

QK_PAD = 2 * LANES
VT_ROWS = MLA_DV + BF16_SUBLANES
MLA_TQ = 2048
MLA_TQ_SUB = 256
MLA_TK = 512
ROW_TILE = 512
FF_TILE = 512
MOE_DEPTH = 3
ADA_TILE = 1024
Z_CQ = 0
Z_CKV = Z_CQ + Q_LORA
Z_KRA = Z_CKV + KV_LORA
Z_KRB = Z_KRA + LANES
Z_GQ = Z_KRB + LANES
Z_GK = Z_GQ + GQA_HEADS * GQA_DH
Z_GV = Z_GK + GQA_KV_HEADS * GQA_DH
Z_END = Z_GV + GQA_KV_HEADS * GQA_DH


def _params(*sem):
    return pltpu.CompilerParams(dimension_semantics=sem, vmem_limit_bytes=V7X_VMEM_LIMIT)


def _rms(x, g):
    return x * lax.rsqrt(jnp.mean(x * x, axis=-1, keepdims=True) + EPS) * g


def _silu(a):
    return a * (1.0 / (1.0 + jnp.exp(-a)))


def _const_spec(shape):
    nd = len(shape)
    return pl.BlockSpec(shape, lambda *_: (0,) * nd)


def _ada_kernel(c_ref, w_ref, b_ref, o_ref):
    cs = _silu(c_ref[...])
    o_ref[...] = jnp.dot(cs, w_ref[...], preferred_element_type=F32,
                         precision=lax.Precision.HIGHEST) + b_ref[...]


def _ada_mod(c_all, w_ada, b_ada):
    rows = c_all.shape[0]
    n = w_ada.shape[-1]
    tn = ADA_TILE
    return pl.pallas_call(
        _ada_kernel,
        grid=(DEPTH, n // tn),
        in_specs=[
            pl.BlockSpec((rows, D_MODEL), lambda l, j: (0, 0)),
            pl.BlockSpec((None, D_MODEL, tn), lambda l, j: (l, 0, j)),
            pl.BlockSpec((None, 1, tn), lambda l, j: (l, 0, j)),
        ],
        out_specs=pl.BlockSpec((None, rows, tn), lambda l, j: (l, 0, j)),
        out_shape=jax.ShapeDtypeStruct((DEPTH, rows, n), F32),
        compiler_params=_params("parallel", "parallel"),
        name="ada_mod",
    )(c_all, w_ada, b_ada.reshape(DEPTH, 1, n))


def _premix_kernel(x_ref, mod_ref, gn_ref, cos_ref, sin_ref, win_ref, gq_ref, wuqa_ref, wuqb_ref,
                   gkv_ref, wukv_ref, q_ref, k_ref, v_ref, wq_ref, wk_ref, wv_ref):
    mod = mod_ref[...]
    h = _rms(x_ref[...], gn_ref[...]) * (1.0 + mod[1:2, :]) + mod[0:1, :]
    z = jnp.dot(h.astype(BF16), win_ref[...], preferred_element_type=F32)
    cqn = _rms(z[:, Z_CQ:Z_CKV], gq_ref[...]).astype(BF16)
    ckvn = _rms(z[:, Z_CKV:Z_KRA], gkv_ref[...]).astype(BF16)
    cos_t = cos_ref[...]
    sin_t = sin_ref[...]
    kr = (z[:, Z_KRA:Z_KRB] * cos_t + z[:, Z_KRB:Z_GQ] * sin_t).astype(BF16)
    wq_ref[...] = (z[:, Z_GQ:Z_GK] * GQA_SCALE).astype(BF16)
    wk_ref[...] = z[:, Z_GK:Z_GV].astype(BF16)
    wv_ref[...] = z[:, Z_GV:Z_END].astype(BF16)
    qa = jnp.dot(cqn, wuqa_ref[...], preferred_element_type=F32)
    qb = jnp.dot(cqn, wuqb_ref[...], preferred_element_type=F32)
    kv = jnp.dot(ckvn, wukv_ref[...], preferred_element_type=F32)
    vchunk = v_ref.shape[3]
    sub = lax.broadcasted_iota(jnp.int32, (VT_ROWS - MLA_DV, vchunk), 0)
    ones_row = jnp.where(sub == 0, 1.0, 0.0).astype(BF16)
    for hh in range(MLA_HEADS):
        a0 = hh * QK_PAD
        q_ref[hh, :, 0:LANES] = (qa[:, a0:a0 + LANES] * MLA_QSCALE).astype(BF16)
        q_rope = qa[:, a0 + LANES:a0 + QK_PAD] * cos_t + qb[:, hh * LANES:(hh + 1) * LANES] * sin_t
        q_ref[hh, :, LANES:QK_PAD] = (q_rope * MLA_QSCALE).astype(BF16)
        k_ref[hh, :, 0:LANES] = kv[:, a0:a0 + LANES].astype(BF16)
        k_ref[hh, :, LANES:QK_PAD] = kr
        vt = kv[:, a0 + LANES:a0 + QK_PAD].T.astype(BF16)
        for c in range(v_ref.shape[1]):
            v_ref[hh, c, 0:MLA_DV, :] = vt[:, c * vchunk:(c + 1) * vchunk]
            v_ref[hh, c, MLA_DV:VT_ROWS, :] = ones_row


def _premix(x, mod, gn, cos_t, sin_t, w_in_p, g_q, w_uq_a, w_uq_b, g_kv, w_ukv, *, seq, tm, vchunk):
    t = x.shape[0]
    nt = seq // tm
    cpt = tm // vchunk
    row = lambda i: (i, 0)
    hrow = lambda i: (0, i, 0)
    return pl.pallas_call(
        _premix_kernel,
        grid=(t // tm,),
        in_specs=[
            pl.BlockSpec((tm, D_MODEL), row),
            pl.BlockSpec((None, 6, D_MODEL), lambda i: (i // nt, 0, 0)),
            _const_spec((1, D_MODEL)),
            pl.BlockSpec((tm, LANES), lambda i: (i % nt, 0)),
            pl.BlockSpec((tm, LANES), lambda i: (i % nt, 0)),
            _const_spec(w_in_p.shape),
            _const_spec((1, Q_LORA)),
            _const_spec(w_uq_a.shape),
            _const_spec(w_uq_b.shape),
            _const_spec((1, KV_LORA)),
            _const_spec(w_ukv.shape),
        ],
        out_specs=[
            pl.BlockSpec((MLA_HEADS, tm, QK_PAD), hrow),
            pl.BlockSpec((MLA_HEADS, tm, QK_PAD), hrow),
            pl.BlockSpec((MLA_HEADS, cpt, VT_ROWS, vchunk), lambda i: (0, i, 0, 0)),
            pl.BlockSpec((tm, GQA_HEADS * GQA_DH), row),
            pl.BlockSpec((tm, GQA_KV_HEADS * GQA_DH), row),
            pl.BlockSpec((tm, GQA_KV_HEADS * GQA_DH), row),
        ],
        out_shape=[
            jax.ShapeDtypeStruct((MLA_HEADS, t, QK_PAD), BF16),
            jax.ShapeDtypeStruct((MLA_HEADS, t, QK_PAD), BF16),
            jax.ShapeDtypeStruct((MLA_HEADS, t // vchunk, VT_ROWS, vchunk), BF16),
            jax.ShapeDtypeStruct((t, GQA_HEADS * GQA_DH), BF16),
            jax.ShapeDtypeStruct((t, GQA_KV_HEADS * GQA_DH), BF16),
            jax.ShapeDtypeStruct((t, GQA_KV_HEADS * GQA_DH), BF16),
        ],
        compiler_params=_params("parallel"),
        name="premix",
    )(x, mod, gn, cos_t, sin_t, w_in_p, g_q, w_uq_a, w_uq_b, g_kv, w_ukv)


def _mla_kernel(q_ref, k_ref, v_ref, o_ref, qt_scr, s_scr, p_scr, m_scr, a_scr, acc_scr, pm_scr, *, tk, nk, nsub):
    tq = q_ref.shape[0] // nsub
    subs = range(nsub)
    for i in subs:
        qt_scr[i] = q_ref[i * tq:(i + 1) * tq, :].astype(F32).T.astype(BF16)

    def scores(j, buf, i):
        k = k_ref[pl.ds(pl.multiple_of(j * tk, tk), tk), :]
        s = jnp.dot(k, qt_scr[i], preferred_element_type=F32)
        s_scr[buf, i] = s
        pm = s[0:SUBLANES, :]
        for g in range(1, tk // SUBLANES):
            pm = jnp.maximum(pm, s[g * SUBLANES:(g + 1) * SUBLANES, :])
        pm_scr[buf, i] = pm

    def smax(buf, i):
        m_old = m_scr[i]
        m_new = jnp.maximum(m_old, jnp.max(pm_scr[buf, i], axis=0, keepdims=True))
        m_scr[i] = m_new
        a_scr[buf, i] = jnp.exp2(m_old - m_new)

    def sexp(buf, i):
        p_scr[buf, i] = jnp.exp2(s_scr[buf, i] - m_scr[i]).astype(BF16)

    def softmax(buf, i):
        smax(buf, i)
        sexp(buf, i)

    def values(j, buf, i):
        v = v_ref[j]
        pv = jnp.dot(v, p_scr[buf, i], preferred_element_type=F32)
        acc_scr[i] = a_scr[buf, i] * acc_scr[i] + pv

    m_scr[...] = jnp.full(m_scr.shape, -jnp.inf, F32)
    acc_scr[...] = jnp.zeros(acc_scr.shape, F32)
    for i in subs:
        scores(0, 0, i)
        scores(1, 1, i)
        softmax(0, i)

    def body(u, carry):
        t = 2 * u + 1
        for i in subs:
            smax(1, i)
            scores(t + 1, 0, i)
            values(t - 1, 0, i)
            sexp(1, i)
        for i in subs:
            smax(0, i)
            scores(t + 2, 1, i)
            values(t, 1, i)
            sexp(0, i)
        return carry

    lax.fori_loop(0, (nk - 2) // 2, body, 0)
    for i in subs:
        softmax(1, i)
        values(nk - 2, 0, i)
        values(nk - 1, 1, i)
    for i in subs:
        acc = acc_scr[i]
        o_ref[i * tq:(i + 1) * tq, :] = (acc[0:MLA_DV, :] * (1.0 / acc[MLA_DV:MLA_DV + 1, :])).T.astype(BF16)


def _mla(q, k, v, *, batch, seq, tq, tk, nsub):
    t = batch * seq
    nq = seq // tq
    nk = seq // tk
    assert nk >= 2 and nk % 2 == 0, (seq, tk)
    sub = tq // nsub
    return pl.pallas_call(
        functools.partial(_mla_kernel, tk=tk, nk=nk, nsub=nsub),
        grid=(batch, MLA_HEADS, nq),
        in_specs=[
            pl.BlockSpec((None, tq, QK_PAD), lambda b, h, i: (h, b * nq + i, 0)),
            pl.BlockSpec((None, seq, QK_PAD), lambda b, h, i: (h, b, 0)),
            pl.BlockSpec((None, nk, VT_ROWS, tk), lambda b, h, i: (h, b, 0, 0)),
        ],
        out_specs=pl.BlockSpec((tq, MLA_DV), lambda b, h, i: (b * nq + i, h)),
        out_shape=jax.ShapeDtypeStruct((t, MLA_HEADS * MLA_DV), BF16),
        scratch_shapes=[
            pltpu.VMEM((nsub, QK_PAD, sub), BF16),
            pltpu.VMEM((2, nsub, tk, sub), F32),
            pltpu.VMEM((2, nsub, tk, sub), BF16),
            pltpu.VMEM((nsub, 1, sub), F32),
            pltpu.VMEM((2, nsub, 1, sub), F32),
            pltpu.VMEM((nsub, VT_ROWS, sub), F32),
            pltpu.VMEM((2, nsub, SUBLANES, sub), F32),
        ],
        compiler_params=_params("parallel", "parallel", "parallel"),
        name="mla_attn",
    )(q, k, v)


def _window_kernel(q_ref, kp_ref, kc_ref, kn_ref, vp_ref, vc_ref, vn_ref, bias_ref, sink_ref, o_ref,
                   kcat, vcat, *, rows, nt):
    i = pl.program_id(0)
    nsub = rows // WINDOW
    kcat[0:WINDOW, :] = kp_ref[...]
    kcat[WINDOW:WINDOW + rows, :] = kc_ref[...]
    kcat[WINDOW + rows:, :] = kn_ref[...]
    vcat[0:WINDOW, :] = vp_ref[...]
    vcat[WINDOW:WINDOW + rows, :] = vc_ref[...]
    vcat[WINDOW + rows:, :] = vn_ref[...]
    first_tile = (i % nt) == 0
    last_tile = (i % nt) == nt - 1
    col = lax.broadcasted_iota(jnp.int32, (1, 3 * WINDOW), 1)

    chains = [(n, kh) for n in range(nsub) for kh in range(GQA_KV_HEADS)]
    s_all = []
    for n, kh in chains:
        r0 = n * WINDOW
        qall = q_ref[r0:r0 + WINDOW, kh * GQA_GROUP * GQA_DH:(kh + 1) * GQA_GROUP * GQA_DH]
        qs = jnp.concatenate([qall[:, g * GQA_DH:(g + 1) * GQA_DH] for g in range(GQA_GROUP)], axis=0)
        kk = kcat[r0:r0 + 3 * WINDOW, kh * GQA_DH:(kh + 1) * GQA_DH]
        s = lax.dot_general(qs, kk, (((1,), (1,)), ((), ())), preferred_element_type=F32) + bias_ref[kh]
        if n == 0:
            s = s + jnp.where(jnp.logical_and(first_tile, col < WINDOW), NEG_BIG, 0.0)
        if n == nsub - 1:
            s = s + jnp.where(jnp.logical_and(last_tile, col >= 2 * WINDOW), NEG_BIG, 0.0)
        s_all.append(s)
    m_all = [jnp.maximum(jnp.max(s, axis=1, keepdims=True), sink_ref[kh]) for s, (n, kh) in zip(s_all, chains)]
    e_all = [jnp.exp(s - m) for s, m in zip(s_all, m_all)]
    d_all = [jnp.sum(e, axis=1, keepdims=True) + jnp.exp(sink_ref[kh] - m)
             for e, m, (n, kh) in zip(e_all, m_all, chains)]
    for e, d, (n, kh) in zip(e_all, d_all, chains):
        r0 = n * WINDOW
        vv = vcat[r0:r0 + 3 * WINDOW, kh * GQA_DH:(kh + 1) * GQA_DH]
        o = jnp.dot(e.astype(BF16), vv, preferred_element_type=F32) * (1.0 / d)
        for g in range(GQA_GROUP):
            hcol = (kh * GQA_GROUP + g) * GQA_DH
            o_ref[r0:r0 + WINDOW, hcol:hcol + GQA_DH] = o[g * WINDOW:(g + 1) * WINDOW].astype(BF16)


def _window(gq, gk, gv, bias, sink_col, *, seq, rows):
    t = gq.shape[0]
    nt = seq // rows
    rb = rows // WINDOW
    nblk = t // WINDOW
    kvw = GQA_KV_HEADS * GQA_DH
    prev = lambda i: (jnp.maximum(i * rb - 1, 0), 0)
    nxt = lambda i: (jnp.minimum((i + 1) * rb, nblk - 1), 0)
    cur = lambda i: (i, 0)
    return pl.pallas_call(
        functools.partial(_window_kernel, rows=rows, nt=nt),
        grid=(t // rows,),
        in_specs=[
            pl.BlockSpec((rows, GQA_HEADS * GQA_DH), cur),
            pl.BlockSpec((WINDOW, kvw), prev),
            pl.BlockSpec((rows, kvw), cur),
            pl.BlockSpec((WINDOW, kvw), nxt),
            pl.BlockSpec((WINDOW, kvw), prev),
            pl.BlockSpec((rows, kvw), cur),
            pl.BlockSpec((WINDOW, kvw), nxt),
            _const_spec(bias.shape),
            _const_spec(sink_col.shape),
        ],
        out_specs=pl.BlockSpec((rows, GQA_HEADS * GQA_DH), cur),
        out_shape=jax.ShapeDtypeStruct((t, GQA_HEADS * GQA_DH), BF16),
        scratch_shapes=[pltpu.VMEM((rows + 2 * WINDOW, kvw), BF16),
                        pltpu.VMEM((rows + 2 * WINDOW, kvw), BF16)],
        compiler_params=_params("parallel"),
        name="window_attn",
    )(gq, gk, gk, gk, gv, gv, gv, bias, sink_col)


def _router(h2, wr_ref, topw_ref, topi_ref):
    h_hi = h2.astype(BF16)
    h_lo = (h2 - h_hi.astype(F32)).astype(BF16)
    logits = (jnp.dot(h_hi, wr_ref[0], preferred_element_type=F32)
              + jnp.dot(h_lo, wr_ref[0], preferred_element_type=F32)
              + jnp.dot(h_hi, wr_ref[1], preferred_element_type=F32))
    lane = lax.broadcasted_iota(jnp.int32, logits.shape, 1)
    lg = jnp.where(lane < N_EXPERTS, logits, -jnp.inf)
    m1 = jnp.max(lg, axis=1, keepdims=True)
    i1 = jnp.min(jnp.where(lg == m1, lane, LANES), axis=1, keepdims=True)
    lg2 = jnp.where(lane == i1, -jnp.inf, lg)
    m2 = jnp.max(lg2, axis=1, keepdims=True)
    i2 = jnp.min(jnp.where(lg2 == m2, lane, LANES), axis=1, keepdims=True)
    e2 = jnp.exp(m2 - m1)
    w1 = 1.0 / (1.0 + e2)
    w2 = e2 * w1
    topw_ref[...] = jnp.where(lane == 0, w1, jnp.where(lane == 1, w2, 0.0))
    topi_ref[...] = jnp.where(lane == 0, i1, jnp.where(lane == 1, i2, 0))


def _postmix_kernel(oa_ref, ob_ref, x_ref, mod_ref, ga_ref, gb_ref, wa_ref, wb_ref, gf_ref, *rest, moe):
    if moe:
        wr_ref, x1_ref, h2_ref, topw_ref, topi_ref, hprev = rest

        @pl.when(pl.program_id(0) == 0)
        def _():
            hprev[...] = jnp.zeros_like(hprev)

        _router(hprev[...], wr_ref, topw_ref, topi_ref)
    else:
        x1_ref, h2_ref = rest
    mod = mod_ref[...]
    na = _rms(oa_ref[...].astype(F32), ga_ref[...]).astype(BF16)
    nb = _rms(ob_ref[...].astype(F32), gb_ref[...]).astype(BF16)
    mix = (jnp.dot(na, wa_ref[...], preferred_element_type=F32)
           + jnp.dot(nb, wb_ref[...], preferred_element_type=F32))
    x1 = x_ref[...] + mod[2:3, :] * mix
    x1_ref[...] = x1
    h2 = _rms(x1, gf_ref[...]) * (1.0 + mod[4:5, :]) + mod[3:4, :]
    h2_ref[...] = h2.astype(h2_ref.dtype)
    if moe:
        hprev[...] = h2


def _postmix(oa, ob, x, mod, ga, gb, wa, wb, gf, wr, *, seq, tm):
    t = x.shape[0]
    nt = seq // tm
    nb = t // tm
    moe = wr is not None
    row = (lambda i: (jnp.minimum(i, nb - 1), 0)) if moe else (lambda i: (i, 0))
    lag = lambda i: (jnp.maximum(i - 1, 0), 0)
    half = oa.shape[1]
    in_specs = [
        pl.BlockSpec((tm, half), row),
        pl.BlockSpec((tm, half), row),
        pl.BlockSpec((tm, D_MODEL), row),
        pl.BlockSpec((None, 6, D_MODEL), lambda i: (row(i)[0] // nt, 0, 0)),
        _const_spec((1, half)),
        _const_spec((1, half)),
        _const_spec(wa.shape),
        _const_spec(wb.shape),
        _const_spec((1, D_MODEL)),
    ]
    out_specs = [pl.BlockSpec((tm, D_MODEL), row), pl.BlockSpec((tm, D_MODEL), row)]
    out_shape = [jax.ShapeDtypeStruct((t, D_MODEL), F32), jax.ShapeDtypeStruct((t, D_MODEL), F32 if moe else BF16)]
    args = [oa, ob, x, mod, ga, gb, wa, wb, gf]
    scratch = []
    if moe:
        in_specs.append(_const_spec(wr.shape))
        out_specs += [pl.BlockSpec((tm, LANES), lag), pl.BlockSpec((tm, LANES), lag)]
        out_shape += [jax.ShapeDtypeStruct((t, LANES), F32), jax.ShapeDtypeStruct((t, LANES), jnp.int32)]
        args.append(wr)
        scratch.append(pltpu.VMEM((tm, D_MODEL), F32))
    return pl.pallas_call(
        functools.partial(_postmix_kernel, moe=moe),
        grid=(nb + 1 if moe else nb,),
        in_specs=in_specs,
        out_specs=out_specs,
        out_shape=out_shape,
        scratch_shapes=scratch,
        compiler_params=_params("arbitrary" if moe else "parallel"),
        name="postmix_moe" if moe else "postmix",
    )(*args)


def _ffn_kernel(x1_ref, h_ref, mod_ref, wg_ref, wu_ref, wd_ref, o_ref, acc_ref):
    f = pl.program_id(1)

    @pl.when(f == 0)
    def _():
        acc_ref[...] = jnp.zeros_like(acc_ref)

    h = h_ref[...]
    a = jnp.dot(h, wg_ref[...], preferred_element_type=F32)
    u = jnp.dot(h, wu_ref[...], preferred_element_type=F32)
    acc_ref[...] += jnp.dot((_silu(a) * u).astype(BF16), wd_ref[...], preferred_element_type=F32)

    @pl.when(f == pl.num_programs(1) - 1)
    def _():
        o_ref[...] = x1_ref[...] + mod_ref[...][5:6, :] * acc_ref[...]


def _ffn(x1, h2, mod, wg, wu, wd, *, seq, tm, tf):
    t = x1.shape[0]
    nt = seq // tm
    dff = wg.shape[1]
    row = lambda i, f: (i, 0)
    return pl.pallas_call(
        _ffn_kernel,
        grid=(t // tm, dff // tf),
        in_specs=[
            pl.BlockSpec((tm, D_MODEL), row),
            pl.BlockSpec((tm, D_MODEL), row),
            pl.BlockSpec((None, 6, D_MODEL), lambda i, f: (i // nt, 0, 0)),
            pl.BlockSpec((D_MODEL, tf), lambda i, f: (0, f)),
            pl.BlockSpec((D_MODEL, tf), lambda i, f: (0, f)),
            pl.BlockSpec((tf, D_MODEL), lambda i, f: (f, 0)),
        ],
        out_specs=pl.BlockSpec((tm, D_MODEL), row),
        out_shape=jax.ShapeDtypeStruct((t, D_MODEL), F32),
        scratch_shapes=[pltpu.VMEM((tm, D_MODEL), F32)],
        compiler_params=_params("parallel", "arbitrary"),
        name="ffn_dense",
    )(x1, h2, mod, wg, wu, wd)


def _route(topi, tm, rp):
    t = topi.shape[0]
    na = 2 * t
    nt = na // tm + N_EXPERTS
    flat_e = topi.reshape(na)
    onehot = (flat_e[:, None] == jnp.arange(N_EXPERTS, dtype=jnp.int32)[None, :]).astype(jnp.int32)
    csum = jnp.cumsum(onehot, axis=0)
    rank = jnp.sum((csum - onehot) * onehot, axis=1)
    padded = ((csum[-1] + tm - 1) // tm) * tm
    ends = jnp.cumsum(padded)
    pos = (ends - padded)[flat_e] + rank
    pair = jnp.full((nt * tm,), -1, jnp.int32).at[pos].set(jnp.arange(na, dtype=jnp.int32))
    pair = jnp.pad(pair.reshape(nt, tm), ((0, 0), (0, rp - tm)), constant_values=-1)
    spare = (na + (jnp.arange(nt + 1, dtype=jnp.int32)[:, None] % MOE_DEPTH) * rp
             + jnp.arange(rp, dtype=jnp.int32)[None, :])
    src = jnp.where(pair >= 0, pair // 2, 0)
    dst = jnp.where(pair >= 0, (pair % 2) * t + pair // 2, spare[1:])
    dst = jnp.concatenate([spare[:1], dst], axis=0)
    tile_start = jnp.arange(nt, dtype=jnp.int32) * tm
    tile_valid = (tile_start < ends[-1]).astype(jnp.int32)
    tile_expert = jnp.minimum(jnp.sum((tile_start[:, None] >= ends[None, :]).astype(jnp.int32), axis=1),
                              N_EXPERTS - 1)
    return src.reshape(nt, 1, rp), dst.reshape(nt + 1, 1, rp), tile_expert, tile_valid


def _moe_kernel(te_ref, tv_ref, src0_ref, src1_ref, src2_ref, pdst_ref, dst_ref, h_hbm, wg_ref, wu_ref, wd_ref,
                y_hbm, hbuf, hb16, acc, gsem, ssem, *, tm, ch):
    i = pl.program_id(0)
    f = pl.program_id(1)
    nt = pl.num_programs(0)
    nf = pl.num_programs(1)
    rp = nf * ch
    cur = i % MOE_DEPTH
    nxt = (i + 1) % MOE_DEPTH
    oth = (i + 2) % MOE_DEPTH
    valid = tv_ref[i] == 1
    last_valid = jnp.logical_and(valid, jnp.logical_or(i == nt - 1, tv_ref[jnp.minimum(i + 1, nt - 1)] == 0))

    def gather_row(idx_ref, r, b):
        pltpu.make_async_copy(h_hbm.at[pl.ds(idx_ref[0, r], 1), :], hbuf.at[b, pl.ds(r, 1), :],
                              gsem.at[b]).start()

    def scatter_row(idx_ref, r, b):
        pltpu.make_async_copy(acc.at[b, pl.ds(r, 1), :], y_hbm.at[pl.ds(idx_ref[0, r], 1), :],
                              ssem.at[b]).start()

    def wait_rows(buf, sem, b):
        pltpu.make_async_copy(buf.at[b], buf.at[b], sem.at[b]).wait()

    @pl.when(jnp.logical_and(i == 0, f == 0))
    def _():
        acc[...] = jnp.zeros_like(acc)

        def first(r, c):
            gather_row(src0_ref, r, 0)
            gather_row(src1_ref, r, 1)
            return c
        lax.fori_loop(0, rp, first, 0, unroll=8)

    @pl.when(jnp.logical_and(valid, f == 0))
    def _():
        wait_rows(hbuf, gsem, cur)
        hb16[...] = hbuf[cur, 0:tm, :].astype(BF16)

        @pl.when(i >= 2)
        def _():
            wait_rows(acc, ssem, cur)

        acc[cur, 0:tm, :] = jnp.zeros((tm, D_MODEL), F32)

    @pl.when(valid)
    def _():
        for c in range(ch):
            r = f * ch + c
            gather_row(src2_ref, r, oth)
            scatter_row(pdst_ref, r, oth)
        h = hb16[...]
        a = jnp.dot(h, wg_ref[...], preferred_element_type=F32)
        u = jnp.dot(h, wu_ref[...], preferred_element_type=F32)
        acc[cur, 0:tm, :] += jnp.dot((_silu(a) * u).astype(BF16), wd_ref[...], preferred_element_type=F32)

    @pl.when(jnp.logical_and(last_valid, f == nf - 1))
    def _():
        def own(r, c):
            scatter_row(dst_ref, r, cur)
            return c
        lax.fori_loop(0, rp, own, 0, unroll=8)
        @pl.when(i >= 1)
        def _():
            wait_rows(acc, ssem, nxt)

        wait_rows(acc, ssem, oth)
        wait_rows(acc, ssem, cur)
        wait_rows(hbuf, gsem, nxt)
        wait_rows(hbuf, gsem, oth)
        hbuf[0] = jnp.zeros((rp, D_MODEL), F32)
        for b in range(MOE_DEPTH):
            spare = pltpu.make_async_copy(hbuf.at[0], y_hbm.at[pl.ds(y_hbm.shape[0] - (b + 1) * rp, rp), :],
                                          gsem.at[0])
            spare.start()
            spare.wait()


def _moe_ffn(h2, topi, wg, wu, wd, *, tm, tf):
    t = h2.shape[0]
    dff = wg.shape[2]
    nf = dff // tf
    ch = -(-tm // (nf * SUBLANES)) * SUBLANES
    rp = nf * ch
    src, dst, tile_expert, tile_valid = _route(topi, tm, rp)
    nt = src.shape[0]
    idx_spec = lambda fn: pl.BlockSpec((None, 1, rp), fn, memory_space=pltpu.SMEM)
    fsel = lambda i, f, tv: jnp.where(tv[i] == 1, f, nf - 1)
    return pl.pallas_call(
        functools.partial(_moe_kernel, tm=tm, ch=ch),
        grid_spec=pltpu.PrefetchScalarGridSpec(
            num_scalar_prefetch=2,
            grid=(nt, nf),
            in_specs=[
                idx_spec(lambda i, f, te, tv: (i, 0, 0)),
                idx_spec(lambda i, f, te, tv: (jnp.minimum(i + 1, nt - 1), 0, 0)),
                idx_spec(lambda i, f, te, tv: (jnp.minimum(i + 2, nt - 1), 0, 0)),
                idx_spec(lambda i, f, te, tv: (i, 0, 0)),
                idx_spec(lambda i, f, te, tv: (i + 1, 0, 0)),
                pl.BlockSpec(memory_space=pl.ANY),
                pl.BlockSpec((None, D_MODEL, tf), lambda i, f, te, tv: (te[i], 0, fsel(i, f, tv))),
                pl.BlockSpec((None, D_MODEL, tf), lambda i, f, te, tv: (te[i], 0, fsel(i, f, tv))),
                pl.BlockSpec((None, tf, D_MODEL), lambda i, f, te, tv: (te[i], fsel(i, f, tv), 0)),
            ],
            out_specs=pl.BlockSpec(memory_space=pl.ANY),
            scratch_shapes=[
                pltpu.VMEM((MOE_DEPTH, rp, D_MODEL), F32),
                pltpu.VMEM((tm, D_MODEL), BF16),
                pltpu.VMEM((MOE_DEPTH, rp, D_MODEL), F32),
                pltpu.SemaphoreType.DMA((MOE_DEPTH,)),
                pltpu.SemaphoreType.DMA((MOE_DEPTH,)),
            ],
        ),
        out_shape=jax.ShapeDtypeStruct((2 * t + MOE_DEPTH * rp, D_MODEL), F32),
        compiler_params=_params("arbitrary", "arbitrary"),
        name="ffn_moe",
    )(tile_expert, tile_valid, src, src, src, dst, dst, h2, wg, wu, wd)


def _combine_kernel(x1_ref, ya_ref, yb_ref, topw_ref, mod_ref, gfin_ref, o_ref):
    w = topw_ref[...]
    f = w[:, 0:1] * ya_ref[...] + w[:, 1:2] * yb_ref[...]
    o_ref[...] = _rms(x1_ref[...] + mod_ref[...][5:6, :] * f, gfin_ref[...])


def _combine(x1, y2, topw, mod, gfin, *, seq, tm):
    t = x1.shape[0]
    nt = seq // tm
    nb = t // tm
    row = lambda i: (i, 0)
    return pl.pallas_call(
        _combine_kernel,
        grid=(nb,),
        in_specs=[
            pl.BlockSpec((tm, D_MODEL), row),
            pl.BlockSpec((tm, D_MODEL), row),
            pl.BlockSpec((tm, D_MODEL), lambda i: (nb + i, 0)),
            pl.BlockSpec((tm, LANES), row),
            pl.BlockSpec((None, 6, D_MODEL), lambda i: (i // nt, 0, 0)),
            _const_spec((1, D_MODEL)),
        ],
        out_specs=pl.BlockSpec((tm, D_MODEL), row),
        out_shape=jax.ShapeDtypeStruct((t, D_MODEL), F32),
        compiler_params=_params("parallel"),
        name="moe_combine",
    )(x1, y2, y2, topw, mod, gfin)


def _rope_tables(seq):
    pos = jnp.arange(seq, dtype=F32)
    inv = 1.0 / (ROPE_THETA ** (jnp.arange(0, MLA_DR, 2, dtype=F32) / MLA_DR))
    ang = pos[:, None] * inv[None, :]
    cos, sin = jnp.cos(ang), jnp.sin(ang)
    zero = jnp.zeros((seq, LANES - MLA_DR), F32)
    return (jnp.concatenate([cos, cos, zero], axis=1), jnp.concatenate([-sin, sin, zero], axis=1))


def _t5_bucket(rel):
    nb = NUM_BUCKETS // 2
    ret = (rel > 0).astype(jnp.int32) * nb
    n = jnp.abs(rel)
    max_exact = nb // 2
    nf = jnp.maximum(n, 1).astype(F32)
    large = max_exact + (jnp.log(nf / max_exact) / math.log(MAX_DISTANCE / max_exact)
                         * (nb - max_exact)).astype(jnp.int32)
    large = jnp.minimum(large, nb - 1)
    return ret + jnp.where(n < max_exact, n, large)


def _window_bias(rel_bias):
    qpos = jnp.arange(WINDOW, dtype=jnp.int32)
    jpos = jnp.arange(3 * WINDOW, dtype=jnp.int32)
    rel = jpos[None, :] - WINDOW - qpos[:, None]
    bucket = _t5_bucket(rel)
    bias = jnp.zeros((GQA_HEADS, WINDOW, 3 * WINDOW), F32)
    for b in range(NUM_BUCKETS):
        bias = bias + jnp.where((bucket == b)[None], rel_bias[b].astype(F32)[:, None, None], 0.0)
    bias = jnp.where((jnp.abs(rel) <= WINDOW)[None], bias, NEG_BIG)
    return bias.reshape(GQA_KV_HEADS, GQA_GROUP * WINDOW, 3 * WINDOW)


def _prep_layer(w_in, w_uq, w_ukv, w_out):
    half = MLA_DR // 2
    zpad = jnp.zeros((D_MODEL, LANES - MLA_DR), F32)
    kr0 = Q_LORA + KV_LORA
    k1 = w_in[:, kr0:kr0 + half]
    k2 = w_in[:, kr0 + half:kr0 + MLA_DR]
    w_in_p = jnp.concatenate(
        [w_in[:, :kr0], k1, k2, zpad, k2, k1, zpad, w_in[:, kr0 + MLA_DR:]], axis=1).astype(BF16)
    wq = w_uq.reshape(Q_LORA, MLA_HEADS, MLA_DN + MLA_DR)
    r1 = wq[:, :, MLA_DN:MLA_DN + half]
    r2 = wq[:, :, MLA_DN + half:]
    zq = jnp.zeros((Q_LORA, MLA_HEADS, LANES - MLA_DR), F32)
    w_uq_a = jnp.concatenate([wq[:, :, :MLA_DN], r1, r2, zq], axis=2).reshape(Q_LORA, MLA_HEADS * QK_PAD)
    w_uq_b = jnp.concatenate([r2, r1, zq], axis=2).reshape(Q_LORA, MLA_HEADS * LANES)
    out_a = MLA_HEADS * MLA_DV
    return (w_in_p, w_uq_a.astype(BF16), w_uq_b.astype(BF16), w_ukv.astype(BF16),
            w_out[:out_a].astype(BF16), w_out[out_a:].astype(BF16))


def _trunk(x, mods, layers, bias, g_final, *, batch, seq):
    tm = min(ROW_TILE, seq)
    cos_t, sin_t = _rope_tables(seq)
    for l, p in enumerate(layers):
        tk = min(MLA_TK, tm)
        q, k, v, gq, gk, gv = _premix(x, mods[l], p["g_norm_mix"], cos_t, sin_t, p["w_in_p"], p["g_q_lat"],
                                      p["w_uq_a"], p["w_uq_b"], p["g_kv_lat"], p["w_ukv"], seq=seq, tm=tm,
                                      vchunk=tk)
        tq_sub = min(MLA_TQ_SUB, seq)
        tq = min(MLA_TQ, seq)
        oa = _mla(q, k, v, batch=batch, seq=seq, tq=tq, tk=tk, nsub=tq // tq_sub)
        ob = _window(gq, gk, gv, bias, p["sink_col"], seq=seq, rows=min(ROW_TILE, seq))
        res = _postmix(oa, ob, x, mods[l], p["g_out_a"], p["g_out_b"], p["w_out_a"], p["w_out_b"],
                       p["g_norm_ffn"], p.get("w_router"), seq=seq, tm=tm)
        if "w_router" in p:
            assert l == len(layers) - 1
            x1, h2, topw, topi = res
            y2 = _moe_ffn(h2, topi[:, :2], p["w_gate"], p["w_up"], p["w_down"], tm=ROW_TILE, tf=FF_TILE)
            x = _combine(x1, y2, topw, mods[l], g_final, seq=seq, tm=tm)
        else:
            x1, h2 = res
            x = _ffn(x1, h2, mods[l], p["w_gate"], p["w_up"], p["w_down"], seq=seq, tm=tm, tf=FF_TILE)
    return x


def kernel(x_prompt, x_sample, c_prompt, c_sample, rel_bias, w_ada, b_ada, g_norm_mix, g_norm_ffn, w_in, g_q_lat, w_uq, g_kv_lat, w_ukv, sink, g_out_a, g_out_b, w_out, w_gate_d, w_up_d, w_down_d, w_router, w_gate_e, w_up_e, w_down_e, g_final):
    bp, sp, _ = x_prompt.shape
    bs, ss, _ = x_sample.shape
    rows = -(-(bp + bs) // 8) * 8
    c_all = jnp.concatenate([c_prompt, c_sample, jnp.zeros((rows - bp - bs, D_MODEL), F32)], axis=0)
    mod = _ada_mod(c_all, w_ada, b_ada)
    mod_p = [mod[l, :bp].reshape(bp, 6, D_MODEL) for l in range(DEPTH)]
    mod_s = [mod[l, bp:bp + bs].reshape(bs, 6, D_MODEL) for l in range(DEPTH)]
    bias = _window_bias(rel_bias)
    layers = []
    for l in range(DEPTH):
        w_in_p, w_uq_a, w_uq_b, w_ukv_b, w_out_a, w_out_b = _prep_layer(w_in[l], w_uq[l], w_ukv[l], w_out[l])
        out_a = MLA_HEADS * MLA_DV
        p = dict(
            g_norm_mix=g_norm_mix[l][None], g_norm_ffn=g_norm_ffn[l][None],
            w_in_p=w_in_p, g_q_lat=g_q_lat[l][None], w_uq_a=w_uq_a, w_uq_b=w_uq_b,
            g_kv_lat=g_kv_lat[l][None], w_ukv=w_ukv_b,
            sink_col=jnp.repeat(sink[l].astype(F32), WINDOW).reshape(GQA_KV_HEADS, GQA_GROUP * WINDOW, 1),
            g_out_a=g_out_a[l][None], g_out_b=g_out_b[l][None], w_out_a=w_out_a, w_out_b=w_out_b,
        )
        i = l // 2
        if l % 2 == 0:
            p.update(w_gate=w_gate_d[i].astype(BF16), w_up=w_up_d[i].astype(BF16), w_down=w_down_d[i].astype(BF16))
        else:
            wr = jnp.concatenate([w_router[i], jnp.zeros((D_MODEL, LANES - N_EXPERTS), F32)], axis=1)
            wr_hi = wr.astype(BF16)
            wr = jnp.stack([wr_hi, (wr - wr_hi.astype(F32)).astype(BF16)])
            p.update(w_gate=w_gate_e[i].astype(BF16), w_up=w_up_e[i].astype(BF16),
                     w_down=w_down_e[i].astype(BF16), w_router=wr)
        layers.append(p)
    gfin = g_final[None]
    y_s = _trunk(x_sample.reshape(bs * ss, D_MODEL), mod_s, layers, bias, gfin, batch=bs, seq=ss)
    y_p = _trunk(x_prompt.reshape(bp * sp, D_MODEL), mod_p, layers, bias, gfin, batch=bp, seq=sp)
    return (y_p.reshape(bp, sp, D_MODEL), y_s.reshape(bs, ss, D_MODEL))
```

```python
import functools
import math

import jax
import jax.numpy as jnp
from jax import lax
from jax.experimental import pallas as pl
from jax.experimental.pallas import tpu as pltpu

F32 = jnp.float32
BF16 = jnp.bfloat16

D_MODEL = 2048
DEPTH = 2
EPS = 1e-6
MLA_HEADS = 8
MLA_DN = 128
MLA_DR = 64
MLA_DV = 128
Q_LORA = 512
KV_LORA = 512
ROPE_THETA = 10000.0
MLA_SCALE = 1.0 / math.sqrt(MLA_DN + MLA_DR)
MLA_QSCALE = MLA_SCALE * math.log2(math.e)
GQA_HEADS = 8
GQA_KV_HEADS = 2
GQA_GROUP = GQA_HEADS // GQA_KV_HEADS
GQA_DH = 128
WINDOW = 128
GQA_SCALE = 1.0 / math.sqrt(GQA_DH)
NUM_BUCKETS = 32
MAX_DISTANCE = 128
N_EXPERTS = 8
NEG_BIG = -1e30

LANES = 128
SUBLANES = 8
BF16_SUBLANES = 16
V7X_VMEM_LIMIT = 56 * 1024 * 1024

QK_PAD = 2 * LANES
VT_ROWS = MLA_DV + BF16_SUBLANES
MLA_TQ = 2048
MLA_TQ_SUB = 256
MLA_TK = 512
ROW_TILE = 512
FF_TILE = 512
MOE_DEPTH = 3
ADA_TILE = 1024
Z_CQ = 0
Z_CKV = Z_CQ + Q_LORA
Z_KRA = Z_CKV + KV_LORA
Z_KRB = Z_KRA + LANES
Z_GQ = Z_KRB + LANES
Z_GK = Z_GQ + GQA_HEADS * GQA_DH
Z_GV = Z_GK + GQA_KV_HEADS * GQA_DH
Z_END = Z_GV + GQA_KV_HEADS * GQA_DH


def _params(*sem):
    return pltpu.CompilerParams(dimension_semantics=sem, vmem_limit_bytes=V7X_VMEM_LIMIT)


def _rms(x, g):
    return x * lax.rsqrt(jnp.mean(x * x, axis=-1, keepdims=True) + EPS) * g


def _silu(a):
    return a * (1.0 / (1.0 + jnp.exp(-a)))


def _const_spec(shape):
    nd = len(shape)
    return pl.BlockSpec(shape, lambda *_: (0,) * nd)


def _ada_kernel(c_ref, w_ref, b_ref, o_ref):
    cs = _silu(c_ref[...])
    o_ref[...] = jnp.dot(cs, w_ref[...], preferred_element_type=F32,
                         precision=lax.Precision.HIGHEST) + b_ref[...]


def _ada_mod(c_all, w_ada, b_ada):
    rows = c_all.shape[0]
    n = w_ada.shape[-1]
    tn = ADA_TILE
    return pl.pallas_call(
        _ada_kernel,
        grid=(DEPTH, n // tn),
        in_specs=[
            pl.BlockSpec((rows, D_MODEL), lambda l, j: (0, 0)),
            pl.BlockSpec((None, D_MODEL, tn), lambda l, j: (l, 0, j)),
            pl.BlockSpec((None, 1, tn), lambda l, j: (l, 0, j)),
        ],
        out_specs=pl.BlockSpec((None, rows, tn), lambda l, j: (l, 0, j)),
        out_shape=jax.ShapeDtypeStruct((DEPTH, rows, n), F32),
        compiler_params=_params("parallel", "parallel"),
        name="ada_mod",
    )(c_all, w_ada, b_ada.reshape(DEPTH, 1, n))


def _premix_kernel(x_ref, mod_ref, gn_ref, cos_ref, sin_ref, win_ref, gq_ref, wuqa_ref, wuqb_ref,
                   gkv_ref, wukv_ref, q_ref, k_ref, v_ref, wq_ref, wk_ref, wv_ref):
    mod = mod_ref[...]
    h = _rms(x_ref[...], gn_ref[...]) * (1.0 + mod[1:2, :]) + mod[0:1, :]
    z = jnp.dot(h.astype(BF16), win_ref[...], preferred_element_type=F32)
    cqn = _rms(z[:, Z_CQ:Z_CKV], gq_ref[...]).astype(BF16)
    ckvn = _rms(z[:, Z_CKV:Z_KRA], gkv_ref[...]).astype(BF16)
    cos_t = cos_ref[...]
    sin_t = sin_ref[...]
    kr = (z[:, Z_KRA:Z_KRB] * cos_t + z[:, Z_KRB:Z_GQ] * sin_t).astype(BF16)
    wq_ref[...] = (z[:, Z_GQ:Z_GK] * GQA_SCALE).astype(BF16)
    wk_ref[...] = z[:, Z_GK:Z_GV].astype(BF16)
    wv_ref[...] = z[:, Z_GV:Z_END].astype(BF16)
    qa = jnp.dot(cqn, wuqa_ref[...], preferred_element_type=F32)
    qb = jnp.dot(cqn, wuqb_ref[...], preferred_element_type=F32)
    kv = jnp.dot(ckvn, wukv_ref[...], preferred_element_type=F32)
    vchunk = v_ref.shape[3]
    sub = lax.broadcasted_iota(jnp.int32, (VT_ROWS - MLA_DV, vchunk), 0)
    ones_row = jnp.where(sub == 0, 1.0, 0.0).astype(BF16)
    for hh in range(MLA_HEADS):
        a0 = hh * QK_PAD
        q_ref[hh, :, 0:LANES] = (qa[:, a0:a0 + LANES] * MLA_QSCALE).astype(BF16)
        q_rope = qa[:, a0 + LANES:a0 + QK_PAD] * cos_t + qb[:, hh * LANES:(hh + 1) * LANES] * sin_t
        q_ref[hh, :, LANES:QK_PAD] = (q_rope * MLA_QSCALE).astype(BF16)
        k_ref[hh, :, 0:LANES] = kv[:, a0:a0 + LANES].astype(BF16)
        k_ref[hh, :, LANES:QK_PAD] = kr
        vt = kv[:, a0 + LANES:a0 + QK_PAD].T.astype(BF16)
        for c in range(v_ref.shape[1]):
            v_ref[hh, c, 0:MLA_DV, :] = vt[:, c * vchunk:(c + 1) * vchunk]
            v_ref[hh, c, MLA_DV:VT_ROWS, :] = ones_row


def _premix(x, mod, gn, cos_t, sin_t, w_in_p, g_q, w_uq_a, w_uq_b, g_kv, w_ukv, *, seq, tm, vchunk):
    t = x.shape[0]
    nt = seq // tm
    cpt = tm // vchunk
    row = lambda i: (i, 0)
    hrow = lambda i: (0, i, 0)
    return pl.pallas_call(
        _premix_kernel,
        grid=(t // tm,),
        in_specs=[
            pl.BlockSpec((tm, D_MODEL), row),
            pl.BlockSpec((None, 6, D_MODEL), lambda i: (i // nt, 0, 0)),
            _const_spec((1, D_MODEL)),
            pl.BlockSpec((tm, LANES), lambda i: (i % nt, 0)),
            pl.BlockSpec((tm, LANES), lambda i: (i % nt, 0)),
            _const_spec(w_in_p.shape),
            _const_spec((1, Q_LORA)),
            _const_spec(w_uq_a.shape),
            _const_spec(w_uq_b.shape),
            _const_spec((1, KV_LORA)),
            _const_spec(w_ukv.shape),
        ],
        out_specs=[
            pl.BlockSpec((MLA_HEADS, tm, QK_PAD), hrow),
            pl.BlockSpec((MLA_HEADS, tm, QK_PAD), hrow),
            pl.BlockSpec((MLA_HEADS, cpt, VT_ROWS, vchunk), lambda i: (0, i, 0, 0)),
            pl.BlockSpec((tm, GQA_HEADS * GQA_DH), row),
            pl.BlockSpec((tm, GQA_KV_HEADS * GQA_DH), row),
            pl.BlockSpec((tm, GQA_KV_HEADS * GQA_DH), row),
        ],
        out_shape=[
            jax.ShapeDtypeStruct((MLA_HEADS, t, QK_PAD), BF16),
            jax.ShapeDtypeStruct((MLA_HEADS, t, QK_PAD), BF16),
            jax.ShapeDtypeStruct((MLA_HEADS, t // vchunk, VT_ROWS, vchunk), BF16),
            jax.ShapeDtypeStruct((t, GQA_HEADS * GQA_DH), BF16),
            jax.ShapeDtypeStruct((t, GQA_KV_HEADS * GQA_DH), BF16),
            jax.ShapeDtypeStruct((t, GQA_KV_HEADS * GQA_DH), BF16),
        ],
        compiler_params=_params("parallel"),
        name="premix",
    )(x, mod, gn, cos_t, sin_t, w_in_p, g_q, w_uq_a, w_uq_b, g_kv, w_ukv)


def _mla_kernel(q_ref, k_ref, v_ref, o_ref, qt_scr, s_scr, p_scr, m_scr, a_scr, acc_scr, pm_scr, *, tk, nk, nsub):
    tq = q_ref.shape[0] // nsub
    subs = range(nsub)
    for i in subs:
        qt_scr[i] = q_ref[i * tq:(i + 1) * tq, :].astype(F32).T.astype(BF16)

    def scores(j, buf, i):
        k = k_ref[pl.ds(pl.multiple_of(j * tk, tk), tk), :]
        s = jnp.dot(k, qt_scr[i], preferred_element_type=F32)
        s_scr[buf, i] = s
        pm = s[0:SUBLANES, :]
        for g in range(1, tk // SUBLANES):
            pm = jnp.maximum(pm, s[g * SUBLANES:(g + 1) * SUBLANES, :])
        pm_scr[buf, i] = pm

    def smax(buf, i):
        m_old = m_scr[i]
        m_new = jnp.maximum(m_old, jnp.max(pm_scr[buf, i], axis=0, keepdims=True))
        m_scr[i] = m_new
        a_scr[buf, i] = jnp.exp2(m_old - m_new)

    def sexp(buf, i):
        p_scr[buf, i] = jnp.exp2(s_scr[buf, i] - m_scr[i]).astype(BF16)

    def softmax(buf, i):
        smax(buf, i)
        sexp(buf, i)

    def values(j, buf, i):
        v = v_ref[j]
        pv = jnp.dot(v, p_scr[buf, i], preferred_element_type=F32)
        acc_scr[i] = a_scr[buf, i] * acc_scr[i] + pv

    m_scr[...] = jnp.full(m_scr.shape, -jnp.inf, F32)
    acc_scr[...] = jnp.zeros(acc_scr.shape, F32)
    for i in subs:
        scores(0, 0, i)
        scores(1, 1, i)
        softmax(0, i)

    def body(u, carry):
        t = 2 * u + 1
        for i in subs:
            smax(1, i)
            scores(t + 1, 0, i)
            values(t - 1, 0, i)
            sexp(1, i)
        for i in subs:
            smax(0, i)
            scores(t + 2, 1, i)
            values(t, 1, i)
            sexp(0, i)
        return carry

    lax.fori_loop(0, (nk - 2) // 2, body, 0)
    for i in subs:
        softmax(1, i)
        values(nk - 2, 0, i)
        values(nk - 1, 1, i)
    for i in subs:
        acc = acc_scr[i]
        o_ref[i * tq:(i + 1) * tq, :] = (acc[0:MLA_DV, :] * (1.0 / acc[MLA_DV:MLA_DV + 1, :])).T.astype(BF16)


def _mla(q, k, v, *, batch, seq, tq, tk, nsub):
    t = batch * seq
    nq = seq // tq
    nk = seq // tk
    assert nk >= 2 and nk % 2 == 0, (seq, tk)
    sub = tq // nsub
    return pl.pallas_call(
        functools.partial(_mla_kernel, tk=tk, nk=nk, nsub=nsub),
        grid=(batch, MLA_HEADS, nq),
        in_specs=[
            pl.BlockSpec((None, tq, QK_PAD), lambda b, h, i: (h, b * nq + i, 0)),
            pl.BlockSpec((None, seq, QK_PAD), lambda b, h, i: (h, b, 0)),
            pl.BlockSpec((None, nk, VT_ROWS, tk), lambda b, h, i: (h, b, 0, 0)),
        ],
        out_specs=pl.BlockSpec((tq, MLA_DV), lambda b, h, i: (b * nq + i, h)),
        out_shape=jax.ShapeDtypeStruct((t, MLA_HEADS * MLA_DV), BF16),
        scratch_shapes=[
            pltpu.VMEM((nsub, QK_PAD, sub), BF16),
            pltpu.VMEM((2, nsub, tk, sub), F32),
            pltpu.VMEM((2, nsub, tk, sub), BF16),
            pltpu.VMEM((nsub, 1, sub), F32),
            pltpu.VMEM((2, nsub, 1, sub), F32),
            pltpu.VMEM((nsub, VT_ROWS, sub), F32),
            pltpu.VMEM((2, nsub, SUBLANES, sub), F32),
        ],
        compiler_params=_params("parallel", "parallel", "parallel"),
        name="mla_attn",
    )(q, k, v)


def _window_kernel(q_ref, kp_ref, kc_ref, kn_ref, vp_ref, vc_ref, vn_ref, bias_ref, sink_ref, o_ref,
                   kcat, vcat, *, rows, nt):
    i = pl.program_id(0)
    nsub = rows // WINDOW
    kcat[0:WINDOW, :] = kp_ref[...]
    kcat[WINDOW:WINDOW + rows, :] = kc_ref[...]
    kcat[WINDOW + rows:, :] = kn_ref[...]
    vcat[0:WINDOW, :] = vp_ref[...]
    vcat[WINDOW:WINDOW + rows, :] = vc_ref[...]
    vcat[WINDOW + rows:, :] = vn_ref[...]
    first_tile = (i % nt) == 0
    last_tile = (i % nt) == nt - 1
    col = lax.broadcasted_iota(jnp.int32, (1, 3 * WINDOW), 1)

    chains = [(n, kh) for n in range(nsub) for kh in range(GQA_KV_HEADS)]
    s_all = []
    for n, kh in chains:
        r0 = n * WINDOW
        qall = q_ref[r0:r0 + WINDOW, kh * GQA_GROUP * GQA_DH:(kh + 1) * GQA_GROUP * GQA_DH]
        qs = jnp.concatenate([qall[:, g * GQA_DH:(g + 1) * GQA_DH] for g in range(GQA_GROUP)], axis=0)
        kk = kcat[r0:r0 + 3 * WINDOW, kh * GQA_DH:(kh + 1) * GQA_DH]
        s = lax.dot_general(qs, kk, (((1,), (1,)), ((), ())), preferred_element_type=F32) + bias_ref[kh]
        if n == 0:
            s = s + jnp.where(jnp.logical_and(first_tile, col < WINDOW), NEG_BIG, 0.0)
        if n == nsub - 1:
            s = s + jnp.where(jnp.logical_and(last_tile, col >= 2 * WINDOW), NEG_BIG, 0.0)
        s_all.append(s)
    m_all = [jnp.maximum(jnp.max(s, axis=1, keepdims=True), sink_ref[kh]) for s, (n, kh) in zip(s_all, chains)]
    e_all = [jnp.exp(s - m) for s, m in zip(s_all, m_all)]
    d_all = [jnp.sum(e, axis=1, keepdims=True) + jnp.exp(sink_ref[kh] - m)
             for e, m, (n, kh) in zip(e_all, m_all, chains)]
    for e, d, (n, kh) in zip(e_all, d_all, chains):
        r0 = n * WINDOW
        vv = vcat[r0:r0 + 3 * WINDOW, kh * GQA_DH:(kh + 1) * GQA_DH]
        o = jnp.dot(e.astype(BF16), vv, preferred_element_type=F32) * (1.0 / d)
        for g in range(GQA_GROUP):
            hcol = (kh * GQA_GROUP + g) * GQA_DH
            o_ref[r0:r0 + WINDOW, hcol:hcol + GQA_DH] = o[g * WINDOW:(g + 1) * WINDOW].astype(BF16)


def _window(gq, gk, gv, bias, sink_col, *, seq, rows):
    t = gq.shape[0]
    nt = seq // rows
    rb = rows // WINDOW
    nblk = t // WINDOW
    kvw = GQA_KV_HEADS * GQA_DH
    prev = lambda i: (jnp.maximum(i * rb - 1, 0), 0)
    nxt = lambda i: (jnp.minimum((i + 1) * rb, nblk - 1), 0)
    cur = lambda i: (i, 0)
    return pl.pallas_call(
        functools.partial(_window_kernel, rows=rows, nt=nt),
        grid=(t // rows,),
        in_specs=[
            pl.BlockSpec((rows, GQA_HEADS * GQA_DH), cur),
            pl.BlockSpec((WINDOW, kvw), prev),
            pl.BlockSpec((rows, kvw), cur),
            pl.BlockSpec((WINDOW, kvw), nxt),
            pl.BlockSpec((WINDOW, kvw), prev),
            pl.BlockSpec((rows, kvw), cur),
            pl.BlockSpec((WINDOW, kvw), nxt),
            _const_spec(bias.shape),
            _const_spec(sink_col.shape),
        ],
        out_specs=pl.BlockSpec((rows, GQA_HEADS * GQA_DH), cur),
        out_shape=jax.ShapeDtypeStruct((t, GQA_HEADS * GQA_DH), BF16),
        scratch_shapes=[pltpu.VMEM((rows + 2 * WINDOW, kvw), BF16),
                        pltpu.VMEM((rows + 2 * WINDOW, kvw), BF16)],
        compiler_params=_params("parallel"),
        name="window_attn",
    )(gq, gk, gk, gk, gv, gv, gv, bias, sink_col)


def _router(h2, wr_ref, topw_ref, topi_ref):
    h_hi = h2.astype(BF16)
    h_lo = (h2 - h_hi.astype(F32)).astype(BF16)
    logits = (jnp.dot(h_hi, wr_ref[0], preferred_element_type=F32)
              + jnp.dot(h_lo, wr_ref[0], preferred_element_type=F32)
              + jnp.dot(h_hi, wr_ref[1], preferred_element_type=F32))
    lane = lax.broadcasted_iota(jnp.int32, logits.shape, 1)
    lg = jnp.where(lane < N_EXPERTS, logits, -jnp.inf)
    m1 = jnp.max(lg, axis=1, keepdims=True)
    i1 = jnp.min(jnp.where(lg == m1, lane, LANES), axis=1, keepdims=True)
    lg2 = jnp.where(lane == i1, -jnp.inf, lg)
    m2 = jnp.max(lg2, axis=1, keepdims=True)
    i2 = jnp.min(jnp.where(lg2 == m2, lane, LANES), axis=1, keepdims=True)
    e2 = jnp.exp(m2 - m1)
    w1 = 1.0 / (1.0 + e2)
    w2 = e2 * w1
    topw_ref[...] = jnp.where(lane == 0, w1, jnp.where(lane == 1, w2, 0.0))
    topi_ref[...] = jnp.where(lane == 0, i1, jnp.where(lane == 1, i2, 0))


def _postmix_kernel(oa_ref, ob_ref, x_ref, mod_ref, ga_ref, gb_ref, wa_ref, wb_ref, gf_ref, *rest, moe):
    if moe:
        wr_ref, x1_ref, h2_ref, topw_ref, topi_ref, hprev = rest

        @pl.when(pl.program_id(0) == 0)
        def _():
            hprev[...] = jnp.zeros_like(hprev)

        _router(hprev[...], wr_ref, topw_ref, topi_ref)
    else:
        x1_ref, h2_ref = rest
    mod = mod_ref[...]
    na = _rms(oa_ref[...].astype(F32), ga_ref[...]).astype(BF16)
    nb = _rms(ob_ref[...].astype(F32), gb_ref[...]).astype(BF16)
    mix = (jnp.dot(na, wa_ref[...], preferred_element_type=F32)
           + jnp.dot(nb, wb_ref[...], preferred_element_type=F32))
    x1 = x_ref[...] + mod[2:3, :] * mix
    x1_ref[...] = x1
    h2 = _rms(x1, gf_ref[...]) * (1.0 + mod[4:5, :]) + mod[3:4, :]
    h2_ref[...] = h2.astype(h2_ref.dtype)
    if moe:
        hprev[...] = h2


def _postmix(oa, ob, x, mod, ga, gb, wa, wb, gf, wr, *, seq, tm):
    t = x.shape[0]
    nt = seq // tm
    nb = t // tm
    moe = wr is not None
    row = (lambda i: (jnp.minimum(i, nb - 1), 0)) if moe else (lambda i: (i, 0))
    lag = lambda i: (jnp.maximum(i - 1, 0), 0)
    half = oa.shape[1]
    in_specs = [
        pl.BlockSpec((tm, half), row),
        pl.BlockSpec((tm, half), row),
        pl.BlockSpec((tm, D_MODEL), row),
        pl.BlockSpec((None, 6, D_MODEL), lambda i: (row(i)[0] // nt, 0, 0)),
        _const_spec((1, half)),
        _const_spec((1, half)),
        _const_spec(wa.shape),
        _const_spec(wb.shape),
        _const_spec((1, D_MODEL)),
    ]
    out_specs = [pl.BlockSpec((tm, D_MODEL), row), pl.BlockSpec((tm, D_MODEL), row)]
    out_shape = [jax.ShapeDtypeStruct((t, D_MODEL), F32), jax.ShapeDtypeStruct((t, D_MODEL), F32 if moe else BF16)]
    args = [oa, ob, x, mod, ga, gb, wa, wb, gf]
    scratch = []
    if moe:
        in_specs.append(_const_spec(wr.shape))
        out_specs += [pl.BlockSpec((tm, LANES), lag), pl.BlockSpec((tm, LANES), lag)]
        out_shape += [jax.ShapeDtypeStruct((t, LANES), F32), jax.ShapeDtypeStruct((t, LANES), jnp.int32)]
        args.append(wr)
        scratch.append(pltpu.VMEM((tm, D_MODEL), F32))
    return pl.pallas_call(
        functools.partial(_postmix_kernel, moe=moe),
        grid=(nb + 1 if moe else nb,),
        in_specs=in_specs,
        out_specs=out_specs,
        out_shape=out_shape,
        scratch_shapes=scratch,
        compiler_params=_params("arbitrary" if moe else "parallel"),
        name="postmix_moe" if moe else "postmix",
    )(*args)


def _ffn_kernel(x1_ref, h_ref, mod_ref, wg_ref, wu_ref, wd_ref, o_ref, acc_ref):
    f = pl.program_id(1)

    @pl.when(f == 0)
    def _():
        acc_ref[...] = jnp.zeros_like(acc_ref)

    h = h_ref[...]
    a = jnp.dot(h, wg_ref[...], preferred_element_type=F32)
    u = jnp.dot(h, wu_ref[...], preferred_element_type=F32)
    acc_ref[...] += jnp.dot((_silu(a) * u).astype(BF16), wd_ref[...], preferred_element_type=F32)

    @pl.when(f == pl.num_programs(1) - 1)
    def _():
        o_ref[...] = x1_ref[...] + mod_ref[...][5:6, :] * acc_ref[...]


def _ffn(x1, h2, mod, wg, wu, wd, *, seq, tm, tf):
    t = x1.shape[0]
    nt = seq // tm
    dff = wg.shape[1]
    assert t % tm == 0 and seq % tm == 0 and dff % tf == 0, (t, seq, tm, dff, tf)
    row = lambda i, f: (i, 0)
    return pl.pallas_call(
        _ffn_kernel,
        grid=(t // tm, dff // tf),
        in_specs=[
            pl.BlockSpec((tm, D_MODEL), row),
            pl.BlockSpec((tm, D_MODEL), row),
            pl.BlockSpec((None, 6, D_MODEL), lambda i, f: (i // nt, 0, 0)),
            pl.BlockSpec((D_MODEL, tf), lambda i, f: (0, f)),
            pl.BlockSpec((D_MODEL, tf), lambda i, f: (0, f)),
            pl.BlockSpec((tf, D_MODEL), lambda i, f: (f, 0)),
        ],
        out_specs=pl.BlockSpec((tm, D_MODEL), row),
        out_shape=jax.ShapeDtypeStruct((t, D_MODEL), F32),
        scratch_shapes=[pltpu.VMEM((tm, D_MODEL), F32)],
        compiler_params=_params("parallel", "arbitrary"),
        name="ffn_dense",
    )(x1, h2, mod, wg, wu, wd)


def _route(topi, tm, rp):
    t = topi.shape[0]
    na = 2 * t
    nt = na // tm + N_EXPERTS
    flat_e = topi.reshape(na)
    onehot = (flat_e[:, None] == jnp.arange(N_EXPERTS, dtype=jnp.int32)[None, :]).astype(jnp.int32)
    csum = jnp.cumsum(onehot, axis=0)
    rank = jnp.sum((csum - onehot) * onehot, axis=1)
    padded = ((csum[-1] + tm - 1) // tm) * tm
    ends = jnp.cumsum(padded)
    pos = (ends - padded)[flat_e] + rank
    pair = jnp.full((nt * tm,), -1, jnp.int32).at[pos].set(jnp.arange(na, dtype=jnp.int32))
    pair = jnp.pad(pair.reshape(nt, tm), ((0, 0), (0, rp - tm)), constant_values=-1)
    spare = (na + (jnp.arange(nt + 1, dtype=jnp.int32)[:, None] % MOE_DEPTH) * rp
             + jnp.arange(rp, dtype=jnp.int32)[None, :])
    src = jnp.where(pair >= 0, pair // 2, 0)
    dst = jnp.where(pair >= 0, (pair % 2) * t + pair // 2, spare[1:])
    dst = jnp.concatenate([spare[:1], dst], axis=0)
    tile_start = jnp.arange(nt, dtype=jnp.int32) * tm
    tile_valid = (tile_start < ends[-1]).astype(jnp.int32)
    tile_expert = jnp.minimum(jnp.sum((tile_start[:, None] >= ends[None, :]).astype(jnp.int32), axis=1),
                              N_EXPERTS - 1)
    return src.reshape(nt, 1, rp), dst.reshape(nt + 1, 1, rp), tile_expert, tile_valid


def _moe_kernel(te_ref, tv_ref, src0_ref, src1_ref, src2_ref, pdst_ref, dst_ref, h_hbm, wg_ref, wu_ref, wd_ref,
                y_hbm, hbuf, hb16, acc, gsem, ssem, *, tm, ch):
    i = pl.program_id(0)
    f = pl.program_id(1)
    nt = pl.num_programs(0)
    nf = pl.num_programs(1)
    rp = nf * ch
    cur = i % MOE_DEPTH
    nxt = (i + 1) % MOE_DEPTH
    oth = (i + 2) % MOE_DEPTH
    valid = tv_ref[i] == 1
    last_valid = jnp.logical_and(valid, jnp.logical_or(i == nt - 1, tv_ref[jnp.minimum(i + 1, nt - 1)] == 0))

    def gather_row(idx_ref, r, b):
        pltpu.make_async_copy(h_hbm.at[pl.ds(idx_ref[0, r], 1), :], hbuf.at[b, pl.ds(r, 1), :],
                              gsem.at[b]).start()

    def scatter_row(idx_ref, r, b):
        pltpu.make_async_copy(acc.at[b, pl.ds(r, 1), :], y_hbm.at[pl.ds(idx_ref[0, r], 1), :],
                              ssem.at[b]).start()

    def wait_rows(buf, sem, b):
        pltpu.make_async_copy(buf.at[b], buf.at[b], sem.at[b]).wait()

    @pl.when(jnp.logical_and(i == 0, f == 0))
    def _():
        acc[...] = jnp.zeros_like(acc)

        def first(r, c):
            gather_row(src0_ref, r, 0)
            gather_row(src1_ref, r, 1)
            return c
        lax.fori_loop(0, rp, first, 0, unroll=8)

    @pl.when(jnp.logical_and(valid, f == 0))
    def _():
        wait_rows(hbuf, gsem, cur)
        hb16[...] = hbuf[cur, 0:tm, :].astype(BF16)

        @pl.when(i >= 2)
        def _():
            wait_rows(acc, ssem, cur)

        acc[cur, 0:tm, :] = jnp.zeros((tm, D_MODEL), F32)

    @pl.when(valid)
    def _():
        for c in range(ch):
            r = f * ch + c
            gather_row(src2_ref, r, oth)
            scatter_row(pdst_ref, r, oth)
        h = hb16[...]
        a = jnp.dot(h, wg_ref[...], preferred_element_type=F32)
        u = jnp.dot(h, wu_ref[...], preferred_element_type=F32)
        acc[cur, 0:tm, :] += jnp.dot((_silu(a) * u).astype(BF16), wd_ref[...], preferred_element_type=F32)

    @pl.when(jnp.logical_and(last_valid, f == nf - 1))
    def _():
        def own(r, c):
            scatter_row(dst_ref, r, cur)
            return c
        lax.fori_loop(0, rp, own, 0, unroll=8)
        @pl.when(i >= 1)
        def _():
            wait_rows(acc, ssem, nxt)

        wait_rows(acc, ssem, oth)
        wait_rows(acc, ssem, cur)
        wait_rows(hbuf, gsem, nxt)
        wait_rows(hbuf, gsem, oth)
        hbuf[0] = jnp.zeros((rp, D_MODEL), F32)
        for b in range(MOE_DEPTH):
            spare = pltpu.make_async_copy(hbuf.at[0], y_hbm.at[pl.ds(y_hbm.shape[0] - (b + 1) * rp, rp), :],
                                          gsem.at[0])
            spare.start()
            spare.wait()


def _moe_ffn(h2, topi, wg, wu, wd, *, tm, tf):
    t = h2.shape[0]
    dff = wg.shape[2]
    assert (2 * t) % tm == 0 and dff % tf == 0, (t, tm, dff, tf)
    nf = dff // tf
    ch = -(-tm // (nf * SUBLANES)) * SUBLANES
    rp = nf * ch
    src, dst, tile_expert, tile_valid = _route(topi, tm, rp)
    nt = src.shape[0]
    idx_spec = lambda fn: pl.BlockSpec((None, 1, rp), fn, memory_space=pltpu.SMEM)
    fsel = lambda i, f, tv: jnp.where(tv[i] == 1, f, nf - 1)
    return pl.pallas_call(
        functools.partial(_moe_kernel, tm=tm, ch=ch),
        grid_spec=pltpu.PrefetchScalarGridSpec(
            num_scalar_prefetch=2,
            grid=(nt, nf),
            in_specs=[
                idx_spec(lambda i, f, te, tv: (i, 0, 0)),
                idx_spec(lambda i, f, te, tv: (jnp.minimum(i + 1, nt - 1), 0, 0)),
                idx_spec(lambda i, f, te, tv: (jnp.minimum(i + 2, nt - 1), 0, 0)),
                idx_spec(lambda i, f, te, tv: (i, 0, 0)),
                idx_spec(lambda i, f, te, tv: (i + 1, 0, 0)),
                pl.BlockSpec(memory_space=pl.ANY),
                pl.BlockSpec((None, D_MODEL, tf), lambda i, f, te, tv: (te[i], 0, fsel(i, f, tv))),
                pl.BlockSpec((None, D_MODEL, tf), lambda i, f, te, tv: (te[i], 0, fsel(i, f, tv))),
                pl.BlockSpec((None, tf, D_MODEL), lambda i, f, te, tv: (te[i], fsel(i, f, tv), 0)),
            ],
            out_specs=pl.BlockSpec(memory_space=pl.ANY),
            scratch_shapes=[
                pltpu.VMEM((MOE_DEPTH, rp, D_MODEL), F32),
                pltpu.VMEM((tm, D_MODEL), BF16),
                pltpu.VMEM((MOE_DEPTH, rp, D_MODEL), F32),
                pltpu.SemaphoreType.DMA((MOE_DEPTH,)),
                pltpu.SemaphoreType.DMA((MOE_DEPTH,)),
            ],
        ),
        out_shape=jax.ShapeDtypeStruct((2 * t + MOE_DEPTH * rp, D_MODEL), F32),
        compiler_params=_params("arbitrary", "arbitrary"),
        name="ffn_moe",
    )(tile_expert, tile_valid, src, src, src, dst, dst, h2, wg, wu, wd)


def _combine_kernel(x1_ref, ya_ref, yb_ref, topw_ref, mod_ref, gfin_ref, o_ref):
    w = topw_ref[...]
    f = w[:, 0:1] * ya_ref[...] + w[:, 1:2] * yb_ref[...]
    o_ref[...] = _rms(x1_ref[...] + mod_ref[...][5:6, :] * f, gfin_ref[...])


def _combine(x1, y2, topw, mod, gfin, *, seq, tm):
    t = x1.shape[0]
    nt = seq // tm
    nb = t // tm
    row = lambda i: (i, 0)
    return pl.pallas_call(
        _combine_kernel,
        grid=(nb,),
        in_specs=[
            pl.BlockSpec((tm, D_MODEL), row),
            pl.BlockSpec((tm, D_MODEL), row),
            pl.BlockSpec((tm, D_MODEL), lambda i: (nb + i, 0)),
            pl.BlockSpec((tm, LANES), row),
            pl.BlockSpec((None, 6, D_MODEL), lambda i: (i // nt, 0, 0)),
            _const_spec((1, D_MODEL)),
        ],
        out_specs=pl.BlockSpec((tm, D_MODEL), row),
        out_shape=jax.ShapeDtypeStruct((t, D_MODEL), F32),
        compiler_params=_params("parallel"),
        name="moe_combine",
    )(x1, y2, y2, topw, mod, gfin)


def _rope_tables(seq):
    pos = jnp.arange(seq, dtype=F32)
    inv = 1.0 / (ROPE_THETA ** (jnp.arange(0, MLA_DR, 2, dtype=F32) / MLA_DR))
    ang = pos[:, None] * inv[None, :]
    cos, sin = jnp.cos(ang), jnp.sin(ang)
    zero = jnp.zeros((seq, LANES - MLA_DR), F32)
    return (jnp.concatenate([cos, cos, zero], axis=1), jnp.concatenate([-sin, sin, zero], axis=1))


def _t5_bucket(rel):
    nb = NUM_BUCKETS // 2
    ret = (rel > 0).astype(jnp.int32) * nb
    n = jnp.abs(rel)
    max_exact = nb // 2
    nf = jnp.maximum(n, 1).astype(F32)
    large = max_exact + (jnp.log(nf / max_exact) / math.log(MAX_DISTANCE / max_exact)
                         * (nb - max_exact)).astype(jnp.int32)
    large = jnp.minimum(large, nb - 1)
    return ret + jnp.where(n < max_exact, n, large)


def _window_bias(rel_bias):
    qpos = jnp.arange(WINDOW, dtype=jnp.int32)
    jpos = jnp.arange(3 * WINDOW, dtype=jnp.int32)
    rel = jpos[None, :] - WINDOW - qpos[:, None]
    bucket = _t5_bucket(rel)
    bias = jnp.zeros((GQA_HEADS, WINDOW, 3 * WINDOW), F32)
    for b in range(NUM_BUCKETS):
        bias = bias + jnp.where((bucket == b)[None], rel_bias[b].astype(F32)[:, None, None], 0.0)
    bias = jnp.where((jnp.abs(rel) <= WINDOW)[None], bias, NEG_BIG)
    return bias.reshape(GQA_KV_HEADS, GQA_GROUP * WINDOW, 3 * WINDOW)


def _prep_layer(w_in, w_uq, w_ukv, w_out):
    half = MLA_DR // 2
    zpad = jnp.zeros((D_MODEL, LANES - MLA_DR), F32)
    kr0 = Q_LORA + KV_LORA
    k1 = w_in[:, kr0:kr0 + half]
    k2 = w_in[:, kr0 + half:kr0 + MLA_DR]
    w_in_p = jnp.concatenate(
        [w_in[:, :kr0], k1, k2, zpad, k2, k1, zpad, w_in[:, kr0 + MLA_DR:]], axis=1).astype(BF16)
    wq = w_uq.reshape(Q_LORA, MLA_HEADS, MLA_DN + MLA_DR)
    r1 = wq[:, :, MLA_DN:MLA_DN + half]
    r2 = wq[:, :, MLA_DN + half:]
    zq = jnp.zeros((Q_LORA, MLA_HEADS, LANES - MLA_DR), F32)
    w_uq_a = jnp.concatenate([wq[:, :, :MLA_DN], r1, r2, zq], axis=2).reshape(Q_LORA, MLA_HEADS * QK_PAD)
    w_uq_b = jnp.concatenate([r2, r1, zq], axis=2).reshape(Q_LORA, MLA_HEADS * LANES)
    out_a = MLA_HEADS * MLA_DV
    return (w_in_p, w_uq_a.astype(BF16), w_uq_b.astype(BF16), w_ukv.astype(BF16),
            w_out[:out_a].astype(BF16), w_out[out_a:].astype(BF16))


def _trunk(x, mods, layers, bias, g_final, *, batch, seq):
    tm = min(ROW_TILE, seq)
    cos_t, sin_t = _rope_tables(seq)
    for l, p in enumerate(layers):
        tk = min(MLA_TK, tm)
        q, k, v, gq, gk, gv = _premix(x, mods[l], p["g_norm_mix"], cos_t, sin_t, p["w_in_p"], p["g_q_lat"],
                                      p["w_uq_a"], p["w_uq_b"], p["g_kv_lat"], p["w_ukv"], seq=seq, tm=tm,
                                      vchunk=tk)
        tq_sub = min(MLA_TQ_SUB, seq)
        tq = min(MLA_TQ, seq)
        oa = _mla(q, k, v, batch=batch, seq=seq, tq=tq, tk=tk, nsub=tq // tq_sub)
        ob = _window(gq, gk, gv, bias, p["sink_col"], seq=seq, rows=min(ROW_TILE, seq))
        res = _postmix(oa, ob, x, mods[l], p["g_out_a"], p["g_out_b"], p["w_out_a"], p["w_out_b"],
                       p["g_norm_ffn"], p.get("w_router"), seq=seq, tm=tm)
        if "w_router" in p:
            assert l == len(layers) - 1
            x1, h2, topw, topi = res
            y2 = _moe_ffn(h2, topi[:, :2], p["w_gate"], p["w_up"], p["w_down"], tm=ROW_TILE, tf=FF_TILE)
            x = _combine(x1, y2, topw, mods[l], g_final, seq=seq, tm=tm)
        else:
            x1, h2 = res
            x = _ffn(x1, h2, mods[l], p["w_gate"], p["w_up"], p["w_down"], seq=seq, tm=tm, tf=FF_TILE)
    return x


def kernel(x_prompt, x_sample, c_prompt, c_sample, rel_bias, w_ada, b_ada, g_norm_mix, g_norm_ffn, w_in, g_q_lat, w_uq, g_kv_lat, w_ukv, sink, g_out_a, g_out_b, w_out, w_gate_d, w_up_d, w_down_d, w_router, w_gate_e, w_up_e, w_down_e, g_final):
    bp, sp, _ = x_prompt.shape
    bs, ss, _ = x_sample.shape
    rows = -(-(bp + bs) // 8) * 8
    c_all = jnp.concatenate([c_prompt, c_sample, jnp.zeros((rows - bp - bs, D_MODEL), F32)], axis=0)
    mod = _ada_mod(c_all, w_ada, b_ada)
    mod_p = [mod[l, :bp].reshape(bp, 6, D_MODEL) for l in range(DEPTH)]
    mod_s = [mod[l, bp:bp + bs].reshape(bs, 6, D_MODEL) for l in range(DEPTH)]
    bias = _window_bias(rel_bias)
    layers = []
    for l in range(DEPTH):
        w_in_p, w_uq_a, w_uq_b, w_ukv_b, w_out_a, w_out_b = _prep_layer(w_in[l], w_uq[l], w_ukv[l], w_out[l])
        out_a = MLA_HEADS * MLA_DV
        p = dict(
            g_norm_mix=g_norm_mix[l][None], g_norm_ffn=g_norm_ffn[l][None],
            w_in_p=w_in_p, g_q_lat=g_q_lat[l][None], w_uq_a=w_uq_a, w_uq_b=w_uq_b,
            g_kv_lat=g_kv_lat[l][None], w_ukv=w_ukv_b,
            sink_col=jnp.repeat(sink[l].astype(F32), WINDOW).reshape(GQA_KV_HEADS, GQA_GROUP * WINDOW, 1),
            g_out_a=g_out_a[l][None], g_out_b=g_out_b[l][None], w_out_a=w_out_a, w_out_b=w_out_b,
        )
        i = l // 2
        if l % 2 == 0:
            p.update(w_gate=w_gate_d[i].astype(BF16), w_up=w_up_d[i].astype(BF16), w_down=w_down_d[i].astype(BF16))
        else:
            wr = jnp.concatenate([w_router[i], jnp.zeros((D_MODEL, LANES - N_EXPERTS), F32)], axis=1)
            wr_hi = wr.astype(BF16)
            wr = jnp.stack([wr_hi, (wr - wr_hi.astype(F32)).astype(BF16)])
            p.update(w_gate=w_gate_e[i].astype(BF16), w_up=w_up_e[i].astype(BF16),
                     w_down=w_down_e[i].astype(BF16), w_router=wr)
        layers.append(p)
    gfin = g_final[None]
    y_s = _trunk(x_sample.reshape(bs * ss, D_MODEL), mod_s, layers, bias, gfin, batch=bs, seq=ss)
    y_p = _trunk(x_prompt.reshape(bp * sp, D_MODEL), mod_p, layers, bias, gfin, batch=bp, seq=sp)
    return (y_p.reshape(bp, sp, D_MODEL), y_s.reshape(bs, ss, D_MODEL))
```
